```python
import jax, jax.numpy as jnp
from jax import lax
import numpy as np

D_MODEL = 1024
BATCH = 32
SEQ = 256
DEPTH = 4
DEC_BATCH = 2
DEC_SEQ = 4096
PAST_LEN = 512

GRID_W = 64
N_MIXERS = 3
N_ATTN_LAYERS = (DEPTH + 2) // 3
N_RWKV_LAYERS = (DEPTH + 1) // 3
N_POOL_LAYERS = DEPTH // 3
MLA_HEADS = 16
Q_LORA_RANK = 384
KV_LORA_RANK = 256
QK_NOPE_DIM = 64
QK_ROPE_DIM = 32
V_HEAD_DIM = 64
ROPE_THETA = 10000.0
Q_BLOCK = 128
RWKV_HEAD = 64
RWKV_HEADS = D_MODEL // RWKV_HEAD
DECAY_LORA = 64
AAA_LORA = 64
GATE_LORA = 128
RWKV_GN_EPS = 64e-5
POOL_WINDOWS = (2, 4, 8, 16)
POOL_GROUP = D_MODEL // len(POOL_WINDOWS)
D_FF = -(-8 * D_MODEL // (3 * 256)) * 256
ALPHA = (2 * DEPTH) ** 0.25
BETA = (8 * DEPTH) ** -0.25
LN_EPS = 1e-5
RMS_EPS = 1e-6

kernel_name = 'hybrid_mla_rwkv7_pool_diffusion_step'


def layer_norm(x, g, b):
    xf = x.astype(jnp.float32)
    mu = xf.mean(-1, keepdims=True)
    var = jnp.square(xf - mu).mean(-1, keepdims=True)
    return ((xf - mu) * lax.rsqrt(var + LN_EPS) * g + b).astype(x.dtype)


def rms_norm(x, g):
    xf = x.astype(jnp.float32)
    return (xf * lax.rsqrt(jnp.square(xf).mean(-1, keepdims=True) + RMS_EPS) * g).astype(x.dtype)


def adaln(cond, w, b):
    m = (jax.nn.silu(cond) @ w + b)[..., None, :]
    return jnp.split(m, 6, axis=-1)


def modulate(x, shift, scale):
    return x * (1.0 + scale) + shift


def swiglu(h, wg, wu, wd):
    return (jax.nn.silu(h @ wg) * (h @ wu)) @ wd


def axial_rope(n_tokens, dtype):
    rows = n_tokens // GRID_W
    row = jnp.repeat(jnp.arange(rows, dtype=jnp.float32), GRID_W)
    col = jnp.tile(jnp.arange(GRID_W, dtype=jnp.float32), rows)
    n_freq = QK_ROPE_DIM // 4
    inv_freq = ROPE_THETA ** (-jnp.arange(n_freq, dtype=jnp.float32) / n_freq)
    ang = jnp.stack([row[:, None] * inv_freq, col[:, None] * inv_freq], axis=1)
    return jnp.cos(ang).astype(dtype), jnp.sin(ang).astype(dtype)


def rope_2d(x, cos, sin):
    xs = x.reshape(*x.shape[:-1], 2, 2, QK_ROPE_DIM // 4)
    x1, x2 = xs[..., 0, :], xs[..., 1, :]
    out = jnp.stack([x1 * cos - x2 * sin, x1 * sin + x2 * cos], axis=-2)
    return out.reshape(x.shape)


def mla_queries(h, wq_a, q_norm, wq_b):
    B, T, _ = h.shape
    q = (rms_norm(h @ wq_a, q_norm) @ wq_b).reshape(B, T, MLA_HEADS, QK_NOPE_DIM + QK_ROPE_DIM)
    return q[..., :QK_NOPE_DIM], q[..., QK_NOPE_DIM:]


def mla_compress_kv(h, wkv_a, kv_norm):
    kv = h @ wkv_a
    return rms_norm(kv[..., :KV_LORA_RANK], kv_norm), kv[..., KV_LORA_RANK:]


def mla_expand_kv(ckv, wkv_b):
    B, L, _ = ckv.shape
    kv = (ckv @ wkv_b).reshape(B, L, MLA_HEADS, QK_NOPE_DIM + V_HEAD_DIM)
    return kv[..., :QK_NOPE_DIM], kv[..., QK_NOPE_DIM:]


def mla_attend(q_nope, q_pe, k_nope, k_pe, v):
    B, T, H, _ = q_nope.shape
    nb = T // Q_BLOCK
    scale = (QK_NOPE_DIM + QK_ROPE_DIM) ** -0.5

    def blocks(t):
        return jnp.moveaxis(t.reshape(B, nb, Q_BLOCK, *t.shape[2:]), 1, 0)

    def one_block(qs):
        qn, qp = qs
        s = jnp.einsum('bqhd,bkhd->bhqk', qn, k_nope) + jnp.einsum('bqhr,bkr->bhqk', qp, k_pe)
        p = jax.nn.softmax(s.astype(jnp.float32) * scale, axis=-1).astype(v.dtype)
        return jnp.einsum('bhqk,bkhd->bqhd', p, v)

    o = lax.map(one_block, (blocks(q_nope), blocks(q_pe)))
    return jnp.moveaxis(o, 0, 1).reshape(B, T, H * V_HEAD_DIM)


def mla_context(h, wq_a, q_norm, wq_b, wkv_a, kv_norm, wkv_b, wo):
    q_nope, q_pe = mla_queries(h, wq_a, q_norm, wq_b)
    ckv, kpe = mla_compress_kv(h, wkv_a, kv_norm)
    k_nope, v = mla_expand_kv(ckv, wkv_b)
    return mla_attend(q_nope, q_pe, k_nope, kpe, v) @ wo, ckv, kpe


def mla_latent(h, ckv_ctx, kpe_ctx, cos, sin, wq_a, q_norm, wq_b, wkv_a, kv_norm, wkv_b, wo):
    q_nope, q_pe = mla_queries(h, wq_a, q_norm, wq_b)
    q_pe = rope_2d(q_pe, cos[:, None], sin[:, None])
    ckv, kpe = mla_compress_kv(h, wkv_a, kv_norm)
    kpe = rope_2d(kpe, cos, sin)
    k_nope, v = mla_expand_kv(jnp.concatenate([ckv_ctx, ckv], axis=1), wkv_b)
    k_pe = jnp.concatenate([kpe_ctx, kpe], axis=1)
    return mla_attend(q_nope, q_pe, k_nope, k_pe, v) @ wo


def split_heads(t):
    return t.reshape(*t.shape[:-1], RWKV_HEADS, RWKV_HEAD)


def token_shift_centred(x):
    prev = jnp.pad(x[:, :-1], ((0, 0), (1, 0), (0, 0)))
    nxt = jnp.pad(x[:, 1:], ((0, 0), (0, 1), (0, 0)))
    return 0.5 * (prev + nxt)


def wkv7_scan(r, w, k, v, a, b, s0, reverse):
    f32 = jnp.float32
    seq = tuple(jnp.moveaxis(t.astype(f32), 1, 0) for t in (r, w, k, v, a, b))

    def step(S, inp):
        rt, wt, kt, vt, at, bt = inp
        sa = jnp.einsum('bhvk,bhk->bhv', S, at)
        S = S * wt[:, :, None, :] + sa[..., None] * bt[:, :, None, :] + vt[..., None] * kt[:, :, None, :]
        return S, jnp.einsum('bhvk,bhk->bhv', S, rt)

    s_fin, y = lax.scan(step, s0.astype(f32), seq, reverse=reverse)
    return jnp.moveaxis(y, 0, 1), s_fin


def rwkv7_bidir(h, s0_f, s0_b, mu, wr, wk, wv, w0, w1, w2, a0, a1, a2, g1, g2,
                k_k, k_a, r_k, lnx_g, lnx_b, wo):
    B, T, D = h.shape
    xx = token_shift_centred(h) - h
    xr, xw, xk, xv, xa, xg = (h + xx * mu[i] for i in range(6))
    r = split_heads(xr @ wr)
    k = xk @ wk
    v = split_heads(xv @ wv)
    g = jax.nn.sigmoid(xg @ g1) @ g2
    kkf = split_heads(k * k_k).astype(jnp.float32)
    kk = kkf / jnp.maximum(jnp.sqrt(jnp.sum(kkf * kkf, -1, keepdims=True)), 1e-12)

    def direction(d, s0, reverse):
        logw = -jax.nn.softplus(-(w0[d] + jnp.tanh(xw @ w1[d]) @ w2[d]).astype(jnp.float32)) - 0.5
        decay = jnp.exp(-jnp.exp(logw))
        a = jax.nn.sigmoid(a0[d] + (xa @ a1[d]) @ a2[d])
        kd = k * (1.0 + (a - 1.0) * k_a)
        y, s = wkv7_scan(r, split_heads(decay), split_heads(kd), v, -kk, kk * split_heads(a), s0, reverse)
        return y, s, kd

    y_f, s_f, k_f = direction(0, s0_f, False)
    y_b, s_b, k_b = direction(1, s0_b, True)
    y = y_f + y_b
    mu_y = y.mean(-1, keepdims=True)
    var_y = jnp.square(y - mu_y).mean(-1, keepdims=True)
    yn = ((y - mu_y) * lax.rsqrt(var_y + RWKV_GN_EPS)).reshape(B, T, D) * lnx_g + lnx_b
    bonus = jnp.sum(r * split_heads(k_f + k_b) * r_k, -1, keepdims=True) * v
    out = (yn.astype(h.dtype) + bonus.reshape(B, T, D)) * g
    return out @ wo, s_f, s_b


def multiscale_pool(h, w_grp, scale):
    B, T, D = h.shape
    C = POOL_GROUP
    hf = h.astype(jnp.float32)
    cs = jnp.pad(lax.cumsum(hf, axis=1), ((0, 0), (1, 0), (0, 0)))
    t = jnp.arange(T)
    parts = []
    for gi, win in enumerate(POOL_WINDOWS):
        lo = jnp.clip(t - win // 2, 0, T)
        hi = jnp.clip(t - win // 2 + win, 0, T)
        sl = slice(gi * C, (gi + 1) * C)
        cnt = (hi - lo).astype(jnp.float32)[None, :, None]
        parts.append((cs[:, hi, sl] - cs[:, lo, sl]) / cnt - hf[:, :, sl])
    pooled = jnp.stack(parts, axis=2).astype(h.dtype)
    out = jnp.einsum('btgc,gcd->btgd', pooled, w_grp).reshape(B, T, D)
    return out * scale


def setup_inputs(seed: int = 0):
    key = jax.random.key(seed)
    keys = iter(jax.random.split(key, 64))
    f32 = jnp.float32
    D, H, N = D_MODEL, RWKV_HEADS, RWKV_HEAD
    NA, NB, NC = N_ATTN_LAYERS, N_RWKV_LAYERS, N_POOL_LAYERS
    G, C = len(POOL_WINDOWS), POOL_GROUP
    qk_dim = QK_NOPE_DIM + QK_ROPE_DIM

    def nrm(shape, scale):
        return scale * jax.random.normal(next(keys), shape, f32)

    def uni(shape, lo, hi):
        return jax.random.uniform(next(keys), shape, f32, lo, hi)

    return {
        'x_prompt': nrm((BATCH, SEQ, D), 1.0),
        'x_sample': nrm((DEC_BATCH, DEC_SEQ, D), 1.0),
        'cache_ckv': nrm((DEC_BATCH, NA, PAST_LEN, KV_LORA_RANK), 1.0),
        'cache_kpe': nrm((DEC_BATCH, NA, PAST_LEN, QK_ROPE_DIM), 1.0),
        'state_wkv': nrm((DEC_BATCH, NB, 2, H, N, N), 0.5),
        'c': nrm((DEC_BATCH, D), 1.0),
        'c_ctx': nrm((D,), 1.0),
        'ada_w': nrm((DEPTH, D, 6 * D), D ** -0.5),
        'ada_b': nrm((DEPTH, 6 * D), 0.02),
        'ln_g': 1.0 + nrm((DEPTH, 2, D), 0.1),
        'ln_b': nrm((DEPTH, 2, D), 0.02),
        'ffn_wg': nrm((DEPTH, D, D_FF), D ** -0.5),
        'ffn_wu': nrm((DEPTH, D, D_FF), D ** -0.5),
        'ffn_wd': nrm((DEPTH, D_FF, D), BETA * D_FF ** -0.5),
        'mla_wq_a': nrm((NA, D, Q_LORA_RANK), D ** -0.5),
        'mla_q_norm': 1.0 + nrm((NA, Q_LORA_RANK), 0.1),
        'mla_wq_b': nrm((NA, Q_LORA_RANK, MLA_HEADS * qk_dim), Q_LORA_RANK ** -0.5),
        'mla_wkv_a': nrm((NA, D, KV_LORA_RANK + QK_ROPE_DIM), D ** -0.5),
        'mla_kv_norm': 1.0 + nrm((NA, KV_LORA_RANK), 0.1),
        'mla_wkv_b': nrm((NA, KV_LORA_RANK, MLA_HEADS * (QK_NOPE_DIM + V_HEAD_DIM)), KV_LORA_RANK ** -0.5),
        'mla_wo': nrm((NA, MLA_HEADS * V_HEAD_DIM, D), BETA * (MLA_HEADS * V_HEAD_DIM) ** -0.5),
        'rwkv_mu': uni((NB, 6, D), 0.0, 1.0),
        'rwkv_wr': nrm((NB, D, D), D ** -0.5),
        'rwkv_wk': nrm((NB, D, D), D ** -0.5),
        'rwkv_wv': nrm((NB, D, D), D ** -0.5),
        'rwkv_w0': uni((NB, 2, D), -6.0, 0.0),
        'rwkv_w1': nrm((NB, 2, D, DECAY_LORA), D ** -0.5),
        'rwkv_w2': nrm((NB, 2, DECAY_LORA, D), 0.5 * DECAY_LORA ** -0.5),
        'rwkv_a0': nrm((NB, 2, D), 0.1),
        'rwkv_a1': nrm((NB, 2, D, AAA_LORA), D ** -0.5),
        'rwkv_a2': nrm((NB, 2, AAA_LORA, D), AAA_LORA ** -0.5),
        'rwkv_g1': nrm((NB, D, GATE_LORA), D ** -0.5),
        'rwkv_g2': nrm((NB, GATE_LORA, D), GATE_LORA ** -0.5),
        'rwkv_k_k': 0.85 + nrm((NB, D), 0.05),
        'rwkv_k_a': 1.0 + nrm((NB, D), 0.05),
        'rwkv_r_k': nrm((NB, H, N), 0.1),
        'rwkv_lnx_g': 1.0 + nrm((NB, D), 0.1),
        'rwkv_lnx_b': nrm((NB, D), 0.02),
        'rwkv_wo': nrm((NB, D, D), BETA * D ** -0.5),
        'pool_w': nrm((NC, G, C, C), BETA * C ** -0.5),
        'pool_scale': 1.0 + nrm((NC, D), 0.1),
    }


def reference(x_prompt, x_sample, cache_ckv, cache_kpe, state_wkv, c, c_ctx,
              ada_w, ada_b, ln_g, ln_b, ffn_wg, ffn_wu, ffn_wd,
              mla_wq_a, mla_q_norm, mla_wq_b, mla_wkv_a, mla_kv_norm, mla_wkv_b, mla_wo,
              rwkv_mu, rwkv_wr, rwkv_wk, rwkv_wv, rwkv_w0, rwkv_w1, rwkv_w2,
              rwkv_a0, rwkv_a1, rwkv_a2, rwkv_g1, rwkv_g2, rwkv_k_k, rwkv_k_a, rwkv_r_k,
              rwkv_lnx_g, rwkv_lnx_b, rwkv_wo, pool_w, pool_scale):
    cos, sin = axial_rope(x_sample.shape[1], x_sample.dtype)
    yp, ys = x_prompt, x_sample
    new_ckv, new_kpe, new_wkv = [], [], []
    for layer in range(DEPTH):
        kind, j = layer % N_MIXERS, layer // N_MIXERS
        mp = adaln(c_ctx, ada_w[layer], ada_b[layer])
        ms = adaln(c, ada_w[layer], ada_b[layer])
        hp = modulate(yp, mp[0], mp[1])
        hs = modulate(ys, ms[0], ms[1])
        if kind == 0:
            wts = (mla_wq_a[j], mla_q_norm[j], mla_wq_b[j], mla_wkv_a[j], mla_kv_norm[j], mla_wkv_b[j], mla_wo[j])
            op, ckv, kpe = mla_context(hp, *wts)
            os_ = mla_latent(hs, cache_ckv[:, j], cache_kpe[:, j], cos, sin, *wts)
            new_ckv.append(ckv)
            new_kpe.append(kpe)
        elif kind == 1:
            wts = (rwkv_mu[j], rwkv_wr[j], rwkv_wk[j], rwkv_wv[j], rwkv_w0[j], rwkv_w1[j], rwkv_w2[j],
                   rwkv_a0[j], rwkv_a1[j], rwkv_a2[j], rwkv_g1[j], rwkv_g2[j], rwkv_k_k[j], rwkv_k_a[j],
                   rwkv_r_k[j], rwkv_lnx_g[j], rwkv_lnx_b[j], rwkv_wo[j])
            zeros = jnp.zeros((yp.shape[0], RWKV_HEADS, RWKV_HEAD, RWKV_HEAD), jnp.float32)
            op, s_f, s_b = rwkv7_bidir(hp, zeros, zeros, *wts)
            os_, _, _ = rwkv7_bidir(hs, state_wkv[:, j, 0], state_wkv[:, j, 1], *wts)
            new_wkv.append(jnp.stack([s_f, s_b], axis=1).astype(yp.dtype))
        else:
            op = multiscale_pool(hp, pool_w[j], pool_scale[j])
            os_ = multiscale_pool(hs, pool_w[j], pool_scale[j])
        yp = layer_norm(ALPHA * yp + mp[2] * op, ln_g[layer, 0], ln_b[layer, 0])
        ys = layer_norm(ALPHA * ys + ms[2] * os_, ln_g[layer, 0], ln_b[layer, 0])
        fp = swiglu(modulate(yp, mp[3], mp[4]), ffn_wg[layer], ffn_wu[layer], ffn_wd[layer])
        fs = swiglu(modulate(ys, ms[3], ms[4]), ffn_wg[layer], ffn_wu[layer], ffn_wd[layer])
        yp = layer_norm(ALPHA * yp + mp[5] * fp, ln_g[layer, 1], ln_b[layer, 1])
        ys = layer_norm(ALPHA * ys + ms[5] * fs, ln_g[layer, 1], ln_b[layer, 1])
    return (yp, ys, jnp.stack(new_ckv, axis=1), jnp.stack(new_kpe, axis=1), jnp.stack(new_wkv, axis=1))
```

```python
import functools

import jax
import jax.numpy as jnp
import numpy as np
from jax import lax
from jax.experimental import pallas as pl
from jax.experimental.pallas import tpu as pltpu

F32 = jnp.float32
BF16 = jnp.bfloat16

D = 1024
BATCH, SEQ = 32, 256
DEC_BATCH, DEC_SEQ = 2, 4096
PAST = 512
DEPTH = 4
GRID_W = 64
HEADS = 16
Q_LORA, KV_LORA = 384, 256
NOPE, ROPE, VDIM = 64, 32, 64
ROPE_THETA = 10000.0
RWKV_N = 64
RWKV_GN_EPS = 64e-5
POOL_WINDOWS = (2, 4, 8, 16)
POOL_C = D // 4
D_FF = 2816
ALPHA = (2 * DEPTH) ** 0.25
LN_EPS = 1e-5
RMS_EPS = 1e-6

N_PROMPT = BATCH * SEQ
N_SAMPLE = DEC_BATCH * DEC_SEQ
N_TOK = N_PROMPT + N_SAMPLE

LANES = 128
SUBLANES = 8
HEAD_PAD = 128
VMEM_LIMIT = 56 * 1024 * 1024

TM = 512
TMIX = 256
HALO = SUBLANES
TQ = 256
CHUNK = 64
TFF = 1408


def _cparams(sem):
    return pltpu.CompilerParams(dimension_semantics=sem, vmem_limit_bytes=VMEM_LIMIT)


def _dot(a, b):
    return jnp.dot(a, b, preferred_element_type=F32)


def _dot_nt(a, b):
    return lax.dot_general(a, b, (((1,), (1,)), ((), ())), preferred_element_type=F32)


def _dot_tn(a, b):
    return lax.dot_general(a, b, (((0,), (0,)), ((), ())), preferred_element_type=F32)


def _dot_f32(a, b):
    return jnp.dot(a, b, preferred_element_type=F32, precision=lax.Precision.HIGHEST)


def _layer_norm(x, g, b):
    mu = jnp.mean(x, axis=-1, keepdims=True)
    xc = x - mu
    var = jnp.mean(xc * xc, axis=-1, keepdims=True)
    return xc * lax.rsqrt(var + LN_EPS) * g + b


def _rms_norm(x, g):
    return x * lax.rsqrt(jnp.mean(x * x, axis=-1, keepdims=True) + RMS_EPS) * g


def _silu(x):
    return x * jax.nn.sigmoid(x)


def _cond_of_tile(i, tile):
    first_sample = N_PROMPT // tile
    per_seq = DEC_SEQ // tile
    return jnp.where(i < first_sample, 0, 1 + (i - first_sample) // per_seq)


def _seg_sum(x, ones_bd):
    outs = []
    for g in range(D // 256):
        xg = x[:, 256 * g:256 * (g + 1)]
        hi = xg.astype(BF16)
        r1 = xg - hi.astype(F32)
        mid = r1.astype(BF16)
        lo = (r1 - mid.astype(F32)).astype(BF16)
        outs.append(_dot(hi, ones_bd) + _dot(mid, ones_bd) + _dot(lo, ones_bd))
    return jnp.concatenate(outs, axis=1)


def _ada_kernel(c_ref, w_ref, b_ref, o_ref):
    a = _silu(c_ref[...]).astype(BF16)
    o_ref[...] = _dot(a, w_ref[...].astype(BF16)) + b_ref[...]


def _ada_call(cond8, ada_w, ada_b):
    tn = 1536
    return pl.pallas_call(
        _ada_kernel,
        grid=(DEPTH, 6 * D // tn),
        in_specs=[
            pl.BlockSpec((SUBLANES, D), lambda l, n: (0, 0)),
            pl.BlockSpec((None, D, tn), lambda l, n: (l, 0, n)),
            pl.BlockSpec((None, 1, tn), lambda l, n: (l, 0, n)),
        ],
        out_specs=pl.BlockSpec((None, SUBLANES, tn), lambda l, n: (l, 0, n)),
        out_shape=jax.ShapeDtypeStruct((DEPTH, SUBLANES, 6 * D), F32),
        compiler_params=_cparams(("parallel", "parallel")),
        name="adaln",
    )(cond8, ada_w, ada_b.reshape(DEPTH, 1, 6 * D))


def _mla_pre_kernel(y_ref, m_ref, wqa_ref, qn_ref, wqb_ref, wqbs_ref, wkva_ref, wkvas_ref,
                    kvn_ref, cos_ref, sin_ref, q_ref, ckv_ref, kpe_ref):
    i = pl.program_id(0)
    h = (y_ref[...] * (1.0 + m_ref[1:2, :]) + m_ref[0:1, :]).astype(BF16)
    qa = _rms_norm(_dot(h, wqa_ref[...]), qn_ref[...]).astype(BF16)
    kv = _dot(h, wkva_ref[...])
    ckv_ref[...] = _rms_norm(kv[:, :KV_LORA], kvn_ref[...])
    kpe = kv[:, KV_LORA:]
    is_latent = i >= N_PROMPT // TM

    @pl.when(jnp.logical_not(is_latent))
    def _():
        q_ref[...] = _dot(qa, wqb_ref[...]).astype(BF16)
        kpe_ref[...] = kpe

    @pl.when(is_latent)
    def _():
        cos = cos_ref[...]
        sin = sin_ref[...]
        kpe_ref[...] = kpe * cos + _dot(h, wkvas_ref[...]) * sin
        q = _dot(qa, wqb_ref[...])
        qs = _dot(qa, wqbs_ref[...])
        for hd in range(HEADS):
            sl = slice(HEAD_PAD * hd, HEAD_PAD * (hd + 1))
            q_ref[:, sl] = (q[:, sl] * cos + qs[:, sl] * sin).astype(BF16)


def _mla_pre_call(y, mods, w):
    first_sample = N_PROMPT // TM
    per_seq = DEC_SEQ // TM
    const = lambda i: (0, 0)
    rope_map = lambda i: (jnp.where(i < first_sample, 0, (i - first_sample) % per_seq), 0)
    return pl.pallas_call(
        _mla_pre_kernel,
        grid=(N_TOK // TM,),
        in_specs=[
            pl.BlockSpec((TM, D), lambda i: (i, 0)),
            pl.BlockSpec((None, SUBLANES, D), lambda i: (_cond_of_tile(i, TM), 0, 0)),
            pl.BlockSpec((D, Q_LORA), const),
            pl.BlockSpec((1, Q_LORA), const),
            pl.BlockSpec((Q_LORA, HEADS * HEAD_PAD), const),
            pl.BlockSpec((Q_LORA, HEADS * HEAD_PAD), const),
            pl.BlockSpec((D, KV_LORA + HEAD_PAD), const),
            pl.BlockSpec((D, HEAD_PAD), const),
            pl.BlockSpec((1, KV_LORA), const),
            pl.BlockSpec((TM, HEAD_PAD), rope_map),
            pl.BlockSpec((TM, HEAD_PAD), rope_map),
        ],
        out_specs=[
            pl.BlockSpec((TM, HEADS * HEAD_PAD), lambda i: (i, 0)),
            pl.BlockSpec((TM, KV_LORA), lambda i: (i, 0)),
            pl.BlockSpec((TM, HEAD_PAD), lambda i: (i, 0)),
        ],
        out_shape=[
            jax.ShapeDtypeStruct((N_TOK, HEADS * HEAD_PAD), BF16),
            jax.ShapeDtypeStruct((N_TOK, KV_LORA), F32),
            jax.ShapeDtypeStruct((N_TOK, HEAD_PAD), F32),
        ],
        compiler_params=_cparams(("parallel",)),
        name="mla_pre",
    )(y, mods, w["wq_a"], w["q_norm"], w["wq_b"], w["wq_b_sw"], w["wkv_a"], w["wkv_a_sw"],
      w["kv_norm"], w["cos"], w["sin"])


def _kv_expand_kernel(ckv_ref, kpe_ref, wk_ref, wv_ref, k_ref, v_ref):
    c = ckv_ref[...].astype(BF16)
    kpe = kpe_ref[...]
    kn = _dot(c, wk_ref[...])
    for hd in range(HEADS):
        sl = slice(HEAD_PAD * hd, HEAD_PAD * (hd + 1))
        k_ref[:, sl] = (kn[:, sl] + kpe).astype(BF16)
    v_ref[...] = _dot(c, wv_ref[...]).astype(BF16)


def _kv_expand_call(ckv_all, kpe_all, w):
    rows = ckv_all.shape[0]
    const = lambda i: (0, 0)
    return pl.pallas_call(
        _kv_expand_kernel,
        grid=(rows // TM,),
        in_specs=[
            pl.BlockSpec((TM, KV_LORA), lambda i: (i, 0)),
            pl.BlockSpec((TM, HEAD_PAD), lambda i: (i, 0)),
            pl.BlockSpec((KV_LORA, HEADS * HEAD_PAD), const),
            pl.BlockSpec((KV_LORA, HEADS * HEAD_PAD), const),
        ],
        out_specs=[
            pl.BlockSpec((TM, HEADS * HEAD_PAD), lambda i: (i, 0)),
            pl.BlockSpec((TM, HEADS * HEAD_PAD), lambda i: (i, 0)),
        ],
        out_shape=[jax.ShapeDtypeStruct((rows, HEADS * HEAD_PAD), BF16)] * 2,
        compiler_params=_cparams(("parallel",)),
        name="kv_expand",
    )(ckv_all, kpe_all, w["wkv_b_k"], w["wkv_b_v"])


def _attn_kernel(q_ref, k_ref, v_ref, o_ref):
    scale = (NOPE + ROPE) ** -0.5
    acc = None
    for e in range(2):
        sl = slice(HEAD_PAD * e, HEAD_PAD * (e + 1))
        s = _dot_nt(q_ref[:, sl], k_ref[:, sl])
        m = jnp.max(s, axis=-1, keepdims=True)
        p = jnp.exp((s - m) * scale)
        l = jnp.sum(p, axis=-1, keepdims=True)
        pv = _dot(p.astype(BF16), v_ref[:, sl]) * (1.0 / l)
        acc = pv if acc is None else acc + pv
    o_ref[...] = acc.astype(BF16)


def _attn_call(q, k, v, n_seq, t_q, t_k, q_row0, k_row0, name):
    nq = t_q // TQ
    q_blk0 = q_row0 // TQ
    k_blk0 = k_row0 // t_k
    return pl.pallas_call(
        _attn_kernel,
        grid=(n_seq, HEADS // 2, nq),
        in_specs=[
            pl.BlockSpec((TQ, 2 * HEAD_PAD), lambda b, p, i: (q_blk0 + b * nq + i, p)),
            pl.BlockSpec((t_k, 2 * HEAD_PAD), lambda b, p, i: (k_blk0 + b, p)),
            pl.BlockSpec((t_k, 2 * HEAD_PAD), lambda b, p, i: (k_blk0 + b, p)),
        ],
        out_specs=pl.BlockSpec((TQ, 2 * VDIM), lambda b, p, i: (b * nq + i, p)),
        out_shape=jax.ShapeDtypeStruct((n_seq * t_q, HEADS * VDIM), BF16),
        compiler_params=_cparams(("parallel", "parallel", "arbitrary")),
        name=name,
    )(q, k, v)


def _post_kernel(has_proj, y_ref, a_ref, m_ref, lng_ref, lnb_ref, *rest):
    if has_proj:
        wo_ref, wg_ref, wu_ref, wd_ref, o_ref, y1_ref, h_ref, acc_ref = rest
    else:
        wg_ref, wu_ref, wd_ref, o_ref, y1_ref, h_ref, acc_ref = rest
    k = pl.program_id(1)

    @pl.when(k == 0)
    def _():
        mix = _dot(a_ref[...], wo_ref[...]) if has_proj else a_ref[...]
        y1 = _layer_norm(ALPHA * y_ref[...] + m_ref[2:3, :] * mix, lng_ref[0:1, :], lnb_ref[0:1, :])
        y1_ref[...] = y1
        h_ref[...] = (y1 * (1.0 + m_ref[4:5, :]) + m_ref[3:4, :]).astype(BF16)

    h = h_ref[...]
    act = (_silu(_dot(h, wg_ref[...])) * _dot(h, wu_ref[...])).astype(BF16)
    part = _dot(act, wd_ref[...])

    @pl.when(k == 0)
    def _():
        acc_ref[...] = part

    @pl.when(k > 0)
    def _():
        acc_ref[...] += part

    @pl.when(k == pl.num_programs(1) - 1)
    def _():
        o_ref[...] = _layer_norm(ALPHA * y1_ref[...] + m_ref[5:6, :] * acc_ref[...],
                                 lng_ref[1:2, :], lnb_ref[1:2, :])


def _post_call(y, a, mods, ln_g, ln_b, wo, wg, wu, wd):
    has_proj = wo is not None
    const = lambda i, k: (0, 0)
    in_specs = [
        pl.BlockSpec((TM, D), lambda i, k: (i, 0)),
        pl.BlockSpec((TM, D), lambda i, k: (i, 0)),
        pl.BlockSpec((None, SUBLANES, D), lambda i, k: (_cond_of_tile(i, TM), 0, 0)),
        pl.BlockSpec((2, D), const),
        pl.BlockSpec((2, D), const),
    ]
    args = [y, a, mods, ln_g, ln_b]
    if has_proj:
        in_specs.append(pl.BlockSpec((D, D), const))
        args.append(wo)
    in_specs += [
        pl.BlockSpec((D, TFF), lambda i, k: (0, k)),
        pl.BlockSpec((D, TFF), lambda i, k: (0, k)),
        pl.BlockSpec((TFF, D), lambda i, k: (k, 0)),
    ]
    args += [wg, wu, wd]
    return pl.pallas_call(
        functools.partial(_post_kernel, has_proj),
        grid=(N_TOK // TM, D_FF // TFF),
        in_specs=in_specs,
        out_specs=pl.BlockSpec((TM, D), lambda i, k: (i, 0)),
        out_shape=jax.ShapeDtypeStruct((N_TOK, D), F32),
        scratch_shapes=[pltpu.VMEM((TM, D), F32), pltpu.VMEM((TM, D), BF16), pltpu.VMEM((TM, D), F32)],
        compiler_params=_cparams(("parallel", "arbitrary")),
        name="post_proj" if has_proj else "post_noproj",
    )(*args)


def _mix_tile_flags(i):
    first_sample = N_PROMPT // TMIX
    tiles_prompt = SEQ // TMIX
    tiles_sample = DEC_SEQ // TMIX
    is_p = i < first_sample
    j = jnp.where(is_p, i % tiles_prompt, (i - first_sample) % tiles_sample)
    n = jnp.where(is_p, tiles_prompt, tiles_sample)
    return j > 0, j < n - 1, j * TMIX, n * TMIX


def _halo_specs():
    blocks_per_tile = TMIX // HALO
    last = N_TOK // HALO - 1
    return [
        pl.BlockSpec((HALO, D), lambda i: (jnp.maximum(i * blocks_per_tile - 1, 0), 0)),
        pl.BlockSpec((TMIX, D), lambda i: (i, 0)),
        pl.BlockSpec((HALO, D), lambda i: (jnp.minimum((i + 1) * blocks_per_tile, last), 0)),
    ]


def _fill_ext(ext_ref, yp_ref, y_ref, yn_ref, m_ref, has_prev, has_next):
    scale1 = 1.0 + m_ref[1:2, :]
    shift = m_ref[0:1, :]
    h = y_ref[...] * scale1 + shift
    ext_ref[0:HALO, :] = jnp.where(has_prev, yp_ref[...] * scale1 + shift, 0.0)
    ext_ref[HALO:HALO + TMIX, :] = h
    ext_ref[HALO + TMIX:, :] = jnp.where(has_next, yn_ref[...] * scale1 + shift, 0.0)
    return h


def _rwkv_pre_kernel(yp_ref, y_ref, yn_ref, m_ref, mu_ref, wr_ref, wk_ref, wv_ref, g1_ref, g2_ref,
                     w1_ref, w2_ref, a1_ref, a2_ref, w0_ref, a0_ref, kk_ref, ka_ref, ones_ref,
                     r_out, v_out, g_out, kk_out, kd_out, b_out, ld_out, ext_ref):
    i = pl.program_id(0)
    has_prev, has_next, _, _ = _mix_tile_flags(i)
    h = _fill_ext(ext_ref, yp_ref, y_ref, yn_ref, m_ref, has_prev, has_next)
    prev = ext_ref[HALO - 1:HALO - 1 + TMIX, :]
    nxt = ext_ref[HALO + 1:HALO + 1 + TMIX, :]
    xx = 0.5 * (prev + nxt) - h

    def mix(j):
        return (h + xx * mu_ref[j:j + 1, :]).astype(BF16)

    r_out[...] = _dot(mix(0), wr_ref[...])
    k = _dot(mix(2), wk_ref[...])
    v_out[...] = _dot(mix(3), wv_ref[...])
    g_out[...] = _dot(jax.nn.sigmoid(_dot(mix(5), g1_ref[...])).astype(BF16), g2_ref[...])
    tw = jnp.tanh(_dot(mix(1), w1_ref[...])).astype(BF16)
    ta = _dot(mix(4), a1_ref[...]).astype(BF16)

    kkf = k * kk_ref[...]
    nrm = jnp.sqrt(_seg_sum(kkf * kkf, ones_ref[...]))
    kk = kkf / jnp.maximum(nrm, 1e-12)
    kk_out[...] = kk
    for d in range(2):
        z = -(w0_ref[d:d + 1, :] + _dot(tw, w2_ref[d]))
        softplus = jnp.maximum(z, 0.0) + jnp.log(1.0 + jnp.exp(-jnp.abs(z)))
        ld_out[d] = -jnp.exp(-softplus - 0.5)
        a = jax.nn.sigmoid(a0_ref[d:d + 1, :] + _dot(ta, a2_ref[d]))
        kd_out[d] = k * (1.0 + (a - 1.0) * ka_ref[...])
        b_out[d] = kk * a


def _rwkv_pre_call(y, mods, w):
    const2 = lambda i: (0, 0)
    const3 = lambda i: (0, 0, 0)
    row = pl.BlockSpec((TMIX, D), lambda i: (i, 0))
    row2 = pl.BlockSpec((2, TMIX, D), lambda i: (0, i, 0))
    one = jax.ShapeDtypeStruct((N_TOK, D), F32)
    two = jax.ShapeDtypeStruct((2, N_TOK, D), F32)
    return pl.pallas_call(
        _rwkv_pre_kernel,
        grid=(N_TOK // TMIX,),
        in_specs=_halo_specs() + [
            pl.BlockSpec((None, SUBLANES, D), lambda i: (_cond_of_tile(i, TMIX), 0, 0)),
            pl.BlockSpec((6, D), const2),
            pl.BlockSpec((D, D), const2), pl.BlockSpec((D, D), const2), pl.BlockSpec((D, D), const2),
            pl.BlockSpec((D, 128), const2), pl.BlockSpec((128, D), const2),
            pl.BlockSpec((D, 128), const2), pl.BlockSpec((2, 128, D), const3),
            pl.BlockSpec((D, 128), const2), pl.BlockSpec((2, 128, D), const3),
            pl.BlockSpec((2, D), const2), pl.BlockSpec((2, D), const2),
            pl.BlockSpec((1, D), const2), pl.BlockSpec((1, D), const2),
            pl.BlockSpec((256, 256), const2),
        ],
        out_specs=[row, row, row, row, row2, row2, row2],
        out_shape=[one, one, one, one, two, two, two],
        scratch_shapes=[pltpu.VMEM((TMIX + 2 * HALO, D), F32)],
        compiler_params=_cparams(("parallel",)),
        name="rwkv_pre",
    )(y, y, y, mods, w["mu"], w["wr"], w["wk"], w["wv"], w["g1"], w["g2"], w["w1"], w["w2"],
      w["a1"], w["a2"], w["w0"], w["a0"], w["k_k"], w["k_a"], w["ones_bd"])


def _scan_kernel(n_chunks, r_ref, v_ref, kk_ref, kd_ref, b_ref, ld_ref, s0_ref, y_ref, sf_ref, s_ref):
    d = pl.program_id(0)
    n = pl.program_id(2)
    C = CHUNK
    P2 = 2 * C
    rev = d == 1

    @pl.when(n == 0)
    def _():
        s_ref[...] = jnp.zeros_like(s_ref)
        for hd in range(HEADS):
            o = RWKV_N * (hd % 2)
            s_ref[hd // 2, o:o + RWKV_N, o:o + RWKV_N] = s0_ref[hd]

    sign = jnp.where(rev, -1, 1)
    ti = lax.broadcasted_iota(jnp.int32, (C, C), 0)
    si = lax.broadcasted_iota(jnp.int32, (C, C), 1)
    cum = (sign * (si - ti) <= 0).astype(F32)
    ld = ld_ref[...]
    cs = _dot_f32(cum, ld)
    last = jnp.where(rev, cs[0:1, :], cs[C - 1:C, :])
    pin = jnp.exp(cs)
    pinv = jnp.exp(-cs)
    at = -kk_ref[...] * jnp.exp(cs - ld)
    rt = r_ref[...] * pin
    bt = b_ref[...] * pinv
    kt = kd_ref[...] * pinv
    tail = jnp.exp(last - cs)
    bh = b_ref[...] * tail
    kh = kd_ref[...] * tail
    pc = jnp.exp(last)
    v = v_ref[...]

    lane = lax.broadcasted_iota(jnp.int32, (C, 2 * RWKV_N), 1)
    even = lane < RWKV_N
    r2 = lax.broadcasted_iota(jnp.int32, (P2, P2), 0)
    c2 = lax.broadcasted_iota(jnp.int32, (P2, P2), 1)
    same = (r2 // C) == (c2 // C)
    order = sign * (c2 % C - r2 % C)
    m_incl = same & (order <= 0)
    m_strict = same & (order < 0)
    eye = (r2 == c2).astype(F32)

    def split(x):
        return jnp.concatenate([jnp.where(even, x, 0.0), jnp.where(even, 0.0, x)], axis=0)

    def twice(x):
        return jnp.concatenate([x, x], axis=0)

    for p in range(HEADS // 2):
        sl = slice(2 * RWKV_N * p, 2 * RWKV_N * (p + 1))
        s0 = s_ref[p]
        a2, q2 = split(at[:, sl]), split(rt[:, sl])
        b2, k2 = twice(bt[:, sl]), twice(kt[:, sl])
        v2 = split(v[:, sl])
        lhs = jnp.concatenate([a2, q2], axis=0).astype(BF16)
        rhs = jnp.concatenate([b2, k2], axis=0).astype(BF16)
        g = _dot_nt(lhs, rhs)
        a_ab = jnp.where(m_strict, g[:P2, :P2], 0.0)
        a_ak = jnp.where(m_strict, g[:P2, P2:], 0.0)
        m_rb = jnp.where(m_incl, g[P2:, :P2], 0.0)
        m_rk = jnp.where(m_incl, g[P2:, P2:], 0.0)
        pw = a_ab
        inv = eye + a_ab
        for _ in range(int(np.log2(C)) - 1):
            pw = _dot_f32(pw, pw)
            inv = inv + _dot_f32(inv, pw)
        xs = _dot_nt(lhs, s0.astype(BF16))
        v2b = v2.astype(BF16)
        u2 = _dot_f32(inv, xs[:P2] + _dot(a_ak.astype(BF16), v2b))
        u2b = u2.astype(BF16)
        y2 = xs[P2:] + _dot(m_rb.astype(BF16), u2b) + _dot(m_rk.astype(BF16), v2b)
        y_ref[:, sl] = y2[:C] + y2[C:]
        bh2 = split(bh[:, sl]).astype(BF16)
        kh2 = split(kh[:, sl]).astype(BF16)
        s_ref[p] = s0 * pc[:, sl] + _dot_tn(u2b, bh2) + _dot_tn(v2b, kh2)

    @pl.when(n == n_chunks - 1)
    def _():
        for hd in range(HEADS):
            o = RWKV_N * (hd % 2)
            sf_ref[hd] = s_ref[hd // 2, o:o + RWKV_N, o:o + RWKV_N]


def _scan_call(r, v, kk, kd, b, ld, s0, n_seq, t_seq, row0, name):
    n_chunks = t_seq // CHUNK
    blk0 = row0 // CHUNK

    def chunk(dd, bb, nn):
        return blk0 + bb * n_chunks + jnp.where(dd == 0, nn, n_chunks - 1 - nn)

    row = pl.BlockSpec((CHUNK, D), lambda dd, bb, nn: (chunk(dd, bb, nn), 0))
    row2 = pl.BlockSpec((None, CHUNK, D), lambda dd, bb, nn: (dd, chunk(dd, bb, nn), 0))
    st = pl.BlockSpec((None, None, HEADS, RWKV_N, RWKV_N), lambda dd, bb, nn: (dd, bb, 0, 0, 0))
    return pl.pallas_call(
        functools.partial(_scan_kernel, n_chunks),
        grid=(2, n_seq, n_chunks),
        in_specs=[row, row, row, row2, row2, row2, st],
        out_specs=[
            pl.BlockSpec((None, CHUNK, D), lambda dd, bb, nn: (dd, chunk(dd, bb, nn) - blk0, 0)),
            st,
        ],
        out_shape=[
            jax.ShapeDtypeStruct((2, n_seq * t_seq, D), F32),
            jax.ShapeDtypeStruct((2, n_seq, HEADS, RWKV_N, RWKV_N), F32),
        ],
        scratch_shapes=[pltpu.VMEM((HEADS // 2, 2 * RWKV_N, 2 * RWKV_N), F32)],
        compiler_params=_cparams(("parallel", "parallel", "arbitrary")),
        name=name,
    )(r, v, kk, kd, b, ld, s0)


def _rwkv_mid_kernel(y_ref, r_ref, kd_ref, v_ref, g_ref, rk_ref, lg_ref, lb_ref, ones_ref, o_ref):
    ones = ones_ref[...]
    y = y_ref[0] + y_ref[1]
    mu = _seg_sum(y, ones) * (1.0 / RWKV_N)
    yc = y - mu
    var = _seg_sum(yc * yc, ones) * (1.0 / RWKV_N)
    yn = yc * lax.rsqrt(var + RWKV_GN_EPS) * lg_ref[...] + lb_ref[...]
    bonus = _seg_sum(r_ref[...] * (kd_ref[0] + kd_ref[1]) * rk_ref[...], ones) * v_ref[...]
    o_ref[...] = ((yn + bonus) * g_ref[...]).astype(BF16)


def _rwkv_mid_call(y2, r, kd, v, g, w):
    const2 = lambda i: (0, 0)
    row = pl.BlockSpec((TM, D), lambda i: (i, 0))
    row2 = pl.BlockSpec((2, TM, D), lambda i: (0, i, 0))
    return pl.pallas_call(
        _rwkv_mid_kernel,
        grid=(N_TOK // TM,),
        in_specs=[row2, row, row2, row, row,
                  pl.BlockSpec((1, D), const2), pl.BlockSpec((1, D), const2), pl.BlockSpec((1, D), const2),
                  pl.BlockSpec((256, 256), const2)],
        out_specs=row,
        out_shape=jax.ShapeDtypeStruct((N_TOK, D), BF16),
        compiler_params=_cparams(("parallel",)),
        name="rwkv_mid",
    )(y2, r, kd, v, g, w["r_k"], w["lnx_g"], w["lnx_b"], w["ones_bd"])


def _pool_kernel(yp_ref, y_ref, yn_ref, m_ref, w_ref, sc_ref, o_ref, ext_ref):
    i = pl.program_id(0)
    has_prev, has_next, pos0, seq_len = _mix_tile_flags(i)
    _fill_ext(ext_ref, yp_ref, y_ref, yn_ref, m_ref, has_prev, has_next)
    t = pos0 + lax.broadcasted_iota(jnp.int32, (TMIX, 1), 0)
    for gi, win in enumerate(POOL_WINDOWS):
        cols = slice(POOL_C * gi, POOL_C * (gi + 1))
        acc = None
        for j in range(-(win // 2), win - win // 2):
            x = ext_ref[HALO + j:HALO + j + TMIX, cols]
            acc = x if acc is None else acc + x
        lo = jnp.maximum(t - win // 2, 0)
        hi = jnp.minimum(t - win // 2 + win, seq_len)
        cnt = (hi - lo).astype(F32)
        pooled = (acc / cnt - ext_ref[HALO:HALO + TMIX, cols]).astype(BF16)
        o_ref[:, cols] = _dot(pooled, w_ref[gi]) * sc_ref[:, cols]


def _pool_call(y, mods, pool_w, pool_scale):
    return pl.pallas_call(
        _pool_kernel,
        grid=(N_TOK // TMIX,),
        in_specs=_halo_specs() + [
            pl.BlockSpec((None, SUBLANES, D), lambda i: (_cond_of_tile(i, TMIX), 0, 0)),
            pl.BlockSpec((4, POOL_C, POOL_C), lambda i: (0, 0, 0)),
            pl.BlockSpec((1, D), lambda i: (0, 0)),
        ],
        out_specs=pl.BlockSpec((TMIX, D), lambda i: (i, 0)),
        out_shape=jax.ShapeDtypeStruct((N_TOK, D), F32),
        scratch_shapes=[pltpu.VMEM((TMIX + 2 * HALO, D), F32)],
        compiler_params=_cparams(("parallel",)),
        name="pool",
    )(y, y, y, mods, pool_w, pool_scale)


def _rope_tables():
    rows = DEC_SEQ // GRID_W
    row = jnp.repeat(jnp.arange(rows, dtype=F32), GRID_W)
    col = jnp.tile(jnp.arange(GRID_W, dtype=F32), rows)
    n_freq = ROPE // 4
    inv_freq = ROPE_THETA ** (-jnp.arange(n_freq, dtype=F32) / n_freq)
    ang = jnp.stack([row[:, None] * inv_freq, col[:, None] * inv_freq], axis=1)
    cos, sin = jnp.cos(ang), jnp.sin(ang)
    cos32 = jnp.concatenate([cos, cos], axis=-1).reshape(DEC_SEQ, ROPE)
    sin32 = jnp.concatenate([-sin, sin], axis=-1).reshape(DEC_SEQ, ROPE)
    ones = jnp.ones((DEC_SEQ, NOPE), F32)
    zeros = jnp.zeros((DEC_SEQ, NOPE), F32)
    tail = HEAD_PAD - NOPE - ROPE
    cos_t = jnp.concatenate([ones, cos32, jnp.ones((DEC_SEQ, tail), F32)], axis=1)
    sin_t = jnp.concatenate([zeros, sin32, jnp.zeros((DEC_SEQ, tail), F32)], axis=1)
    return cos_t, sin_t


def _swap_halves(w):
    s = w.reshape(*w.shape[:-1], 2, 2, ROPE // 4)
    return s[..., ::-1, :].reshape(w.shape)


def _pad_heads(w, width, offset_fn):
    cols = [jnp.pad(w[:, h], ((0, 0), (offset_fn(h), HEAD_PAD - offset_fn(h) - width))) for h in range(HEADS)]
    return jnp.concatenate(cols, axis=1)


def _mla_weights(wq_a, q_norm, wq_b, wkv_a, kv_norm, wkv_b, wo, cos_t, sin_t):
    qk = NOPE + ROPE
    wq_b3 = wq_b.reshape(Q_LORA, HEADS, qk)
    q_sw = jnp.concatenate([jnp.zeros((Q_LORA, HEADS, NOPE), F32), _swap_halves(wq_b3[..., NOPE:])], axis=-1)
    w_pe = wkv_a[:, KV_LORA:]
    pad_pe = lambda x: jnp.pad(x, ((0, 0), (NOPE, HEAD_PAD - NOPE - ROPE)))
    wkv_b3 = wkv_b.reshape(KV_LORA, HEADS, NOPE + VDIM)
    return {
        "wq_a": wq_a.astype(BF16),
        "q_norm": q_norm.reshape(1, Q_LORA),
        "wq_b": _pad_heads(wq_b3, qk, lambda h: 0).astype(BF16),
        "wq_b_sw": _pad_heads(q_sw, qk, lambda h: 0).astype(BF16),
        "wkv_a": jnp.concatenate([wkv_a[:, :KV_LORA], pad_pe(w_pe)], axis=1).astype(BF16),
        "wkv_a_sw": pad_pe(_swap_halves(w_pe)).astype(BF16),
        "kv_norm": kv_norm.reshape(1, KV_LORA),
        "wkv_b_k": _pad_heads(wkv_b3[..., :NOPE], NOPE, lambda h: 0).astype(BF16),
        "wkv_b_v": _pad_heads(wkv_b3[..., NOPE:], VDIM, lambda h: VDIM * (h % 2)).astype(BF16),
        "wo": wo.astype(BF16),
        "cos": cos_t,
        "sin": sin_t,
    }


def _dir_pad(w):
    z = jnp.zeros_like(w[0])
    return jnp.stack([jnp.concatenate([w[0], z], axis=0), jnp.concatenate([z, w[1]], axis=0)])


def _rwkv_weights(mu, wr, wk, wv, w0, w1, w2, a0, a1, a2, g1, g2, k_k, k_a, r_k, lnx_g, lnx_b, wo):
    blk = np.arange(256) // RWKV_N
    return {
        "mu": mu,
        "wr": wr.astype(BF16), "wk": wk.astype(BF16), "wv": wv.astype(BF16),
        "g1": g1.astype(BF16), "g2": g2.astype(BF16),
        "w1": jnp.concatenate([w1[0], w1[1]], axis=1).astype(BF16),
        "w2": _dir_pad(w2).astype(BF16),
        "a1": jnp.concatenate([a1[0], a1[1]], axis=1).astype(BF16),
        "a2": _dir_pad(a2).astype(BF16),
        "w0": w0, "a0": a0,
        "k_k": k_k.reshape(1, D), "k_a": k_a.reshape(1, D),
        "r_k": r_k.reshape(1, D), "lnx_g": lnx_g.reshape(1, D), "lnx_b": lnx_b.reshape(1, D),
        "wo": wo.astype(BF16),
        "ones_bd": jnp.asarray(blk[:, None] == blk[None, :], BF16),
    }


def _mla_layer(y, mods, w, cache_ckv, cache_kpe):
    q, ckv, kpe = _mla_pre_call(y, mods, w)
    ckv_s = ckv[N_PROMPT:].reshape(DEC_BATCH, DEC_SEQ, KV_LORA)
    kpe_s = kpe[N_PROMPT:].reshape(DEC_BATCH, DEC_SEQ, HEAD_PAD)
    cache_kpe_pad = jnp.pad(cache_kpe, ((0, 0), (0, 0), (NOPE, HEAD_PAD - NOPE - ROPE)))
    t_k = PAST + DEC_SEQ
    ckv_all = jnp.concatenate([
        jnp.concatenate([cache_ckv, ckv_s], axis=1).reshape(DEC_BATCH * t_k, KV_LORA), ckv[:N_PROMPT]], axis=0)
    kpe_all = jnp.concatenate([
        jnp.concatenate([cache_kpe_pad, kpe_s], axis=1).reshape(DEC_BATCH * t_k, HEAD_PAD), kpe[:N_PROMPT]], axis=0)
    k_all, v_all = _kv_expand_call(ckv_all, kpe_all, w)
    o_p = _attn_call(q, k_all, v_all, BATCH, SEQ, SEQ, 0, DEC_BATCH * t_k, "attn_prompt")
    o_s = _attn_call(q, k_all, v_all, DEC_BATCH, DEC_SEQ, t_k, N_PROMPT, 0, "attn_sample")
    new_ckv = ckv[:N_PROMPT].reshape(BATCH, SEQ, KV_LORA)
    new_kpe = kpe[:N_PROMPT, NOPE:NOPE + ROPE].reshape(BATCH, SEQ, ROPE)
    return jnp.concatenate([o_p, o_s], axis=0), new_ckv, new_kpe


def _rwkv_layer(y, mods, w, state):
    r, v, g, kk, kd, b, ld = _rwkv_pre_call(y, mods, w)
    zeros = jnp.zeros((2, BATCH, HEADS, RWKV_N, RWKV_N), F32)
    y_p, s_p = _scan_call(r, v, kk, kd, b, ld, zeros, BATCH, SEQ, 0, "scan_prompt")
    s0 = jnp.moveaxis(state, 1, 0)
    y_s, _ = _scan_call(r, v, kk, kd, b, ld, s0, DEC_BATCH, DEC_SEQ, N_PROMPT, "scan_sample")
    a = _rwkv_mid_call(jnp.concatenate([y_p, y_s], axis=1), r, kd, v, g, w)
    return a, jnp.moveaxis(s_p, 0, 1)


def kernel(x_prompt, x_sample, cache_ckv, cache_kpe, state_wkv, c, c_ctx, ada_w, ada_b, ln_g, ln_b, ffn_wg, ffn_wu, ffn_wd, mla_wq_a, mla_q_norm, mla_wq_b, mla_wkv_a, mla_kv_norm, mla_wkv_b, mla_wo, rwkv_mu, rwkv_wr, rwkv_wk, rwkv_wv, rwkv_w0, rwkv_w1, rwkv_w2, rwkv_a0, rwkv_a1, rwkv_a2, rwkv_g1, rwkv_g2, rwkv_k_k, rwkv_k_a, rwkv_r_k, rwkv_lnx_g, rwkv_lnx_b, rwkv_wo, pool_w, pool_scale):
    y = jnp.concatenate([x_prompt.reshape(N_PROMPT, D), x_sample.reshape(N_SAMPLE, D)], axis=0)
    cond8 = jnp.concatenate([c_ctx[None, :], c, jnp.zeros((SUBLANES - 1 - DEC_BATCH, D), F32)], axis=0)
    mods_all = _ada_call(cond8, ada_w, ada_b)
    mods_all = jnp.pad(mods_all.reshape(DEPTH, SUBLANES, 6, D)[:, :1 + DEC_BATCH],
                       ((0, 0), (0, 0), (0, SUBLANES - 6), (0, 0)))
    cos_t, sin_t = _rope_tables()
    new_ckv, new_kpe, new_wkv = [], [], []
    for layer in range(DEPTH):
        kind, j = layer % 3, layer // 3
        mods = mods_all[layer]
        wo = None
        if kind == 0:
            w = _mla_weights(mla_wq_a[j], mla_q_norm[j], mla_wq_b[j], mla_wkv_a[j], mla_kv_norm[j],
                             mla_wkv_b[j], mla_wo[j], cos_t, sin_t)
            a, ckv, kpe = _mla_layer(y, mods, w, cache_ckv[:, j], cache_kpe[:, j])
            new_ckv.append(ckv)
            new_kpe.append(kpe)
            wo = w["wo"]
        elif kind == 1:
            w = _rwkv_weights(rwkv_mu[j], rwkv_wr[j], rwkv_wk[j], rwkv_wv[j], rwkv_w0[j], rwkv_w1[j],
                              rwkv_w2[j], rwkv_a0[j], rwkv_a1[j], rwkv_a2[j], rwkv_g1[j], rwkv_g2[j],
                              rwkv_k_k[j], rwkv_k_a[j], rwkv_r_k[j], rwkv_lnx_g[j], rwkv_lnx_b[j], rwkv_wo[j])
            a, s_new = _rwkv_layer(y, mods, w, state_wkv[:, j])
            new_wkv.append(s_new)
            wo = w["wo"]
        else:
            a = _pool_call(y, mods, pool_w[j].astype(BF16), pool_scale[j].reshape(1, D))
        y = _post_call(y, a, mods, ln_g[layer], ln_b[layer], wo,
                       ffn_wg[layer].astype(BF16), ffn_wu[layer].astype(BF16), ffn_wd[layer].astype(BF16))
    return (y[:N_PROMPT].reshape(BATCH, SEQ, D), y[N_PROMPT:].reshape(DEC_BATCH, DEC_SEQ, D),
            jnp.stack(new_ckv, axis=1), jnp.stack(new_kpe, axis=1), jnp.stack(new_wkv, axis=1))
```

```python
import functools

import jax
import jax.numpy as jnp
import numpy as np
from jax import lax
from jax.experimental import pallas as pl
from jax.experimental.pallas import tpu as pltpu

F32 = jnp.float32
BF16 = jnp.bfloat16

D = 1024
BATCH, SEQ = 32, 256
DEC_BATCH, DEC_SEQ = 2, 4096
PAST = 512
DEPTH = 4
GRID_W = 64
HEADS = 16
Q_LORA, KV_LORA = 384, 256
NOPE, ROPE, VDIM = 64, 32, 64
ROPE_THETA = 10000.0
RWKV_N = 64
RWKV_GN_EPS = 64e-5
POOL_WINDOWS = (2, 4, 8, 16)
POOL_C = D // 4
D_FF = 2816
ALPHA = (2 * DEPTH) ** 0.25
LN_EPS = 1e-5
RMS_EPS = 1e-6
LOG2_E = 1.4426950408889634

N_PROMPT = BATCH * SEQ
N_SAMPLE = DEC_BATCH * DEC_SEQ
N_TOK = N_PROMPT + N_SAMPLE

LANES = 128
SUBLANES = 8
HEAD_PAD = 128
VMEM_LIMIT = 56 * 1024 * 1024

TM = 512
TMIX = 256
HALO = SUBLANES
TQ = 256
ATTN_HEADS = 4
ATTN_KC = 512
ATTN_SKEW = 2
CHUNK = 64
TFF = 256


def _cparams(sem):
    return pltpu.CompilerParams(dimension_semantics=sem, vmem_limit_bytes=VMEM_LIMIT)


def _dot(a, b):
    return jnp.dot(a, b, preferred_element_type=F32)


def _dot_nt(a, b):
    return lax.dot_general(a, b, (((1,), (1,)), ((), ())), preferred_element_type=F32)


def _dot_tn(a, b):
    return lax.dot_general(a, b, (((0,), (0,)), ((), ())), preferred_element_type=F32)


def _layer_norm(x, g, b):
    mu = jnp.mean(x, axis=-1, keepdims=True)
    xc = x - mu
    var = jnp.mean(xc * xc, axis=-1, keepdims=True)
    return xc * lax.rsqrt(var + LN_EPS) * g + b


def _rms_norm(x, g):
    return x * lax.rsqrt(jnp.mean(x * x, axis=-1, keepdims=True) + RMS_EPS) * g


def _silu(x):
    return x * jax.nn.sigmoid(x)


def _cond_of_tile(i, tile):
    first_sample = N_PROMPT // tile
    per_seq = DEC_SEQ // tile
    return jnp.where(i < first_sample, 0, 1 + (i - first_sample) // per_seq)


def _split3(x):
    hi = x.astype(BF16)
    r1 = x - hi.astype(F32)
    mid = r1.astype(BF16)
    lo = (r1 - mid.astype(F32)).astype(BF16)
    return hi, mid, lo


def _seg_sum(x, ones_bd):
    outs = []
    for g in range(D // 256):
        hi, mid, lo = _split3(x[:, 256 * g:256 * (g + 1)])
        outs.append(_dot(hi, ones_bd) + _dot(mid, ones_bd) + _dot(lo, ones_bd))
    return jnp.concatenate(outs, axis=1)


def _ada_kernel(c_ref, w_ref, b_ref, o_ref):
    a = _silu(c_ref[...]).astype(BF16)
    o_ref[...] = _dot(a, w_ref[...].astype(BF16)) + b_ref[...]


def _ada_call(cond8, ada_w, ada_b):
    tn = 1536
    return pl.pallas_call(
        _ada_kernel,
        grid=(DEPTH, 6 * D // tn),
        in_specs=[
            pl.BlockSpec((SUBLANES, D), lambda l, n: (0, 0)),
            pl.BlockSpec((None, D, tn), lambda l, n: (l, 0, n)),
            pl.BlockSpec((None, 1, tn), lambda l, n: (l, 0, n)),
        ],
        out_specs=pl.BlockSpec((None, SUBLANES, tn), lambda l, n: (l, 0, n)),
        out_shape=jax.ShapeDtypeStruct((DEPTH, SUBLANES, 6 * D), F32),
        compiler_params=_cparams(("parallel", "parallel")),
        name="adaln",
    )(cond8, ada_w, ada_b.reshape(DEPTH, 1, 6 * D))


def _mla_pre_kernel(y_ref, m_ref, wqa_ref, qn_ref, wqb_ref, wqbs_ref, wkva_ref, wkvas_ref,
                    kvn_ref, cos_ref, sin_ref, q_ref, ckv_ref, kpe_ref):
    i = pl.program_id(0)
    h = (y_ref[...] * (1.0 + m_ref[1:2, :]) + m_ref[0:1, :]).astype(BF16)
    qa = _rms_norm(_dot(h, wqa_ref[...]), qn_ref[...]).astype(BF16)
    kv = _dot(h, wkva_ref[...])
    ckv_ref[...] = _rms_norm(kv[:, :KV_LORA], kvn_ref[...])
    kpe = kv[:, KV_LORA:]
    is_latent = i >= N_PROMPT // TM

    @pl.when(jnp.logical_not(is_latent))
    def _():
        q_ref[...] = _dot(qa, wqb_ref[...]).astype(BF16)
        kpe_ref[...] = kpe

    @pl.when(is_latent)
    def _():
        cos = cos_ref[...]
        sin = sin_ref[...]
        kpe_ref[...] = kpe * cos + _dot(h, wkvas_ref[...]) * sin
        q = _dot(qa, wqb_ref[...])
        qs = _dot(qa, wqbs_ref[...])
        for hd in range(HEADS):
            sl = slice(HEAD_PAD * hd, HEAD_PAD * (hd + 1))
            q_ref[:, sl] = (q[:, sl] * cos + qs[:, sl] * sin).astype(BF16)


def _mla_pre_call(y, mods, w):
    first_sample = N_PROMPT // TM
    per_seq = DEC_SEQ // TM
    const = lambda i: (0, 0)
    rope_map = lambda i: (jnp.where(i < first_sample, 0, (i - first_sample) % per_seq), 0)
    return pl.pallas_call(
        _mla_pre_kernel,
        grid=(N_TOK // TM,),
        in_specs=[
            pl.BlockSpec((TM, D), lambda i: (i, 0)),
            pl.BlockSpec((None, SUBLANES, D), lambda i: (_cond_of_tile(i, TM), 0, 0)),
            pl.BlockSpec((D, Q_LORA), const),
            pl.BlockSpec((1, Q_LORA), const),
            pl.BlockSpec((Q_LORA, HEADS * HEAD_PAD), const),
            pl.BlockSpec((Q_LORA, HEADS * HEAD_PAD), const),
            pl.BlockSpec((D, KV_LORA + HEAD_PAD), const),
            pl.BlockSpec((D, HEAD_PAD), const),
            pl.BlockSpec((1, KV_LORA), const),
            pl.BlockSpec((TM, HEAD_PAD), rope_map),
            pl.BlockSpec((TM, HEAD_PAD), rope_map),
        ],
        out_specs=[
            pl.BlockSpec((TM, HEADS * HEAD_PAD), lambda i: (i, 0)),
            pl.BlockSpec((TM, KV_LORA), lambda i: (i, 0)),
            pl.BlockSpec((TM, HEAD_PAD), lambda i: (i, 0)),
        ],
        out_shape=[
            jax.ShapeDtypeStruct((N_TOK, HEADS * HEAD_PAD), BF16),
            jax.ShapeDtypeStruct((N_TOK, KV_LORA), F32),
            jax.ShapeDtypeStruct((N_TOK, HEAD_PAD), F32),
        ],
        compiler_params=_cparams(("parallel",)),
        name="mla_pre",
    )(y, mods, w["wq_a"], w["q_norm"], w["wq_b"], w["wq_b_sw"], w["wkv_a"], w["wkv_a_sw"],
      w["kv_norm"], w["cos"], w["sin"])


def _kv_expand_kernel(ckv_ref, kpe_ref, wk_ref, wv_ref, vone_ref, k_ref, v_ref):
    c = ckv_ref[...].astype(BF16)
    kpe = kpe_ref[...]
    kn = _dot(c, wk_ref[...])
    for hd in range(HEADS):
        sl = slice(HEAD_PAD * hd, HEAD_PAD * (hd + 1))
        k_ref[:, sl] = (kn[:, sl] + kpe).astype(BF16)
    v_ref[...] = (_dot(c, wv_ref[...]) + vone_ref[...]).astype(BF16)


def _kv_expand_call(ckv_all, kpe_all, w):
    rows = ckv_all.shape[0]
    const = lambda i: (0, 0)
    return pl.pallas_call(
        _kv_expand_kernel,
        grid=(rows // TM,),
        in_specs=[
            pl.BlockSpec((TM, KV_LORA), lambda i: (i, 0)),
            pl.BlockSpec((TM, HEAD_PAD), lambda i: (i, 0)),
            pl.BlockSpec((KV_LORA, HEADS * HEAD_PAD), const),
            pl.BlockSpec((KV_LORA, HEADS * HEAD_PAD), const),
            pl.BlockSpec((1, HEADS * HEAD_PAD), const),
        ],
        out_specs=[
            pl.BlockSpec((TM, HEADS * HEAD_PAD), lambda i: (i, 0)),
            pl.BlockSpec((TM, HEADS * HEAD_PAD), lambda i: (i, 0)),
        ],
        out_shape=[jax.ShapeDtypeStruct((rows, HEADS * HEAD_PAD), BF16)] * 2,
        compiler_params=_cparams(("parallel",)),
        name="kv_expand",
    )(ckv_all, kpe_all, w["wkv_b_k"], w["wkv_b_v"], w["v_one"])


def _sum_lane(h):
    return HEAD_PAD - 1 if h % 2 == 0 else 0


def _attn_kernel(t_k, q_ref, k_ref, v_ref, o_ref):
    scale2 = (NOPE + ROPE) ** -0.5 * LOG2_E
    kc = min(ATTN_KC, t_k)
    chunks = [slice(c * kc, (c + 1) * kc) for c in range(t_k // kc)]
    head = [slice(HEAD_PAD * e, HEAD_PAD * (e + 1)) for e in range(ATTN_HEADS)]

    def scores(e, c):
        return _dot_nt(q_ref[:, head[e]], k_ref[c, head[e]])

    lane = lax.broadcasted_iota(jnp.int32, (TQ, HEAD_PAD), 1)
    s_cur = [scores(0, c) for c in chunks]
    outs = []
    for e in range(ATTN_HEADS):
        m = functools.reduce(jnp.maximum, [jnp.max(s, axis=-1, keepdims=True) for s in s_cur])
        s_next, acc = [], None
        for i in range(len(chunks) + ATTN_SKEW):
            if e + 1 < ATTN_HEADS and i < len(chunks):
                s_next.append(scores(e + 1, chunks[i]))
            if i >= ATTN_SKEW:
                c = chunks[i - ATTN_SKEW]
                p = jnp.exp2((s_cur[i - ATTN_SKEW] - m) * scale2).astype(BF16)
                pv = _dot(p, v_ref[c, head[e]])
                acc = pv if acc is None else acc + pv
        sl = _sum_lane(e)
        l = acc[:, sl:sl + 1]
        outs.append(jnp.where(lane == sl, 0.0, acc) * (1.0 / l))
        s_cur = s_next
    for j in range(ATTN_HEADS // 2):
        o_ref[:, 2 * VDIM * j:2 * VDIM * (j + 1)] = (outs[2 * j] + outs[2 * j + 1]).astype(BF16)


def _attn_call(q, k, v, n_seq, t_q, t_k, q_row0, k_row0, name):
    nq = t_q // TQ
    q_blk0 = q_row0 // TQ
    k_blk0 = k_row0 // t_k
    width = ATTN_HEADS * HEAD_PAD
    return pl.pallas_call(
        functools.partial(_attn_kernel, t_k),
        grid=(n_seq, HEADS // ATTN_HEADS, nq),
        in_specs=[
            pl.BlockSpec((TQ, width), lambda b, p, i: (q_blk0 + b * nq + i, p)),
            pl.BlockSpec((t_k, width), lambda b, p, i: (k_blk0 + b, p)),
            pl.BlockSpec((t_k, width), lambda b, p, i: (k_blk0 + b, p)),
        ],
        out_specs=pl.BlockSpec((TQ, ATTN_HEADS * VDIM), lambda b, p, i: (b * nq + i, p)),
        out_shape=jax.ShapeDtypeStruct((n_seq * t_q, HEADS * VDIM), BF16),
        compiler_params=_cparams(("parallel", "parallel", "arbitrary")),
        name=name,
    )(q, k, v)


def _post_kernel(has_proj, y_ref, a_ref, m_ref, lng_ref, lnb_ref, *rest):
    if has_proj:
        wo_ref, wg_ref, wu_ref, wd_ref, o_ref = rest
    else:
        wg_ref, wu_ref, wd_ref, o_ref = rest
    mix = _dot(a_ref[...], wo_ref[...]) if has_proj else a_ref[...]
    y1 = _layer_norm(ALPHA * y_ref[...] + m_ref[2:3, :] * mix, lng_ref[0:1, :], lnb_ref[0:1, :])
    h = (y1 * (1.0 + m_ref[4:5, :]) + m_ref[3:4, :]).astype(BF16)
    acc = None
    for j in range(D_FF // TFF):
        cols = slice(TFF * j, TFF * (j + 1))
        act = (_silu(_dot(h, wg_ref[:, cols])) * _dot(h, wu_ref[:, cols])).astype(BF16)
        part = _dot(act, wd_ref[cols, :])
        acc = part if acc is None else acc + part
    o_ref[...] = _layer_norm(ALPHA * y1 + m_ref[5:6, :] * acc, lng_ref[1:2, :], lnb_ref[1:2, :])


def _post_call(y, a, mods, ln_g, ln_b, wo, wg, wu, wd):
    has_proj = wo is not None
    const = lambda i: (0, 0)
    resident = pl.Buffered(1)
    in_specs = [
        pl.BlockSpec((TM, D), lambda i: (i, 0)),
        pl.BlockSpec((TM, D), lambda i: (i, 0)),
        pl.BlockSpec((None, SUBLANES, D), lambda i: (_cond_of_tile(i, TM), 0, 0)),
        pl.BlockSpec((2, D), const),
        pl.BlockSpec((2, D), const),
    ]
    args = [y, a, mods, ln_g, ln_b]
    if has_proj:
        in_specs.append(pl.BlockSpec((D, D), const, pipeline_mode=resident))
        args.append(wo)
    in_specs += [
        pl.BlockSpec((D, D_FF), const, pipeline_mode=resident),
        pl.BlockSpec((D, D_FF), const, pipeline_mode=resident),
        pl.BlockSpec((D_FF, D), const, pipeline_mode=resident),
    ]
    args += [wg, wu, wd]
    return pl.pallas_call(
        functools.partial(_post_kernel, has_proj),
        grid=(N_TOK // TM,),
        in_specs=in_specs,
        out_specs=pl.BlockSpec((TM, D), lambda i: (i, 0)),
        out_shape=jax.ShapeDtypeStruct((N_TOK, D), F32),
        compiler_params=_cparams(("parallel",)),
        name="post_proj" if has_proj else "post_noproj",
    )(*args)


def _mix_tile_flags(i):
    first_sample = N_PROMPT // TMIX
    tiles_prompt = SEQ // TMIX
    tiles_sample = DEC_SEQ // TMIX
    is_p = i < first_sample
    j = jnp.where(is_p, i % tiles_prompt, (i - first_sample) % tiles_sample)
    n = jnp.where(is_p, tiles_prompt, tiles_sample)
    return j > 0, j < n - 1, j * TMIX, n * TMIX


def _halo_specs():
    blocks_per_tile = TMIX // HALO
    last = N_TOK // HALO - 1
    return [
        pl.BlockSpec((HALO, D), lambda i: (jnp.maximum(i * blocks_per_tile - 1, 0), 0)),
        pl.BlockSpec((TMIX, D), lambda i: (i, 0)),
        pl.BlockSpec((HALO, D), lambda i: (jnp.minimum((i + 1) * blocks_per_tile, last), 0)),
    ]


def _fill_ext(ext_ref, yp_ref, y_ref, yn_ref, m_ref, has_prev, has_next):
    scale1 = 1.0 + m_ref[1:2, :]
    shift = m_ref[0:1, :]
    h = y_ref[...] * scale1 + shift
    ext_ref[0:HALO, :] = jnp.where(has_prev, yp_ref[...] * scale1 + shift, 0.0)
    ext_ref[HALO:HALO + TMIX, :] = h
    ext_ref[HALO + TMIX:, :] = jnp.where(has_next, yn_ref[...] * scale1 + shift, 0.0)
    return h


def _rwkv_pre_kernel(yp_ref, y_ref, yn_ref, m_ref, mu_ref, wr_ref, wk_ref, wv_ref, g1_ref, g2_ref,
                     w1_ref, w2_ref, a1_ref, a2_ref, w0_ref, a0_ref, kk_ref, ka_ref, ones_ref,
                     r_out, v_out, g_out, kk_out, kd_out, b_out, ld_out, ext_ref):
    i = pl.program_id(0)
    has_prev, has_next, _, _ = _mix_tile_flags(i)
    h = _fill_ext(ext_ref, yp_ref, y_ref, yn_ref, m_ref, has_prev, has_next)
    prev = ext_ref[HALO - 1:HALO - 1 + TMIX, :]
    nxt = ext_ref[HALO + 1:HALO + 1 + TMIX, :]
    xx = 0.5 * (prev + nxt) - h

    def mix(j):
        return (h + xx * mu_ref[j:j + 1, :]).astype(BF16)

    r_out[...] = _dot(mix(0), wr_ref[...])
    k = _dot(mix(2), wk_ref[...])
    v_out[...] = _dot(mix(3), wv_ref[...])
    g_out[...] = _dot(jax.nn.sigmoid(_dot(mix(5), g1_ref[...])).astype(BF16), g2_ref[...])
    tw = jnp.tanh(_dot(mix(1), w1_ref[...])).astype(BF16)
    ta = _dot(mix(4), a1_ref[...]).astype(BF16)

    kkf = k * kk_ref[...]
    nrm = jnp.sqrt(_seg_sum(kkf * kkf, ones_ref[...]))
    kk = kkf / jnp.maximum(nrm, 1e-12)
    kk_out[...] = kk
    for d in range(2):
        z = -(w0_ref[d:d + 1, :] + _dot(tw, w2_ref[d]))
        softplus = jnp.maximum(z, 0.0) + jnp.log(1.0 + jnp.exp(-jnp.abs(z)))
        ld_out[d] = -jnp.exp(-softplus - 0.5)
        a = jax.nn.sigmoid(a0_ref[d:d + 1, :] + _dot(ta, a2_ref[d]))
        kd_out[d] = k * (1.0 + (a - 1.0) * ka_ref[...])
        b_out[d] = kk * a


def _rwkv_pre_call(y, mods, w):
    const2 = lambda i: (0, 0)
    const3 = lambda i: (0, 0, 0)
    row = pl.BlockSpec((TMIX, D), lambda i: (i, 0))
    row2 = pl.BlockSpec((2, TMIX, D), lambda i: (0, i, 0))
    one = jax.ShapeDtypeStruct((N_TOK, D), F32)
    two = jax.ShapeDtypeStruct((2, N_TOK, D), F32)
    return pl.pallas_call(
        _rwkv_pre_kernel,
        grid=(N_TOK // TMIX,),
        in_specs=_halo_specs() + [
            pl.BlockSpec((None, SUBLANES, D), lambda i: (_cond_of_tile(i, TMIX), 0, 0)),
            pl.BlockSpec((6, D), const2),
            pl.BlockSpec((D, D), const2), pl.BlockSpec((D, D), const2), pl.BlockSpec((D, D), const2),
            pl.BlockSpec((D, 128), const2), pl.BlockSpec((128, D), const2),
            pl.BlockSpec((D, 128), const2), pl.BlockSpec((2, 128, D), const3),
            pl.BlockSpec((D, 128), const2), pl.BlockSpec((2, 128, D), const3),
            pl.BlockSpec((2, D), const2), pl.BlockSpec((2, D), const2),
            pl.BlockSpec((1, D), const2), pl.BlockSpec((1, D), const2),
            pl.BlockSpec((256, 256), const2),
        ],
        out_specs=[row, row, row, row, row2, row2, row2],
        out_shape=[one, one, one, one, two, two, two],
        scratch_shapes=[pltpu.VMEM((TMIX + 2 * HALO, D), F32)],
        compiler_params=_cparams(("parallel",)),
        name="rwkv_pre",
    )(y, y, y, mods, w["mu"], w["wr"], w["wk"], w["wv"], w["g1"], w["g2"], w["w1"], w["w2"],
      w["a1"], w["a2"], w["w0"], w["a0"], w["k_k"], w["k_a"], w["ones_bd"])


def _scan_kernel(n_chunks, r_ref, v_ref, kk_ref, kd_ref, b_ref, ld_ref, s0_ref, y_ref, sf_ref, s_ref):
    d = pl.program_id(0)
    n = pl.program_id(2)
    C = CHUNK
    P2 = 2 * C
    rev = d == 1

    @pl.when(n == 0)
    def _():
        s_ref[...] = jnp.zeros_like(s_ref)
        for hd in range(HEADS):
            o = RWKV_N * (hd % 2)
            s_ref[hd // 2, o:o + RWKV_N, o:o + RWKV_N] = s0_ref[hd]

    sign = jnp.where(rev, -1, 1)
    ti = lax.broadcasted_iota(jnp.int32, (C, C), 0)
    si = lax.broadcasted_iota(jnp.int32, (C, C), 1)
    cum = (sign * (si - ti) <= 0).astype(BF16)
    ld = ld_ref[...]
    ld_hi, ld_mid, ld_lo = _split3(ld)
    cs = _dot(cum, ld_hi) + _dot(cum, ld_mid) + _dot(cum, ld_lo)
    last = jnp.where(rev, cs[0:1, :], cs[C - 1:C, :])
    pin = jnp.exp(cs)
    pinv = jnp.exp(-cs)
    at = -kk_ref[...] * jnp.exp(cs - ld)
    rt = r_ref[...] * pin
    bt = b_ref[...] * pinv
    kt = kd_ref[...] * pinv
    tail = jnp.exp(last - cs)
    bh = b_ref[...] * tail
    kh = kd_ref[...] * tail
    pc = jnp.exp(last)
    v = v_ref[...]

    lane = lax.broadcasted_iota(jnp.int32, (C, 2 * RWKV_N), 1)
    even = lane < RWKV_N
    r2 = lax.broadcasted_iota(jnp.int32, (P2, P2), 0)
    c2 = lax.broadcasted_iota(jnp.int32, (P2, P2), 1)
    same = (r2 // C) == (c2 // C)
    order = sign * (c2 % C - r2 % C)
    m_incl = same & (order <= 0)
    m_strict = same & (order < 0)
    eye = (r2 == c2).astype(F32)

    def same_block(size):
        return (r2 // size) == (c2 // size)

    leaf = same_block(2)
    merges = [same_block(2 * size) & jnp.logical_not(same_block(size)) for size in (2, 4, 8, 16, 32)]

    def split(x):
        return jnp.concatenate([jnp.where(even, x, 0.0), jnp.where(even, 0.0, x)], axis=0)

    def twice(x):
        return jnp.concatenate([x, x], axis=0)

    pairs = range(HEADS // 2)
    lanes = [slice(2 * RWKV_N * p, 2 * RWKV_N * (p + 1)) for p in pairs]
    s0 = [s_ref[p] for p in pairs]
    lhs = [jnp.concatenate([split(at[:, sl]), split(rt[:, sl])], axis=0).astype(BF16) for sl in lanes]
    rhs = [jnp.concatenate([twice(bt[:, sl]), twice(kt[:, sl])], axis=0).astype(BF16) for sl in lanes]
    v2b = [split(v[:, sl]).astype(BF16) for sl in lanes]
    g = [_dot_nt(lhs[p], rhs[p]) for p in pairs]
    xs = [_dot_nt(lhs[p], s0[p].astype(BF16)) for p in pairs]
    a_ab = [jnp.where(m_strict, g[p][:P2, :P2], 0.0) for p in pairs]
    inv = [eye + jnp.where(leaf, a_ab[p], 0.0) for p in pairs]
    for mk in merges:
        invb = [inv[p].astype(BF16) for p in pairs]
        binv = [_dot(jnp.where(mk, a_ab[p], 0.0).astype(BF16), invb[p]).astype(BF16) for p in pairs]
        inv = [inv[p] + _dot(invb[p], binv[p]) for p in pairs]
    a_ak = [jnp.where(m_strict, g[p][:P2, P2:], 0.0).astype(BF16) for p in pairs]
    rhs_u = [(xs[p][:P2] + _dot(a_ak[p], v2b[p])).astype(BF16) for p in pairs]
    u2b = [_dot(inv[p].astype(BF16), rhs_u[p]).astype(BF16) for p in pairs]
    m_rb = [jnp.where(m_incl, g[p][P2:, :P2], 0.0).astype(BF16) for p in pairs]
    m_rk = [jnp.where(m_incl, g[p][P2:, P2:], 0.0).astype(BF16) for p in pairs]
    for p in pairs:
        y2 = xs[p][P2:] + _dot(m_rb[p], u2b[p]) + _dot(m_rk[p], v2b[p])
        y_ref[:, lanes[p]] = y2[:C] + y2[C:]
    for p in pairs:
        bh2 = split(bh[:, lanes[p]]).astype(BF16)
        kh2 = split(kh[:, lanes[p]]).astype(BF16)
        s_ref[p] = s0[p] * pc[:, lanes[p]] + _dot_tn(u2b[p], bh2) + _dot_tn(v2b[p], kh2)

    @pl.when(n == n_chunks - 1)
    def _():
        for hd in range(HEADS):
            o = RWKV_N * (hd % 2)
            sf_ref[hd] = s_ref[hd // 2, o:o + RWKV_N, o:o + RWKV_N]


def _scan_call(r, v, kk, kd, b, ld, s0, n_seq, t_seq, row0, name):
    n_chunks = t_seq // CHUNK
    blk0 = row0 // CHUNK

    def chunk(dd, bb, nn):
        return blk0 + bb * n_chunks + jnp.where(dd == 0, nn, n_chunks - 1 - nn)

    row = pl.BlockSpec((CHUNK, D), lambda dd, bb, nn: (chunk(dd, bb, nn), 0))
    row2 = pl.BlockSpec((None, CHUNK, D), lambda dd, bb, nn: (dd, chunk(dd, bb, nn), 0))
    st = pl.BlockSpec((None, None, HEADS, RWKV_N, RWKV_N), lambda dd, bb, nn: (dd, bb, 0, 0, 0))
    return pl.pallas_call(
        functools.partial(_scan_kernel, n_chunks),
        grid=(2, n_seq, n_chunks),
        in_specs=[row, row, row, row2, row2, row2, st],
        out_specs=[
            pl.BlockSpec((None, CHUNK, D), lambda dd, bb, nn: (dd, chunk(dd, bb, nn) - blk0, 0)),
            st,
        ],
        out_shape=[
            jax.ShapeDtypeStruct((2, n_seq * t_seq, D), F32),
            jax.ShapeDtypeStruct((2, n_seq, HEADS, RWKV_N, RWKV_N), F32),
        ],
        scratch_shapes=[pltpu.VMEM((HEADS // 2, 2 * RWKV_N, 2 * RWKV_N), F32)],
        compiler_params=_cparams(("parallel", "parallel", "arbitrary")),
        name=name,
    )(r, v, kk, kd, b, ld, s0)


def _rwkv_mid_kernel(y_ref, r_ref, kd_ref, v_ref, g_ref, rk_ref, lg_ref, lb_ref, ones_ref, o_ref):
    ones = ones_ref[...]
    y = y_ref[0] + y_ref[1]
    mu = _seg_sum(y, ones) * (1.0 / RWKV_N)
    yc = y - mu
    var = _seg_sum(yc * yc, ones) * (1.0 / RWKV_N)
    yn = yc * lax.rsqrt(var + RWKV_GN_EPS) * lg_ref[...] + lb_ref[...]
    bonus = _seg_sum(r_ref[...] * (kd_ref[0] + kd_ref[1]) * rk_ref[...], ones) * v_ref[...]
    o_ref[...] = ((yn + bonus) * g_ref[...]).astype(BF16)


def _rwkv_mid_call(y2, r, kd, v, g, w):
    const2 = lambda i: (0, 0)
    row = pl.BlockSpec((TM, D), lambda i: (i, 0))
    row2 = pl.BlockSpec((2, TM, D), lambda i: (0, i, 0))
    return pl.pallas_call(
        _rwkv_mid_kernel,
        grid=(N_TOK // TM,),
        in_specs=[row2, row, row2, row, row,
                  pl.BlockSpec((1, D), const2), pl.BlockSpec((1, D), const2), pl.BlockSpec((1, D), const2),
                  pl.BlockSpec((256, 256), const2)],
        out_specs=row,
        out_shape=jax.ShapeDtypeStruct((N_TOK, D), BF16),
        compiler_params=_cparams(("parallel",)),
        name="rwkv_mid",
    )(y2, r, kd, v, g, w["r_k"], w["lnx_g"], w["lnx_b"], w["ones_bd"])


def _pool_kernel(yp_ref, y_ref, yn_ref, m_ref, w_ref, sc_ref, o_ref, ext_ref):
    i = pl.program_id(0)
    has_prev, has_next, pos0, seq_len = _mix_tile_flags(i)
    _fill_ext(ext_ref, yp_ref, y_ref, yn_ref, m_ref, has_prev, has_next)
    t = pos0 + lax.broadcasted_iota(jnp.int32, (TMIX, 1), 0)
    for gi, win in enumerate(POOL_WINDOWS):
        cols = slice(POOL_C * gi, POOL_C * (gi + 1))
        acc = None
        for j in range(-(win // 2), win - win // 2):
            x = ext_ref[HALO + j:HALO + j + TMIX, cols]
            acc = x if acc is None else acc + x
        lo = jnp.maximum(t - win // 2, 0)
        hi = jnp.minimum(t - win // 2 + win, seq_len)
        cnt = (hi - lo).astype(F32)
        pooled = (acc / cnt - ext_ref[HALO:HALO + TMIX, cols]).astype(BF16)
        o_ref[:, cols] = _dot(pooled, w_ref[gi]) * sc_ref[:, cols]


def _pool_call(y, mods, pool_w, pool_scale):
    return pl.pallas_call(
        _pool_kernel,
        grid=(N_TOK // TMIX,),
        in_specs=_halo_specs() + [
            pl.BlockSpec((None, SUBLANES, D), lambda i: (_cond_of_tile(i, TMIX), 0, 0)),
            pl.BlockSpec((4, POOL_C, POOL_C), lambda i: (0, 0, 0)),
            pl.BlockSpec((1, D), lambda i: (0, 0)),
        ],
        out_specs=pl.BlockSpec((TMIX, D), lambda i: (i, 0)),
        out_shape=jax.ShapeDtypeStruct((N_TOK, D), F32),
        scratch_shapes=[pltpu.VMEM((TMIX + 2 * HALO, D), F32)],
        compiler_params=_cparams(("parallel",)),
        name="pool",
    )(y, y, y, mods, pool_w, pool_scale)


def _rope_tables():
    rows = DEC_SEQ // GRID_W
    row = jnp.repeat(jnp.arange(rows, dtype=F32), GRID_W)
    col = jnp.tile(jnp.arange(GRID_W, dtype=F32), rows)
    n_freq = ROPE // 4
    inv_freq = ROPE_THETA ** (-jnp.arange(n_freq, dtype=F32) / n_freq)
    ang = jnp.stack([row[:, None] * inv_freq, col[:, None] * inv_freq], axis=1)
    cos, sin = jnp.cos(ang), jnp.sin(ang)
    cos32 = jnp.concatenate([cos, cos], axis=-1).reshape(DEC_SEQ, ROPE)
    sin32 = jnp.concatenate([-sin, sin], axis=-1).reshape(DEC_SEQ, ROPE)
    ones = jnp.ones((DEC_SEQ, NOPE), F32)
    zeros = jnp.zeros((DEC_SEQ, NOPE), F32)
    tail = HEAD_PAD - NOPE - ROPE
    cos_t = jnp.concatenate([ones, cos32, jnp.ones((DEC_SEQ, tail), F32)], axis=1)
    sin_t = jnp.concatenate([zeros, sin32, jnp.zeros((DEC_SEQ, tail), F32)], axis=1)
    return cos_t, sin_t


def _swap_halves(w):
    s = w.reshape(*w.shape[:-1], 2, 2, ROPE // 4)
    return s[..., ::-1, :].reshape(w.shape)


def _pad_heads(w, width, offset_fn):
    cols = [jnp.pad(w[:, h], ((0, 0), (offset_fn(h), HEAD_PAD - offset_fn(h) - width))) for h in range(HEADS)]
    return jnp.concatenate(cols, axis=1)


def _mla_weights(wq_a, q_norm, wq_b, wkv_a, kv_norm, wkv_b, wo, cos_t, sin_t):
    qk = NOPE + ROPE
    wq_b3 = wq_b.reshape(Q_LORA, HEADS, qk)
    q_sw = jnp.concatenate([jnp.zeros((Q_LORA, HEADS, NOPE), F32), _swap_halves(wq_b3[..., NOPE:])], axis=-1)
    w_pe = wkv_a[:, KV_LORA:]
    pad_pe = lambda x: jnp.pad(x, ((0, 0), (NOPE, HEAD_PAD - NOPE - ROPE)))
    wkv_b3 = wkv_b.reshape(KV_LORA, HEADS, NOPE + VDIM)
    return {
        "wq_a": wq_a.astype(BF16),
        "q_norm": q_norm.reshape(1, Q_LORA),
        "wq_b": _pad_heads(wq_b3, qk, lambda h: 0).astype(BF16),
        "wq_b_sw": _pad_heads(q_sw, qk, lambda h: 0).astype(BF16),
        "wkv_a": jnp.concatenate([wkv_a[:, :KV_LORA], pad_pe(w_pe)], axis=1).astype(BF16),
        "wkv_a_sw": pad_pe(_swap_halves(w_pe)).astype(BF16),
        "kv_norm": kv_norm.reshape(1, KV_LORA),
        "wkv_b_k": _pad_heads(wkv_b3[..., :NOPE], NOPE, lambda h: 0).astype(BF16),
        "wkv_b_v": _pad_heads(wkv_b3[..., NOPE:], VDIM, lambda h: VDIM * (h % 2)).astype(BF16),
        "v_one": jnp.zeros((1, HEADS * HEAD_PAD), F32).at[
            0, np.array([HEAD_PAD * h + _sum_lane(h) for h in range(HEADS)])].set(1.0),
        "wo": wo.astype(BF16),
        "cos": cos_t,
        "sin": sin_t,
    }


def _dir_pad(w):
    z = jnp.zeros_like(w[0])
    return jnp.stack([jnp.concatenate([w[0], z], axis=0), jnp.concatenate([z, w[1]], axis=0)])


def _rwkv_weights(mu, wr, wk, wv, w0, w1, w2, a0, a1, a2, g1, g2, k_k, k_a, r_k, lnx_g, lnx_b, wo):
    blk = np.arange(256) // RWKV_N
    return {
        "mu": mu,
        "wr": wr.astype(BF16), "wk": wk.astype(BF16), "wv": wv.astype(BF16),
        "g1": g1.astype(BF16), "g2": g2.astype(BF16),
        "w1": jnp.concatenate([w1[0], w1[1]], axis=1).astype(BF16),
        "w2": _dir_pad(w2).astype(BF16),
        "a1": jnp.concatenate([a1[0], a1[1]], axis=1).astype(BF16),
        "a2": _dir_pad(a2).astype(BF16),
        "w0": w0, "a0": a0,
        "k_k": k_k.reshape(1, D), "k_a": k_a.reshape(1, D),
        "r_k": r_k.reshape(1, D), "lnx_g": lnx_g.reshape(1, D), "lnx_b": lnx_b.reshape(1, D),
        "wo": wo.astype(BF16),
        "ones_bd": jnp.asarray(blk[:, None] == blk[None, :], BF16),
    }


def _mla_layer(y, mods, w, cache_ckv, cache_kpe):
    q, ckv, kpe = _mla_pre_call(y, mods, w)
    ckv_s = ckv[N_PROMPT:].reshape(DEC_BATCH, DEC_SEQ, KV_LORA)
    kpe_s = kpe[N_PROMPT:].reshape(DEC_BATCH, DEC_SEQ, HEAD_PAD)
    cache_kpe_pad = jnp.pad(cache_kpe, ((0, 0), (0, 0), (NOPE, HEAD_PAD - NOPE - ROPE)))
    t_k = PAST + DEC_SEQ
    ckv_all = jnp.concatenate([
        jnp.concatenate([cache_ckv, ckv_s], axis=1).reshape(DEC_BATCH * t_k, KV_LORA), ckv[:N_PROMPT]], axis=0)
    kpe_all = jnp.concatenate([
        jnp.concatenate([cache_kpe_pad, kpe_s], axis=1).reshape(DEC_BATCH * t_k, HEAD_PAD), kpe[:N_PROMPT]], axis=0)
    k_all, v_all = _kv_expand_call(ckv_all, kpe_all, w)
    o_p = _attn_call(q, k_all, v_all, BATCH, SEQ, SEQ, 0, DEC_BATCH * t_k, "attn_prompt")
    o_s = _attn_call(q, k_all, v_all, DEC_BATCH, DEC_SEQ, t_k, N_PROMPT, 0, "attn_sample")
    new_ckv = ckv[:N_PROMPT].reshape(BATCH, SEQ, KV_LORA)
    new_kpe = kpe[:N_PROMPT, NOPE:NOPE + ROPE].reshape(BATCH, SEQ, ROPE)
    return jnp.concatenate([o_p, o_s], axis=0), new_ckv, new_kpe


def _rwkv_layer(y, mods, w, state):
    r, v, g, kk, kd, b, ld = _rwkv_pre_call(y, mods, w)
    zeros = jnp.zeros((2, BATCH, HEADS, RWKV_N, RWKV_N), F32)
    y_p, s_p = _scan_call(r, v, kk, kd, b, ld, zeros, BATCH, SEQ, 0, "scan_prompt")
    s0 = jnp.moveaxis(state, 1, 0)
    y_s, _ = _scan_call(r, v, kk, kd, b, ld, s0, DEC_BATCH, DEC_SEQ, N_PROMPT, "scan_sample")
    a = _rwkv_mid_call(jnp.concatenate([y_p, y_s], axis=1), r, kd, v, g, w)
    return a, jnp.moveaxis(s_p, 0, 1)


def kernel(x_prompt, x_sample, cache_ckv, cache_kpe, state_wkv, c, c_ctx, ada_w, ada_b, ln_g, ln_b, ffn_wg, ffn_wu, ffn_wd, mla_wq_a, mla_q_norm, mla_wq_b, mla_wkv_a, mla_kv_norm, mla_wkv_b, mla_wo, rwkv_mu, rwkv_wr, rwkv_wk, rwkv_wv, rwkv_w0, rwkv_w1, rwkv_w2, rwkv_a0, rwkv_a1, rwkv_a2, rwkv_g1, rwkv_g2, rwkv_k_k, rwkv_k_a, rwkv_r_k, rwkv_lnx_g, rwkv_lnx_b, rwkv_wo, pool_w, pool_scale):
    y = jnp.concatenate([x_prompt.reshape(N_PROMPT, D), x_sample.reshape(N_SAMPLE, D)], axis=0)
    cond8 = jnp.concatenate([c_ctx[None, :], c, jnp.zeros((SUBLANES - 1 - DEC_BATCH, D), F32)], axis=0)
    mods_all = _ada_call(cond8, ada_w, ada_b)
    mods_all = jnp.pad(mods_all.reshape(DEPTH, SUBLANES, 6, D)[:, :1 + DEC_BATCH],
                       ((0, 0), (0, 0), (0, SUBLANES - 6), (0, 0)))
    cos_t, sin_t = _rope_tables()
    new_ckv, new_kpe, new_wkv = [], [], []
    for layer in range(DEPTH):
        kind, j = layer % 3, layer // 3
        mods = mods_all[layer]
        wo = None
        if kind == 0:
            w = _mla_weights(mla_wq_a[j], mla_q_norm[j], mla_wq_b[j], mla_wkv_a[j], mla_kv_norm[j],
                             mla_wkv_b[j], mla_wo[j], cos_t, sin_t)
            a, ckv, kpe = _mla_layer(y, mods, w, cache_ckv[:, j], cache_kpe[:, j])
            new_ckv.append(ckv)
            new_kpe.append(kpe)
            wo = w["wo"]
        elif kind == 1:
            w = _rwkv_weights(rwkv_mu[j], rwkv_wr[j], rwkv_wk[j], rwkv_wv[j], rwkv_w0[j], rwkv_w1[j],
                              rwkv_w2[j], rwkv_a0[j], rwkv_a1[j], rwkv_a2[j], rwkv_g1[j], rwkv_g2[j],
                              rwkv_k_k[j], rwkv_k_a[j], rwkv_r_k[j], rwkv_lnx_g[j], rwkv_lnx_b[j], rwkv_wo[j])
            a, s_new = _rwkv_layer(y, mods, w, state_wkv[:, j])
            new_wkv.append(s_new)
            wo = w["wo"]
        else:
            a = _pool_call(y, mods, pool_w[j].astype(BF16), pool_scale[j].reshape(1, D))
        y = _post_call(y, a, mods, ln_g[layer], ln_b[layer], wo,
                       ffn_wg[layer].astype(BF16), ffn_wu[layer].astype(BF16), ffn_wd[layer].astype(BF16))
    return (y[:N_PROMPT].reshape(BATCH, SEQ, D), y[N_PROMPT:].reshape(DEC_BATCH, DEC_SEQ, D),
            jnp.stack(new_ckv, axis=1), jnp.stack(new_kpe, axis=1), jnp.stack(new_wkv, axis=1))
```

```python
import functools

import jax
import jax.numpy as jnp
import numpy as np
from jax import lax
from jax.experimental import pallas as pl
from jax.experimental.pallas import tpu as pltpu

F32 = jnp.float32
BF16 = jnp.bfloat16

D = 1024
BATCH, SEQ = 32, 256
DEC_BATCH, DEC_SEQ = 2, 4096
PAST = 512
DEPTH = 4
GRID_W = 64
HEADS = 16
Q_LORA, KV_LORA = 384, 256
NOPE, ROPE, VDIM = 64, 32, 64
ROPE_THETA = 10000.0
RWKV_N = 64
RWKV_GN_EPS = 64e-5
POOL_WINDOWS = (2, 4, 8, 16)
POOL_C = D // 4
D_FF = 2816
ALPHA = (2 * DEPTH) ** 0.25
LN_EPS = 1e-5
RMS_EPS = 1e-6
LOG2_E = 1.4426950408889634
EXP_M_HALF = 0.6065306597126334

N_PROMPT = BATCH * SEQ
N_SAMPLE = DEC_BATCH * DEC_SEQ
N_TOK = N_PROMPT + N_SAMPLE

LANES = 128
SUBLANES = 8
HEAD_PAD = 128
VMEM_LIMIT = 56 * 1024 * 1024

TM = 512
TMIX = 256
HALO = SUBLANES
TQ = 256
ATTN_HEADS = 4
ATTN_KC = 512
ATTN_SKEW = 2
ATTN_GATE = 3
CHUNK = 64
TFF = 256


def _cparams(sem):
    return pltpu.CompilerParams(dimension_semantics=sem, vmem_limit_bytes=VMEM_LIMIT)


def _dot(a, b):
    return jnp.dot(a, b, preferred_element_type=F32)


def _dot_nt(a, b):
    return lax.dot_general(a, b, (((1,), (1,)), ((), ())), preferred_element_type=F32)


def _dot_tn(a, b):
    return lax.dot_general(a, b, (((0,), (0,)), ((), ())), preferred_element_type=F32)


def _layer_norm(x, g, b):
    mu = jnp.mean(x, axis=-1, keepdims=True)
    xc = x - mu
    var = jnp.mean(xc * xc, axis=-1, keepdims=True)
    return xc * lax.rsqrt(var + LN_EPS) * g + b


def _rms_norm(x, g):
    return x * lax.rsqrt(jnp.mean(x * x, axis=-1, keepdims=True) + RMS_EPS) * g


def _sigmoid(x):
    return 0.5 * jnp.tanh(0.5 * x) + 0.5


def _silu(x):
    return x * _sigmoid(x)


def _cond_of_tile(i, tile):
    first_sample = N_PROMPT // tile
    per_seq = DEC_SEQ // tile
    return jnp.where(i < first_sample, 0, 1 + (i - first_sample) // per_seq)


def _split3(x):
    hi = x.astype(BF16)
    r1 = x - hi.astype(F32)
    mid = r1.astype(BF16)
    lo = (r1 - mid.astype(F32)).astype(BF16)
    return hi, mid, lo


def _seg_sum(x, ones_bd):
    outs = []
    for g in range(D // 256):
        hi, mid, lo = _split3(x[:, 256 * g:256 * (g + 1)])
        outs.append(_dot(hi, ones_bd) + _dot(mid, ones_bd) + _dot(lo, ones_bd))
    return jnp.concatenate(outs, axis=1)


def _ada_kernel(c_ref, w_ref, b_ref, o_ref):
    a = _silu(c_ref[...]).astype(BF16)
    o_ref[...] = _dot(a, w_ref[...].astype(BF16)) + b_ref[...]


def _ada_call(cond8, ada_w, ada_b):
    tn = 1536
    return pl.pallas_call(
        _ada_kernel,
        grid=(DEPTH, 6 * D // tn),
        in_specs=[
            pl.BlockSpec((SUBLANES, D), lambda l, n: (0, 0)),
            pl.BlockSpec((None, D, tn), lambda l, n: (l, 0, n)),
            pl.BlockSpec((None, 1, tn), lambda l, n: (l, 0, n)),
        ],
        out_specs=pl.BlockSpec((None, SUBLANES, tn), lambda l, n: (l, 0, n)),
        out_shape=jax.ShapeDtypeStruct((DEPTH, SUBLANES, 6 * D), F32),
        compiler_params=_cparams(("parallel", "parallel")),
        name="adaln",
    )(cond8, ada_w, ada_b.reshape(DEPTH, 1, 6 * D))


def _act_specs(y, tile):
    if not isinstance(y, tuple):
        return [pl.BlockSpec((tile, D), lambda i: (i, 0))], [y]
    n_p = N_PROMPT // tile
    return [pl.BlockSpec((tile, D), lambda i: (jnp.minimum(i, n_p - 1), 0)),
            pl.BlockSpec((tile, D), lambda i: (jnp.maximum(i - n_p, 0), 0))], list(y)


def _act_tile(y_refs, tile):
    if len(y_refs) == 1:
        return y_refs[0][...]
    return jnp.where(pl.program_id(0) < N_PROMPT // tile, y_refs[0][...], y_refs[1][...])


def _mla_pre_kernel(n_y, *refs):
    (m_ref, wqa_ref, qn_ref, wqb_ref, wqbs_ref, wkva_ref, wkvas_ref,
     kvn_ref, cos_ref, sin_ref, q_ref, ckv_ref, kpe_ref) = refs[n_y:]
    i = pl.program_id(0)
    h = (_act_tile(refs[:n_y], TM) * (1.0 + m_ref[1:2, :]) + m_ref[0:1, :]).astype(BF16)
    qa = _rms_norm(_dot(h, wqa_ref[...]), qn_ref[...]).astype(BF16)
    kv = _dot(h, wkva_ref[...])
    ckv_ref[...] = _rms_norm(kv[:, :KV_LORA], kvn_ref[...])
    kpe = kv[:, KV_LORA:]
    is_latent = i >= N_PROMPT // TM

    @pl.when(jnp.logical_not(is_latent))
    def _():
        q_ref[...] = _dot(qa, wqb_ref[...]).astype(BF16)
        kpe_ref[...] = kpe

    @pl.when(is_latent)
    def _():
        cos = cos_ref[...]
        sin = sin_ref[...]
        kpe_ref[...] = kpe * cos + _dot(h, wkvas_ref[...]) * sin
        q = _dot(qa, wqb_ref[...])
        qs = _dot(qa, wqbs_ref[...])
        for hd in range(HEADS):
            sl = slice(HEAD_PAD * hd, HEAD_PAD * (hd + 1))
            q_ref[:, sl] = (q[:, sl] * cos + qs[:, sl] * sin).astype(BF16)


def _mla_pre_call(y, mods, w):
    first_sample = N_PROMPT // TM
    per_seq = DEC_SEQ // TM
    const = lambda i: (0, 0)
    rope_map = lambda i: (jnp.where(i < first_sample, 0, (i - first_sample) % per_seq), 0)
    y_specs, y_args = _act_specs(y, TM)
    return pl.pallas_call(
        functools.partial(_mla_pre_kernel, len(y_args)),
        grid=(N_TOK // TM,),
        in_specs=y_specs + [
            pl.BlockSpec((None, SUBLANES, D), lambda i: (_cond_of_tile(i, TM), 0, 0)),
            pl.BlockSpec((D, Q_LORA), const),
            pl.BlockSpec((1, Q_LORA), const),
            pl.BlockSpec((Q_LORA, HEADS * HEAD_PAD), const),
            pl.BlockSpec((Q_LORA, HEADS * HEAD_PAD), const),
            pl.BlockSpec((D, KV_LORA + HEAD_PAD), const),
            pl.BlockSpec((D, HEAD_PAD), const),
            pl.BlockSpec((1, KV_LORA), const),
            pl.BlockSpec((TM, HEAD_PAD), rope_map),
            pl.BlockSpec((TM, HEAD_PAD), rope_map),
        ],
        out_specs=[
            pl.BlockSpec((TM, HEADS * HEAD_PAD), lambda i: (i, 0)),
            pl.BlockSpec((TM, KV_LORA), lambda i: (i, 0)),
            pl.BlockSpec((TM, HEAD_PAD), lambda i: (i, 0)),
        ],
        out_shape=[
            jax.ShapeDtypeStruct((N_TOK, HEADS * HEAD_PAD), BF16),
            jax.ShapeDtypeStruct((N_TOK, KV_LORA), F32),
            jax.ShapeDtypeStruct((N_TOK, HEAD_PAD), F32),
        ],
        compiler_params=_cparams(("parallel",)),
        name="mla_pre",
    )(*y_args, mods, w["wq_a"], w["q_norm"], w["wq_b"], w["wq_b_sw"], w["wkv_a"], w["wkv_a_sw"],
      w["kv_norm"], w["cos"], w["sin"])


KV_TILES_PER_SEQ = (PAST + DEC_SEQ) // TM
KV_ROWS = DEC_BATCH * (PAST + DEC_SEQ) + N_PROMPT


def _kv_tile_source(i):
    n_lat = DEC_BATCH * KV_TILES_PER_SEQ
    b = jnp.minimum(i // KV_TILES_PER_SEQ, DEC_BATCH - 1)
    j = i - b * KV_TILES_PER_SEQ
    is_cache = (i < n_lat) & (j < PAST // TM)
    new_tile = N_PROMPT // TM + b * (DEC_SEQ // TM) + jnp.maximum(j - PAST // TM, 0)
    return is_cache, b * (PAST // TM) + jnp.minimum(j, PAST // TM - 1), jnp.where(i < n_lat, new_tile, i - n_lat)


def _kv_expand_kernel(cckv_ref, ckpe_ref, ckv_ref, kpe_ref, wk_ref, wv_ref, vone_ref, k_ref, v_ref):
    is_cache, _, _ = _kv_tile_source(pl.program_id(0))
    c = jnp.where(is_cache, cckv_ref[...], ckv_ref[...]).astype(BF16)
    kpe = jnp.where(is_cache, ckpe_ref[...], kpe_ref[...])
    kn = _dot(c, wk_ref[...])
    for hd in range(HEADS):
        sl = slice(HEAD_PAD * hd, HEAD_PAD * (hd + 1))
        k_ref[:, sl] = (kn[:, sl] + kpe).astype(BF16)
    v_ref[...] = (_dot(c, wv_ref[...]) + vone_ref[...]).astype(BF16)


def _kv_expand_call(cache_ckv, cache_kpe, ckv, kpe, w):
    rows = KV_ROWS
    const = lambda i: (0, 0)
    return pl.pallas_call(
        _kv_expand_kernel,
        grid=(rows // TM,),
        in_specs=[
            pl.BlockSpec((TM, KV_LORA), lambda i: (_kv_tile_source(i)[1], 0)),
            pl.BlockSpec((TM, HEAD_PAD), lambda i: (_kv_tile_source(i)[1], 0)),
            pl.BlockSpec((TM, KV_LORA), lambda i: (_kv_tile_source(i)[2], 0)),
            pl.BlockSpec((TM, HEAD_PAD), lambda i: (_kv_tile_source(i)[2], 0)),
            pl.BlockSpec((KV_LORA, HEADS * HEAD_PAD), const),
            pl.BlockSpec((KV_LORA, HEADS * HEAD_PAD), const),
            pl.BlockSpec((1, HEADS * HEAD_PAD), const),
        ],
        out_specs=[
            pl.BlockSpec((TM, HEADS * HEAD_PAD), lambda i: (i, 0)),
            pl.BlockSpec((TM, HEADS * HEAD_PAD), lambda i: (i, 0)),
        ],
        out_shape=[jax.ShapeDtypeStruct((rows, HEADS * HEAD_PAD), BF16)] * 2,
        compiler_params=_cparams(("parallel",)),
        name="kv_expand",
    )(cache_ckv, cache_kpe, ckv, kpe, w["wkv_b_k"], w["wkv_b_v"], w["v_one"])


def _sum_lane(h):
    return HEAD_PAD - 1 if h % 2 == 0 else 0


_V_ONE = np.zeros((1, HEADS * HEAD_PAD), np.float32)
_V_ONE[0, [HEAD_PAD * h + _sum_lane(h) for h in range(HEADS)]] = 1.0


def _zero_like_col(x):
    bits = pltpu.bitcast(x, jnp.uint32)
    zero = lax.shift_right_logical(lax.shift_right_logical(bits, jnp.uint32(16)), jnp.uint32(16))
    return pltpu.bitcast(zero, F32)[:, 0:1]


def _attn_kernel(t_k, q_ref, k_ref, v_ref, *rest):
    o_ref = rest[-1]
    scale2 = (NOPE + ROPE) ** -0.5 * LOG2_E
    kc = min(ATTN_KC, t_k)
    chunks = [slice(c * kc, (c + 1) * kc) for c in range(t_k // kc)]
    n_heads = q_ref.shape[1] // HEAD_PAD
    head = [slice(HEAD_PAD * e, HEAD_PAD * (e + 1)) for e in range(n_heads)]

    def scores(e, c):
        return _dot_nt(q_ref[:, head[e]], k_ref[c, head[e]])

    lane = lax.broadcasted_iota(jnp.int32, (TQ, HEAD_PAD), 1)
    s_cur = [scores(0, c) for c in chunks]
    outs = []
    for e in range(n_heads):
        m = functools.reduce(jnp.maximum, [jnp.max(s, axis=-1, keepdims=True) for s in s_cur])
        s_next, accs = [], []
        for i in range(len(chunks) + ATTN_SKEW):
            if e + 1 < n_heads and i < len(chunks):
                s_next.append(scores(e + 1, chunks[i]))
            if i >= ATTN_SKEW:
                j = i - ATTN_SKEW
                mj = m + _zero_like_col(accs[j - ATTN_GATE]) if j >= ATTN_GATE else m
                p = jnp.exp2((s_cur[j] - mj) * scale2).astype(BF16)
                pv = _dot(p, v_ref[chunks[j], head[e]])
                accs.append(pv if not accs else accs[-1] + pv)
        acc = accs[-1]
        sl = _sum_lane(e)
        l = acc[:, sl:sl + 1]
        outs.append(jnp.where(lane == sl, 0.0, acc) * (1.0 / l))
        s_cur = s_next
    for j in range(n_heads // 2):
        o_ref[:, 2 * VDIM * j:2 * VDIM * (j + 1)] = (outs[2 * j] + outs[2 * j + 1]).astype(BF16)


def _attn_call(q, k, v, o_buf, n_seq, t_q, t_k, q_row0, k_row0, name):
    nq = t_q // TQ
    q_blk0 = q_row0 // TQ
    k_blk0 = k_row0 // t_k
    heads = ATTN_HEADS if t_k > ATTN_KC else HEADS
    width = heads * HEAD_PAD
    in_specs = [
        pl.BlockSpec((TQ, width), lambda b, p, i: (q_blk0 + b * nq + i, p)),
        pl.BlockSpec((t_k, width), lambda b, p, i: (k_blk0 + b, p)),
        pl.BlockSpec((t_k, width), lambda b, p, i: (k_blk0 + b, p)),
    ]
    args = [q, k, v]
    if o_buf is not None:
        in_specs.append(pl.BlockSpec(memory_space=pl.ANY))
        args.append(o_buf)
    return pl.pallas_call(
        functools.partial(_attn_kernel, t_k),
        grid=(n_seq, HEADS // heads, nq),
        in_specs=in_specs,
        out_specs=pl.BlockSpec((TQ, heads * VDIM), lambda b, p, i: (q_blk0 + b * nq + i, p)),
        out_shape=jax.ShapeDtypeStruct((N_TOK, HEADS * VDIM), BF16),
        input_output_aliases={} if o_buf is None else {3: 0},
        compiler_params=_cparams(("parallel", "parallel", "arbitrary")),
        name=name,
    )(*args)


def _post_kernel(n_y, has_proj, split_out, *refs):
    a_ref, m_ref, lng_ref, lnb_ref = refs[n_y:n_y + 4]
    rest = refs[n_y + 4:]
    if has_proj:
        wo_ref, rest = rest[0], rest[1:]
    wg_ref, wu_ref, wd_ref = rest[:3]
    o_refs = rest[3:]
    mix = _dot(a_ref[...], wo_ref[...]) if has_proj else a_ref[...]
    y1 = _layer_norm(ALPHA * _act_tile(refs[:n_y], TM) + m_ref[2:3, :] * mix, lng_ref[0:1, :], lnb_ref[0:1, :])
    h = (y1 * (1.0 + m_ref[4:5, :]) + m_ref[3:4, :]).astype(BF16)
    acc = None
    for j in range(D_FF // TFF):
        cols = slice(TFF * j, TFF * (j + 1))
        act = (_silu(_dot(h, wg_ref[:, cols])) * _dot(h, wu_ref[:, cols])).astype(BF16)
        part = _dot(act, wd_ref[cols, :])
        acc = part if acc is None else acc + part
    out = _layer_norm(ALPHA * y1 + m_ref[5:6, :] * acc, lng_ref[1:2, :], lnb_ref[1:2, :])
    if not split_out:
        o_refs[0][...] = out
    else:
        is_prompt = pl.program_id(0) < N_PROMPT // TM

        @pl.when(is_prompt)
        def _():
            o_refs[0][...] = out

        @pl.when(jnp.logical_not(is_prompt))
        def _():
            o_refs[1][...] = out


def _post_call(y, a, mods, ln_g, ln_b, wo, wg, wu, wd, split_out=False):
    has_proj = wo is not None
    const = lambda i: (0, 0)
    resident = pl.Buffered(1)
    y_specs, y_args = _act_specs(y, TM)
    in_specs = y_specs + [
        pl.BlockSpec((TM, D), lambda i: (i, 0)),
        pl.BlockSpec((None, SUBLANES, D), lambda i: (_cond_of_tile(i, TM), 0, 0)),
        pl.BlockSpec((2, D), const),
        pl.BlockSpec((2, D), const),
    ]
    args = y_args + [a, mods, ln_g, ln_b]
    if has_proj:
        in_specs.append(pl.BlockSpec((D, D), const, pipeline_mode=resident))
        args.append(wo)
    in_specs += [
        pl.BlockSpec((D, D_FF), const, pipeline_mode=resident),
        pl.BlockSpec((D, D_FF), const, pipeline_mode=resident),
        pl.BlockSpec((D_FF, D), const, pipeline_mode=resident),
    ]
    args += [wg, wu, wd]
    if split_out:
        n_p = N_PROMPT // TM
        out_specs = [pl.BlockSpec((TM, D), lambda i: (jnp.minimum(i, n_p - 1), 0)),
                     pl.BlockSpec((TM, D), lambda i: (jnp.maximum(i - n_p, 0), 0))]
        out_shape = [jax.ShapeDtypeStruct((N_PROMPT, D), F32), jax.ShapeDtypeStruct((N_SAMPLE, D), F32)]
    else:
        out_specs = pl.BlockSpec((TM, D), lambda i: (i, 0))
        out_shape = jax.ShapeDtypeStruct((N_TOK, D), F32)
    return pl.pallas_call(
        functools.partial(_post_kernel, len(y_args), has_proj, split_out),
        grid=(N_TOK // TM,),
        in_specs=in_specs,
        out_specs=out_specs,
        out_shape=out_shape,
        compiler_params=_cparams(("arbitrary",) if split_out else ("parallel",)),
        name="post_proj" if has_proj else "post_noproj",
    )(*args)


def _mix_tile_flags(i):
    first_sample = N_PROMPT // TMIX
    tiles_prompt = SEQ // TMIX
    tiles_sample = DEC_SEQ // TMIX
    is_p = i < first_sample
    j = jnp.where(is_p, i % tiles_prompt, (i - first_sample) % tiles_sample)
    n = jnp.where(is_p, tiles_prompt, tiles_sample)
    return j > 0, j < n - 1, j * TMIX, n * TMIX


def _halo_specs():
    blocks_per_tile = TMIX // HALO
    last = N_TOK // HALO - 1
    return [
        pl.BlockSpec((HALO, D), lambda i: (jnp.maximum(i * blocks_per_tile - 1, 0), 0)),
        pl.BlockSpec((TMIX, D), lambda i: (i, 0)),
        pl.BlockSpec((HALO, D), lambda i: (jnp.minimum((i + 1) * blocks_per_tile, last), 0)),
    ]


def _fill_ext(ext_ref, yp_ref, y_ref, yn_ref, m_ref, has_prev, has_next):
    scale1 = 1.0 + m_ref[1:2, :]
    shift = m_ref[0:1, :]
    h = y_ref[...] * scale1 + shift
    ext_ref[0:HALO, :] = jnp.where(has_prev, yp_ref[...] * scale1 + shift, 0.0)
    ext_ref[HALO:HALO + TMIX, :] = h
    ext_ref[HALO + TMIX:, :] = jnp.where(has_next, yn_ref[...] * scale1 + shift, 0.0)
    return h


def _rwkv_pre_kernel(yp_ref, y_ref, yn_ref, m_ref, mu_ref, wr_ref, wk_ref, wv_ref, g1_ref, g2_ref,
                     w1_ref, w2_ref, a1_ref, a2_ref, w0_ref, a0_ref, kk_ref, ka_ref, ones_ref,
                     r_out, v_out, g_out, kk_out, kd_out, b_out, ld_out, ext_ref):
    i = pl.program_id(0)
    has_prev, has_next, _, _ = _mix_tile_flags(i)
    h = _fill_ext(ext_ref, yp_ref, y_ref, yn_ref, m_ref, has_prev, has_next)
    prev = ext_ref[HALO - 1:HALO - 1 + TMIX, :]
    nxt = ext_ref[HALO + 1:HALO + 1 + TMIX, :]
    xx = 0.5 * (prev + nxt) - h

    def mix(j):
        return (h + xx * mu_ref[j:j + 1, :]).astype(BF16)

    r_out[...] = _dot(mix(0), wr_ref[...])
    k = _dot(mix(2), wk_ref[...])
    v_out[...] = _dot(mix(3), wv_ref[...])
    g_out[...] = _dot(_sigmoid(_dot(mix(5), g1_ref[...])).astype(BF16), g2_ref[...])
    tw = jnp.tanh(_dot(mix(1), w1_ref[...])).astype(BF16)
    ta = _dot(mix(4), a1_ref[...]).astype(BF16)

    kkf = k * kk_ref[...]
    kk = kkf * jnp.minimum(lax.rsqrt(_seg_sum(kkf * kkf, ones_ref[...])), 1e12)
    kk_out[...] = kk
    for d in range(2):
        ld_out[d] = -EXP_M_HALF * _sigmoid(w0_ref[d:d + 1, :] + _dot(tw, w2_ref[d]))
        a = _sigmoid(a0_ref[d:d + 1, :] + _dot(ta, a2_ref[d]))
        kd_out[d] = k * (1.0 + (a - 1.0) * ka_ref[...])
        b_out[d] = kk * a


def _rwkv_pre_call(y, mods, w):
    const2 = lambda i: (0, 0)
    const3 = lambda i: (0, 0, 0)
    row = pl.BlockSpec((TMIX, D), lambda i: (i, 0))
    row2 = pl.BlockSpec((2, TMIX, D), lambda i: (0, i, 0))
    one = jax.ShapeDtypeStruct((N_TOK, D), F32)
    two = jax.ShapeDtypeStruct((2, N_TOK, D), F32)
    return pl.pallas_call(
        _rwkv_pre_kernel,
        grid=(N_TOK // TMIX,),
        in_specs=_halo_specs() + [
            pl.BlockSpec((None, SUBLANES, D), lambda i: (_cond_of_tile(i, TMIX), 0, 0)),
            pl.BlockSpec((6, D), const2),
            pl.BlockSpec((D, D), const2), pl.BlockSpec((D, D), const2), pl.BlockSpec((D, D), const2),
            pl.BlockSpec((D, 128), const2), pl.BlockSpec((128, D), const2),
            pl.BlockSpec((D, 128), const2), pl.BlockSpec((2, 128, D), const3),
            pl.BlockSpec((D, 128), const2), pl.BlockSpec((2, 128, D), const3),
            pl.BlockSpec((2, D), const2), pl.BlockSpec((2, D), const2),
            pl.BlockSpec((1, D), const2), pl.BlockSpec((1, D), const2),
            pl.BlockSpec((256, 256), const2),
        ],
        out_specs=[row, row, row, row, row2, row2, row2],
        out_shape=[one, one, one, one, two, two, two],
        scratch_shapes=[pltpu.VMEM((TMIX + 2 * HALO, D), F32)],
        compiler_params=_cparams(("parallel",)),
        name="rwkv_pre",
    )(y, y, y, mods, w["mu"], w["wr"], w["wk"], w["wv"], w["g1"], w["g2"], w["w1"], w["w2"],
      w["a1"], w["a2"], w["w0"], w["a0"], w["k_k"], w["k_a"], w["ones_bd"])


def _scan_kernel(n_chunks, has_s0, aliased, *refs):
    ins = [refs[0:6], refs[6:12]]
    pos = 12
    s0_ref = refs[pos] if has_s0 else None
    pos += (1 if has_s0 else 0) + (2 if aliased else 0)
    y_refs = refs[pos:pos + 2]
    sf_ref, s_ref = refs[pos + 2], refs[pos + 3]
    n = pl.program_id(1)
    C = CHUNK
    P2 = 2 * C
    units = [(d, p) for d in range(2) for p in range(HEADS // 2)]
    lanes = [slice(2 * RWKV_N * p, 2 * RWKV_N * (p + 1)) for p in range(HEADS // 2)]

    @pl.when(n == 0)
    def _():
        s_ref[...] = jnp.zeros_like(s_ref)
        if has_s0:
            for d in range(2):
                for hd in range(HEADS):
                    o = RWKV_N * (hd % 2)
                    s_ref[d, hd // 2, o:o + RWKV_N, o:o + RWKV_N] = s0_ref[d, hd]

    lane = lax.broadcasted_iota(jnp.int32, (C, 2 * RWKV_N), 1)
    even = lane < RWKV_N
    ti = lax.broadcasted_iota(jnp.int32, (C, C), 0)
    si = lax.broadcasted_iota(jnp.int32, (C, C), 1)
    r2 = lax.broadcasted_iota(jnp.int32, (P2, P2), 0)
    c2 = lax.broadcasted_iota(jnp.int32, (P2, P2), 1)
    same = (r2 // C) == (c2 // C)
    eye = (r2 == c2).astype(F32)

    def same_block(size):
        return (r2 // size) == (c2 // size)

    leaf = same_block(2)
    merges = [same_block(2 * size) & jnp.logical_not(same_block(size)) for size in (2, 4, 8, 16, 32)]

    def split(x):
        return jnp.concatenate([jnp.where(even, x, 0.0), jnp.where(even, 0.0, x)], axis=0)

    def twice(x):
        return jnp.concatenate([x, x], axis=0)

    m_incl, m_strict, at, rt, bt, kt, bh, kh, pc, v = ([None, None] for _ in range(10))
    for d in range(2):
        r_ref, v_ref, kk_ref, kd_ref, b_ref, ld_ref = ins[d]
        sign = 1 - 2 * d
        cum = (sign * (si - ti) <= 0).astype(BF16)
        order = sign * (c2 % C - r2 % C)
        m_incl[d] = same & (order <= 0)
        m_strict[d] = same & (order < 0)
        ld = ld_ref[...]
        ld_hi, ld_mid, ld_lo = _split3(ld)
        cs = _dot(cum, ld_hi) + _dot(cum, ld_mid) + _dot(cum, ld_lo)
        last = cs[0:1, :] if d == 1 else cs[C - 1:C, :]
        pin = jnp.exp(cs)
        pinv = jnp.exp(-cs)
        at[d] = -kk_ref[...] * jnp.exp(cs - ld)
        rt[d] = r_ref[...] * pin
        bt[d] = b_ref[...] * pinv
        kt[d] = kd_ref[...] * pinv
        tail = jnp.exp(last - cs)
        bh[d] = b_ref[...] * tail
        kh[d] = kd_ref[...] * tail
        pc[d] = jnp.exp(last)
        v[d] = v_ref[...]

    s0 = [s_ref[d, p] for d, p in units]
    lhs = [jnp.concatenate([split(at[d][:, lanes[p]]), split(rt[d][:, lanes[p]])], axis=0).astype(BF16)
           for d, p in units]
    rhs = [jnp.concatenate([twice(bt[d][:, lanes[p]]), twice(kt[d][:, lanes[p]])], axis=0).astype(BF16)
           for d, p in units]
    v2b = [split(v[d][:, lanes[p]]).astype(BF16) for d, p in units]
    idx = range(len(units))
    g = [_dot_nt(lhs[u], rhs[u]) for u in idx]
    xs = [_dot_nt(lhs[u], s0[u].astype(BF16)) for u in idx]
    a_ab = [jnp.where(m_strict[units[u][0]], g[u][:P2, :P2], 0.0) for u in idx]
    inv = [eye + jnp.where(leaf, a_ab[u], 0.0) for u in idx]
    for mk in merges:
        invb = [inv[u].astype(BF16) for u in idx]
        binv = [_dot(jnp.where(mk, a_ab[u], 0.0).astype(BF16), invb[u]).astype(BF16) for u in idx]
        inv = [inv[u] + _dot(invb[u], binv[u]) for u in idx]
    a_ak = [jnp.where(m_strict[units[u][0]], g[u][:P2, P2:], 0.0).astype(BF16) for u in idx]
    rhs_u = [(xs[u][:P2] + _dot(a_ak[u], v2b[u])).astype(BF16) for u in idx]
    u2b = [_dot(inv[u].astype(BF16), rhs_u[u]).astype(BF16) for u in idx]
    m_rb = [jnp.where(m_incl[units[u][0]], g[u][P2:, :P2], 0.0).astype(BF16) for u in idx]
    m_rk = [jnp.where(m_incl[units[u][0]], g[u][P2:, P2:], 0.0).astype(BF16) for u in idx]
    for u, (d, p) in enumerate(units):
        y2 = xs[u][P2:] + _dot(m_rb[u], u2b[u]) + _dot(m_rk[u], v2b[u])
        y_refs[d][:, lanes[p]] = y2[:C] + y2[C:]
    for u, (d, p) in enumerate(units):
        bh2 = split(bh[d][:, lanes[p]]).astype(BF16)
        kh2 = split(kh[d][:, lanes[p]]).astype(BF16)
        s_ref[d, p] = s0[u] * pc[d][:, lanes[p]] + _dot_tn(u2b[u], bh2) + _dot_tn(v2b[u], kh2)

    @pl.when(n == n_chunks - 1)
    def _():
        for d in range(2):
            for hd in range(HEADS):
                o = RWKV_N * (hd % 2)
                sf_ref[d, hd] = s_ref[d, hd // 2, o:o + RWKV_N, o:o + RWKV_N]


def _scan_call(r, v, kk, kd, b, ld, s0, y_bufs, n_seq, t_seq, row0, name):
    n_chunks = t_seq // CHUNK
    blk0 = row0 // CHUNK
    chunk = [lambda bb, nn: blk0 + bb * n_chunks + nn,
             lambda bb, nn: blk0 + bb * n_chunks + n_chunks - 1 - nn]
    in_specs, args = [], []
    for d in range(2):
        row = pl.BlockSpec((CHUNK, D), lambda bb, nn, d=d: (chunk[d](bb, nn), 0))
        row2 = pl.BlockSpec((None, CHUNK, D), lambda bb, nn, d=d: (d, chunk[d](bb, nn), 0))
        in_specs += [row, row, row, row2, row2, row2]
        args += [r, v, kk, kd, b, ld]
    st = pl.BlockSpec((None, 2, HEADS, RWKV_N, RWKV_N), lambda bb, nn: (bb, 0, 0, 0, 0))
    if s0 is not None:
        in_specs.append(st)
        args.append(s0)
    aliases = {}
    if y_bufs is not None:
        aliases = {len(args): 0, len(args) + 1: 1}
        in_specs += [pl.BlockSpec(memory_space=pl.ANY)] * 2
        args += list(y_bufs)
    out = pl.pallas_call(
        functools.partial(_scan_kernel, n_chunks, s0 is not None, y_bufs is not None),
        grid=(n_seq, n_chunks),
        in_specs=in_specs,
        out_specs=[
            pl.BlockSpec((CHUNK, D), lambda bb, nn: (chunk[0](bb, nn), 0)),
            pl.BlockSpec((CHUNK, D), lambda bb, nn: (chunk[1](bb, nn), 0)),
            st,
        ],
        out_shape=[
            jax.ShapeDtypeStruct((N_TOK, D), F32),
            jax.ShapeDtypeStruct((N_TOK, D), F32),
            jax.ShapeDtypeStruct((n_seq, 2, HEADS, RWKV_N, RWKV_N), F32),
        ],
        scratch_shapes=[pltpu.VMEM((2, HEADS // 2, 2 * RWKV_N, 2 * RWKV_N), F32)],
        input_output_aliases=aliases,
        compiler_params=_cparams(("parallel", "arbitrary")),
        name=name,
    )(*args)
    return (out[0], out[1]), out[2]


def _rwkv_mid_kernel(yf_ref, yb_ref, r_ref, kd_ref, v_ref, g_ref, rk_ref, lg_ref, lb_ref, ones_ref, o_ref):
    ones = ones_ref[...]
    y = yf_ref[...] + yb_ref[...]
    mu = _seg_sum(y, ones) * (1.0 / RWKV_N)
    yc = y - mu
    var = _seg_sum(yc * yc, ones) * (1.0 / RWKV_N)
    yn = yc * lax.rsqrt(var + RWKV_GN_EPS) * lg_ref[...] + lb_ref[...]
    bonus = _seg_sum(r_ref[...] * (kd_ref[0] + kd_ref[1]) * rk_ref[...], ones) * v_ref[...]
    o_ref[...] = ((yn + bonus) * g_ref[...]).astype(BF16)


def _rwkv_mid_call(y_f, y_b, r, kd, v, g, w):
    const2 = lambda i: (0, 0)
    row = pl.BlockSpec((TM, D), lambda i: (i, 0))
    row2 = pl.BlockSpec((2, TM, D), lambda i: (0, i, 0))
    return pl.pallas_call(
        _rwkv_mid_kernel,
        grid=(N_TOK // TM,),
        in_specs=[row, row, row, row2, row, row,
                  pl.BlockSpec((1, D), const2), pl.BlockSpec((1, D), const2), pl.BlockSpec((1, D), const2),
                  pl.BlockSpec((256, 256), const2)],
        out_specs=row,
        out_shape=jax.ShapeDtypeStruct((N_TOK, D), BF16),
        compiler_params=_cparams(("parallel",)),
        name="rwkv_mid",
    )(y_f, y_b, r, kd, v, g, w["r_k"], w["lnx_g"], w["lnx_b"], w["ones_bd"])


def _pool_kernel(yp_ref, y_ref, yn_ref, m_ref, w_ref, sc_ref, o_ref, ext_ref):
    i = pl.program_id(0)
    has_prev, has_next, pos0, seq_len = _mix_tile_flags(i)
    _fill_ext(ext_ref, yp_ref, y_ref, yn_ref, m_ref, has_prev, has_next)
    t = pos0 + lax.broadcasted_iota(jnp.int32, (TMIX, 1), 0)
    for gi, win in enumerate(POOL_WINDOWS):
        cols = slice(POOL_C * gi, POOL_C * (gi + 1))
        acc = None
        for j in range(-(win // 2), win - win // 2):
            x = ext_ref[HALO + j:HALO + j + TMIX, cols]
            acc = x if acc is None else acc + x
        lo = jnp.maximum(t - win // 2, 0)
        hi = jnp.minimum(t - win // 2 + win, seq_len)
        cnt = (hi - lo).astype(F32)
        pooled = (acc / cnt - ext_ref[HALO:HALO + TMIX, cols]).astype(BF16)
        o_ref[:, cols] = _dot(pooled, w_ref[gi]) * sc_ref[:, cols]


def _pool_call(y, mods, pool_w, pool_scale):
    return pl.pallas_call(
        _pool_kernel,
        grid=(N_TOK // TMIX,),
        in_specs=_halo_specs() + [
            pl.BlockSpec((None, SUBLANES, D), lambda i: (_cond_of_tile(i, TMIX), 0, 0)),
            pl.BlockSpec((4, POOL_C, POOL_C), lambda i: (0, 0, 0)),
            pl.BlockSpec((1, D), lambda i: (0, 0)),
        ],
        out_specs=pl.BlockSpec((TMIX, D), lambda i: (i, 0)),
        out_shape=jax.ShapeDtypeStruct((N_TOK, D), F32),
        scratch_shapes=[pltpu.VMEM((TMIX + 2 * HALO, D), F32)],
        compiler_params=_cparams(("parallel",)),
        name="pool",
    )(y, y, y, mods, pool_w, pool_scale)


def _rope_tables():
    rows = DEC_SEQ // GRID_W
    row = jnp.repeat(jnp.arange(rows, dtype=F32), GRID_W)
    col = jnp.tile(jnp.arange(GRID_W, dtype=F32), rows)
    n_freq = ROPE // 4
    inv_freq = ROPE_THETA ** (-jnp.arange(n_freq, dtype=F32) / n_freq)
    ang = jnp.stack([row[:, None] * inv_freq, col[:, None] * inv_freq], axis=1)
    cos, sin = jnp.cos(ang), jnp.sin(ang)
    cos32 = jnp.concatenate([cos, cos], axis=-1).reshape(DEC_SEQ, ROPE)
    sin32 = jnp.concatenate([-sin, sin], axis=-1).reshape(DEC_SEQ, ROPE)
    ones = jnp.ones((DEC_SEQ, NOPE), F32)
    zeros = jnp.zeros((DEC_SEQ, NOPE), F32)
    tail = HEAD_PAD - NOPE - ROPE
    cos_t = jnp.concatenate([ones, cos32, jnp.ones((DEC_SEQ, tail), F32)], axis=1)
    sin_t = jnp.concatenate([zeros, sin32, jnp.zeros((DEC_SEQ, tail), F32)], axis=1)
    return cos_t, sin_t


def _swap_halves(w):
    s = w.reshape(*w.shape[:-1], 2, 2, ROPE // 4)
    return s[..., ::-1, :].reshape(w.shape)


def _pad_heads(w):
    k, _, width = w.shape
    return jnp.pad(w, ((0, 0), (0, 0), (0, HEAD_PAD - width))).reshape(k, HEADS * HEAD_PAD)


def _pad_heads_alternating(w):
    k = w.shape[0]
    pair = w.reshape(k, HEADS // 2, 2, VDIM)
    z = jnp.zeros((k, HEADS // 2, VDIM), w.dtype)
    return jnp.concatenate([pair[:, :, 0], z, z, pair[:, :, 1]], axis=-1).reshape(k, HEADS * HEAD_PAD)


def _mla_weights(wq_a, q_norm, wq_b, wkv_a, kv_norm, wkv_b, wo, cos_t, sin_t):
    qk = NOPE + ROPE
    wq_b3 = wq_b.reshape(Q_LORA, HEADS, qk)
    q_sw = jnp.concatenate([jnp.zeros((Q_LORA, HEADS, NOPE), F32), _swap_halves(wq_b3[..., NOPE:])], axis=-1)
    w_pe = wkv_a[:, KV_LORA:]
    pad_pe = lambda x: jnp.pad(x, ((0, 0), (NOPE, HEAD_PAD - NOPE - ROPE)))
    wkv_b3 = wkv_b.reshape(KV_LORA, HEADS, NOPE + VDIM)
    return {
        "wq_a": wq_a.astype(BF16),
        "q_norm": q_norm.reshape(1, Q_LORA),
        "wq_b": _pad_heads(wq_b3).astype(BF16),
        "wq_b_sw": _pad_heads(q_sw).astype(BF16),
        "wkv_a": jnp.concatenate([wkv_a[:, :KV_LORA], pad_pe(w_pe)], axis=1).astype(BF16),
        "wkv_a_sw": pad_pe(_swap_halves(w_pe)).astype(BF16),
        "kv_norm": kv_norm.reshape(1, KV_LORA),
        "wkv_b_k": _pad_heads(wkv_b3[..., :NOPE]).astype(BF16),
        "wkv_b_v": _pad_heads_alternating(wkv_b3[..., NOPE:]).astype(BF16),
        "v_one": jnp.asarray(_V_ONE),
        "wo": wo.astype(BF16),
        "cos": cos_t,
        "sin": sin_t,
    }


def _dir_pad(w):
    z = jnp.zeros_like(w[0])
    return jnp.stack([jnp.concatenate([w[0], z], axis=0), jnp.concatenate([z, w[1]], axis=0)])


def _rwkv_weights(mu, wr, wk, wv, w0, w1, w2, a0, a1, a2, g1, g2, k_k, k_a, r_k, lnx_g, lnx_b, wo):
    blk = np.arange(256) // RWKV_N
    return {
        "mu": mu,
        "wr": wr.astype(BF16), "wk": wk.astype(BF16), "wv": wv.astype(BF16),
        "g1": g1.astype(BF16), "g2": g2.astype(BF16),
        "w1": jnp.concatenate([w1[0], w1[1]], axis=1).astype(BF16),
        "w2": _dir_pad(w2).astype(BF16),
        "a1": jnp.concatenate([a1[0], a1[1]], axis=1).astype(BF16),
        "a2": _dir_pad(a2).astype(BF16),
        "w0": w0, "a0": a0,
        "k_k": k_k.reshape(1, D), "k_a": k_a.reshape(1, D),
        "r_k": r_k.reshape(1, D), "lnx_g": lnx_g.reshape(1, D), "lnx_b": lnx_b.reshape(1, D),
        "wo": wo.astype(BF16),
        "ones_bd": jnp.asarray(blk[:, None] == blk[None, :], BF16),
    }


def _mla_layer(y, mods, w, cache_ckv, cache_kpe):
    q, ckv, kpe = _mla_pre_call(y, mods, w)
    cache_kpe_pad = jnp.pad(cache_kpe, ((0, 0), (0, 0), (NOPE, HEAD_PAD - NOPE - ROPE)))
    t_k = PAST + DEC_SEQ
    k_all, v_all = _kv_expand_call(cache_ckv.reshape(DEC_BATCH * PAST, KV_LORA),
                                   cache_kpe_pad.reshape(DEC_BATCH * PAST, HEAD_PAD), ckv, kpe, w)
    o = _attn_call(q, k_all, v_all, None, BATCH, SEQ, SEQ, 0, DEC_BATCH * t_k, "attn_prompt")
    o = _attn_call(q, k_all, v_all, o, DEC_BATCH, DEC_SEQ, t_k, N_PROMPT, 0, "attn_sample")
    new_ckv = ckv[:N_PROMPT].reshape(BATCH, SEQ, KV_LORA)
    new_kpe = kpe[:N_PROMPT, NOPE:NOPE + ROPE].reshape(BATCH, SEQ, ROPE)
    return o, new_ckv, new_kpe


def _rwkv_layer(y, mods, w, state):
    r, v, g, kk, kd, b, ld = _rwkv_pre_call(y, mods, w)
    y_fb, s_p = _scan_call(r, v, kk, kd, b, ld, None, None, BATCH, SEQ, 0, "scan_prompt")
    y_fb, _ = _scan_call(r, v, kk, kd, b, ld, state, y_fb, DEC_BATCH, DEC_SEQ, N_PROMPT, "scan_sample")
    return _rwkv_mid_call(y_fb[0], y_fb[1], r, kd, v, g, w), s_p


def kernel(x_prompt, x_sample, cache_ckv, cache_kpe, state_wkv, c, c_ctx, ada_w, ada_b, ln_g, ln_b, ffn_wg, ffn_wu, ffn_wd, mla_wq_a, mla_q_norm, mla_wq_b, mla_wkv_a, mla_kv_norm, mla_wkv_b, mla_wo, rwkv_mu, rwkv_wr, rwkv_wk, rwkv_wv, rwkv_w0, rwkv_w1, rwkv_w2, rwkv_a0, rwkv_a1, rwkv_a2, rwkv_g1, rwkv_g2, rwkv_k_k, rwkv_k_a, rwkv_r_k, rwkv_lnx_g, rwkv_lnx_b, rwkv_wo, pool_w, pool_scale):
    y = (x_prompt.reshape(N_PROMPT, D), x_sample.reshape(N_SAMPLE, D))
    cond8 = jnp.concatenate([c_ctx[None, :], c, jnp.zeros((SUBLANES - 1 - DEC_BATCH, D), F32)], axis=0)
    mods_all = _ada_call(cond8, ada_w, ada_b)
    mods_all = jnp.pad(mods_all.reshape(DEPTH, SUBLANES, 6, D)[:, :1 + DEC_BATCH],
                       ((0, 0), (0, 0), (0, SUBLANES - 6), (0, 0)))
    cos_t, sin_t = _rope_tables()
    new_ckv, new_kpe, new_wkv = [], [], []
    for layer in range(DEPTH):
        kind, j = layer % 3, layer // 3
        mods = mods_all[layer]
        wo = None
        if kind == 0:
            w = _mla_weights(mla_wq_a[j], mla_q_norm[j], mla_wq_b[j], mla_wkv_a[j], mla_kv_norm[j],
                             mla_wkv_b[j], mla_wo[j], cos_t, sin_t)
            a, ckv, kpe = _mla_layer(y, mods, w, cache_ckv[:, j], cache_kpe[:, j])
            new_ckv.append(ckv)
            new_kpe.append(kpe)
            wo = w["wo"]
        elif kind == 1:
            w = _rwkv_weights(rwkv_mu[j], rwkv_wr[j], rwkv_wk[j], rwkv_wv[j], rwkv_w0[j], rwkv_w1[j],
                              rwkv_w2[j], rwkv_a0[j], rwkv_a1[j], rwkv_a2[j], rwkv_g1[j], rwkv_g2[j],
                              rwkv_k_k[j], rwkv_k_a[j], rwkv_r_k[j], rwkv_lnx_g[j], rwkv_lnx_b[j], rwkv_wo[j])
            a, s_new = _rwkv_layer(y, mods, w, state_wkv[:, j])
            new_wkv.append(s_new)
            wo = w["wo"]
        else:
            a = _pool_call(y, mods, pool_w[j].astype(BF16), pool_scale[j].reshape(1, D))
        y = _post_call(y, a, mods, ln_g[layer], ln_b[layer], wo,
                       ffn_wg[layer].astype(BF16), ffn_wu[layer].astype(BF16), ffn_wd[layer].astype(BF16),
                       split_out=layer == DEPTH - 1)
    return (y[0].reshape(BATCH, SEQ, D), y[1].reshape(DEC_BATCH, DEC_SEQ, D),
            jnp.stack(new_ckv, axis=1), jnp.stack(new_kpe, axis=1), jnp.stack(new_wkv, axis=1))
```

```python
import functools

import jax
import jax.numpy as jnp
import numpy as np
from jax import lax
from jax.experimental import pallas as pl
from jax.experimental.pallas import tpu as pltpu

F32 = jnp.float32
BF16 = jnp.bfloat16

D = 1024
BATCH, SEQ = 32, 256
DEC_BATCH, DEC_SEQ = 2, 4096
PAST = 512
DEPTH = 4
GRID_W = 64
HEADS = 16
Q_LORA, KV_LORA = 384, 256
NOPE, ROPE, VDIM = 64, 32, 64
ROPE_THETA = 10000.0
RWKV_N = 64
RWKV_GN_EPS = 64e-5
POOL_WINDOWS = (2, 4, 8, 16)
POOL_C = D // 4
D_FF = 2816
ALPHA = (2 * DEPTH) ** 0.25
LN_EPS = 1e-5
RMS_EPS = 1e-6
LOG2_E = 1.4426950408889634
EXP_M_HALF = 0.6065306597126334

N_PROMPT = BATCH * SEQ
N_SAMPLE = DEC_BATCH * DEC_SEQ
N_TOK = N_PROMPT + N_SAMPLE

LANES = 128
SUBLANES = 8
HEAD_PAD = 128
VMEM_LIMIT = 56 * 1024 * 1024

TM = 512
TMIX = 256
HALO = SUBLANES
TQ = 256
ATTN_HEADS = 4
ATTN_KC = 512
ATTN_SKEW = 2
ATTN_GATE = 3
CHUNK = 64
TFF = 256


def _cparams(sem):
    return pltpu.CompilerParams(dimension_semantics=sem, vmem_limit_bytes=VMEM_LIMIT)


def _dot(a, b):
    return jnp.dot(a, b, preferred_element_type=F32)


def _dot_nt(a, b):
    return lax.dot_general(a, b, (((1,), (1,)), ((), ())), preferred_element_type=F32)


def _dot_tn(a, b):
    return lax.dot_general(a, b, (((0,), (0,)), ((), ())), preferred_element_type=F32)


def _layer_norm(x, g, b):
    mu = jnp.mean(x, axis=-1, keepdims=True)
    xc = x - mu
    var = jnp.mean(xc * xc, axis=-1, keepdims=True)
    return xc * lax.rsqrt(var + LN_EPS) * g + b


def _rms_norm(x, g):
    return x * lax.rsqrt(jnp.mean(x * x, axis=-1, keepdims=True) + RMS_EPS) * g


def _sigmoid(x):
    return 0.5 * jnp.tanh(0.5 * x) + 0.5


def _silu(x):
    return x * _sigmoid(x)


def _cond_of_tile(i, tile):
    first_sample = N_PROMPT // tile
    per_seq = DEC_SEQ // tile
    return jnp.where(i < first_sample, 0, 1 + (i - first_sample) // per_seq)


def _split3(x):
    hi = x.astype(BF16)
    r1 = x - hi.astype(F32)
    mid = r1.astype(BF16)
    lo = (r1 - mid.astype(F32)).astype(BF16)
    return hi, mid, lo


def _seg_sum(x, ones_bd):
    outs = []
    for g in range(D // 256):
        hi, mid, lo = _split3(x[:, 256 * g:256 * (g + 1)])
        outs.append(_dot(hi, ones_bd) + _dot(mid, ones_bd) + _dot(lo, ones_bd))
    return jnp.concatenate(outs, axis=1)


def _ada_kernel(c_ref, w_ref, b_ref, o_ref):
    a = _silu(c_ref[...]).astype(BF16)
    o_ref[...] = _dot(a, w_ref[...].astype(BF16)) + b_ref[...]


def _ada_call(cond8, ada_w, ada_b):
    tn = 1536
    return pl.pallas_call(
        _ada_kernel,
        grid=(DEPTH, 6 * D // tn),
        in_specs=[
            pl.BlockSpec((SUBLANES, D), lambda l, n: (0, 0)),
            pl.BlockSpec((None, D, tn), lambda l, n: (l, 0, n)),
            pl.BlockSpec((None, 1, tn), lambda l, n: (l, 0, n)),
        ],
        out_specs=pl.BlockSpec((None, SUBLANES, tn), lambda l, n: (l, 0, n)),
        out_shape=jax.ShapeDtypeStruct((DEPTH, SUBLANES, 6 * D), F32),
        compiler_params=_cparams(("parallel", "parallel")),
        name="adaln",
    )(cond8, ada_w, ada_b.reshape(DEPTH, 1, 6 * D))


def _act_specs(y, tile):
    if not isinstance(y, tuple):
        return [pl.BlockSpec((tile, D), lambda i: (i, 0))], [y]
    n_p = N_PROMPT // tile
    return [pl.BlockSpec((tile, D), lambda i: (jnp.minimum(i, n_p - 1), 0)),
            pl.BlockSpec((tile, D), lambda i: (jnp.maximum(i - n_p, 0), 0))], list(y)


def _act_tile(y_refs, tile):
    if len(y_refs) == 1:
        return y_refs[0][...]
    return jnp.where(pl.program_id(0) < N_PROMPT // tile, y_refs[0][...], y_refs[1][...])


def _mla_pre_kernel(n_y, *refs):
    m_ref, wa_ref, qn_ref, wqb_ref, kvn_ref, cos_ref, sin_ref, q_ref, ckv_ref, kpe_ref = refs[n_y:]
    i = pl.program_id(0)
    h = (_act_tile(refs[:n_y], TM) * (1.0 + m_ref[1:2, :]) + m_ref[0:1, :]).astype(BF16)
    x = _dot(h, wa_ref[...])
    qa = _rms_norm(x[:, :Q_LORA], qn_ref[...]).astype(BF16)
    ckv_ref[...] = _rms_norm(x[:, Q_LORA:Q_LORA + KV_LORA], kvn_ref[...])
    kpe = x[:, Q_LORA + KV_LORA:]
    q = _dot(qa, wqb_ref[...])
    is_latent = i >= N_PROMPT // TM
    swap_shift = HEAD_PAD - ROPE

    @pl.when(jnp.logical_not(is_latent))
    def _():
        q_ref[...] = q.astype(BF16)
        lane = lax.broadcasted_iota(jnp.int32, kpe.shape, 1)
        kpe_ref[...] = jnp.where(lane < NOPE + ROPE, kpe, 0.0)

    @pl.when(is_latent)
    def _():
        cos = cos_ref[...]
        sin = sin_ref[...]
        kpe_ref[...] = kpe * cos + pltpu.roll(kpe, swap_shift, 1) * sin
        for hd in range(HEADS):
            sl = slice(HEAD_PAD * hd, HEAD_PAD * (hd + 1))
            q_ref[:, sl] = (q[:, sl] * cos + pltpu.roll(q[:, sl], swap_shift, 1) * sin).astype(BF16)


def _mla_pre_call(y, mods, w):
    first_sample = N_PROMPT // TM
    per_seq = DEC_SEQ // TM
    const = lambda i: (0, 0)
    rope_map = lambda i: (jnp.where(i < first_sample, 0, (i - first_sample) % per_seq), 0)
    y_specs, y_args = _act_specs(y, TM)
    return pl.pallas_call(
        functools.partial(_mla_pre_kernel, len(y_args)),
        grid=(N_TOK // TM,),
        in_specs=y_specs + [
            pl.BlockSpec((None, SUBLANES, D), lambda i: (_cond_of_tile(i, TM), 0, 0)),
            pl.BlockSpec((D, Q_LORA + KV_LORA + HEAD_PAD), const),
            pl.BlockSpec((1, Q_LORA), const),
            pl.BlockSpec((Q_LORA, HEADS * HEAD_PAD), const),
            pl.BlockSpec((1, KV_LORA), const),
            pl.BlockSpec((TM, HEAD_PAD), rope_map),
            pl.BlockSpec((TM, HEAD_PAD), rope_map),
        ],
        out_specs=[
            pl.BlockSpec((TM, HEADS * HEAD_PAD), lambda i: (i, 0)),
            pl.BlockSpec((TM, KV_LORA), lambda i: (i, 0)),
            pl.BlockSpec((TM, HEAD_PAD), lambda i: (i, 0)),
        ],
        out_shape=[
            jax.ShapeDtypeStruct((N_TOK, HEADS * HEAD_PAD), BF16),
            jax.ShapeDtypeStruct((N_TOK, KV_LORA), F32),
            jax.ShapeDtypeStruct((N_TOK, HEAD_PAD), F32),
        ],
        compiler_params=_cparams(("parallel",)),
        name="mla_pre",
    )(*y_args, mods, w["w_a"], w["q_norm"], w["wq_b"], w["kv_norm"], w["cos"], w["sin"])


KV_TILES_PER_SEQ = (PAST + DEC_SEQ) // TM
KV_ROWS = DEC_BATCH * (PAST + DEC_SEQ) + N_PROMPT


def _kv_tile_source(i):
    n_lat = DEC_BATCH * KV_TILES_PER_SEQ
    b = jnp.minimum(i // KV_TILES_PER_SEQ, DEC_BATCH - 1)
    j = i - b * KV_TILES_PER_SEQ
    is_cache = (i < n_lat) & (j < PAST // TM)
    new_tile = N_PROMPT // TM + b * (DEC_SEQ // TM) + jnp.maximum(j - PAST // TM, 0)
    return is_cache, b * (PAST // TM) + jnp.minimum(j, PAST // TM - 1), jnp.where(i < n_lat, new_tile, i - n_lat)


def _kv_expand_kernel(cckv_ref, ckpe_ref, ckv_ref, kpe_ref, wk_ref, wv_ref, vone_ref, k_ref, v_ref):
    is_cache, _, _ = _kv_tile_source(pl.program_id(0))
    c = jnp.where(is_cache, cckv_ref[...], ckv_ref[...]).astype(BF16)
    kpe = jnp.where(is_cache, ckpe_ref[...], kpe_ref[...])
    kn = _dot(c, wk_ref[...])
    for hd in range(HEADS):
        sl = slice(HEAD_PAD * hd, HEAD_PAD * (hd + 1))
        k_ref[:, sl] = (kn[:, sl] + kpe).astype(BF16)
    v_ref[...] = (_dot(c, wv_ref[...]) + vone_ref[...]).astype(BF16)


def _kv_expand_call(cache_ckv, cache_kpe, ckv, kpe, w):
    rows = KV_ROWS
    const = lambda i: (0, 0)
    return pl.pallas_call(
        _kv_expand_kernel,
        grid=(rows // TM,),
        in_specs=[
            pl.BlockSpec((TM, KV_LORA), lambda i: (_kv_tile_source(i)[1], 0)),
            pl.BlockSpec((TM, HEAD_PAD), lambda i: (_kv_tile_source(i)[1], 0)),
            pl.BlockSpec((TM, KV_LORA), lambda i: (_kv_tile_source(i)[2], 0)),
            pl.BlockSpec((TM, HEAD_PAD), lambda i: (_kv_tile_source(i)[2], 0)),
            pl.BlockSpec((KV_LORA, HEADS * HEAD_PAD), const),
            pl.BlockSpec((KV_LORA, HEADS * HEAD_PAD), const),
            pl.BlockSpec((1, HEADS * HEAD_PAD), const),
        ],
        out_specs=[
            pl.BlockSpec((TM, HEADS * HEAD_PAD), lambda i: (i, 0)),
            pl.BlockSpec((TM, HEADS * HEAD_PAD), lambda i: (i, 0)),
        ],
        out_shape=[jax.ShapeDtypeStruct((rows, HEADS * HEAD_PAD), BF16)] * 2,
        compiler_params=_cparams(("parallel",)),
        name="kv_expand",
    )(cache_ckv, cache_kpe, ckv, kpe, w["wkv_b_k"], w["wkv_b_v"], w["v_one"])


def _sum_lane(h):
    return HEAD_PAD - 1 if h % 2 == 0 else 0


_V_ONE = np.zeros((1, HEADS * HEAD_PAD), np.float32)
_V_ONE[0, [HEAD_PAD * h + _sum_lane(h) for h in range(HEADS)]] = 1.0


def _zero_like_col(x):
    bits = pltpu.bitcast(x, jnp.uint32)
    zero = lax.shift_right_logical(lax.shift_right_logical(bits, jnp.uint32(16)), jnp.uint32(16))
    return pltpu.bitcast(zero, F32)[:, 0:1]


def _attn_kernel(t_k, q_ref, k_ref, v_ref, *rest):
    o_ref = rest[-1]
    scale2 = (NOPE + ROPE) ** -0.5 * LOG2_E
    kc = min(ATTN_KC, t_k)
    chunks = [slice(c * kc, (c + 1) * kc) for c in range(t_k // kc)]
    n_heads = q_ref.shape[1] // HEAD_PAD
    head = [slice(HEAD_PAD * e, HEAD_PAD * (e + 1)) for e in range(n_heads)]

    def scores(e, c):
        return _dot_nt(q_ref[:, head[e]], k_ref[c, head[e]])

    lane = lax.broadcasted_iota(jnp.int32, (TQ, HEAD_PAD), 1)

    def normalised(e, acc):
        sl = _sum_lane(e)
        return jnp.where(lane == sl, 0.0, acc) * (1.0 / acc[:, sl:sl + 1])

    def store(outs):
        for j in range(n_heads // 2):
            o_ref[:, 2 * VDIM * j:2 * VDIM * (j + 1)] = (outs[2 * j] + outs[2 * j + 1]).astype(BF16)

    if len(chunks) == 1:
        s_all = [scores(e, chunks[0]) for e in range(n_heads)]
        p_all = [jnp.exp2((s - jnp.max(s, axis=-1, keepdims=True)) * scale2).astype(BF16) for s in s_all]
        store([normalised(e, _dot(p_all[e], v_ref[:, head[e]])) for e in range(n_heads)])
        return

    s_cur = [scores(0, c) for c in chunks]
    outs = []
    for e in range(n_heads):
        m = functools.reduce(jnp.maximum, [jnp.max(s, axis=-1, keepdims=True) for s in s_cur])
        s_next, accs = [], []
        for i in range(len(chunks) + ATTN_SKEW):
            if e + 1 < n_heads and i < len(chunks):
                s_next.append(scores(e + 1, chunks[i]))
            if i >= ATTN_SKEW:
                j = i - ATTN_SKEW
                mj = m + _zero_like_col(accs[j - ATTN_GATE]) if j >= ATTN_GATE else m
                p = jnp.exp2((s_cur[j] - mj) * scale2).astype(BF16)
                pv = _dot(p, v_ref[chunks[j], head[e]])
                accs.append(pv if not accs else accs[-1] + pv)
        outs.append(normalised(e, accs[-1]))
        s_cur = s_next
    store(outs)


def _attn_call(q, k, v, o_buf, n_seq, t_q, t_k, q_row0, k_row0, name):
    nq = t_q // TQ
    q_blk0 = q_row0 // TQ
    k_blk0 = k_row0 // t_k
    heads = ATTN_HEADS if t_k > ATTN_KC else HEADS
    width = heads * HEAD_PAD
    in_specs = [
        pl.BlockSpec((TQ, width), lambda b, p, i: (q_blk0 + b * nq + i, p)),
        pl.BlockSpec((t_k, width), lambda b, p, i: (k_blk0 + b, p)),
        pl.BlockSpec((t_k, width), lambda b, p, i: (k_blk0 + b, p)),
    ]
    args = [q, k, v]
    if o_buf is not None:
        in_specs.append(pl.BlockSpec(memory_space=pl.ANY))
        args.append(o_buf)
    return pl.pallas_call(
        functools.partial(_attn_kernel, t_k),
        grid=(n_seq, HEADS // heads, nq),
        in_specs=in_specs,
        out_specs=pl.BlockSpec((TQ, heads * VDIM), lambda b, p, i: (q_blk0 + b * nq + i, p)),
        out_shape=jax.ShapeDtypeStruct((N_TOK, HEADS * VDIM), BF16),
        input_output_aliases={} if o_buf is None else {3: 0},
        compiler_params=_cparams(("parallel", "parallel", "arbitrary")),
        name=name,
    )(*args)


def _rwkv_gate(yf_ref, yb_ref, bonus_ref, g_ref, lg_ref, lb_ref, ones_ref):
    ones = ones_ref[...]
    y = yf_ref[...] + yb_ref[...]
    mu = _seg_sum(y, ones) * (1.0 / RWKV_N)
    yc = y - mu
    var = _seg_sum(yc * yc, ones) * (1.0 / RWKV_N)
    yn = yc * lax.rsqrt(var + RWKV_GN_EPS) * lg_ref[...] + lb_ref[...]
    return ((yn + bonus_ref[...]) * g_ref[...]).astype(BF16)


def _post_kernel(n_y, n_mix, has_proj, split_out, *refs):
    mix_refs = refs[n_y:n_y + n_mix]
    m_ref, lng_ref, lnb_ref = refs[n_y + n_mix:n_y + n_mix + 3]
    rest = refs[n_y + n_mix + 3:]
    if has_proj:
        wo_ref, rest = rest[0], rest[1:]
    wg_ref, wu_ref, wd_ref = rest[:3]
    o_refs = rest[3:]
    a = mix_refs[0][...] if n_mix == 1 else _rwkv_gate(*mix_refs)
    mix = _dot(a, wo_ref[...]) if has_proj else a
    y1 = _layer_norm(ALPHA * _act_tile(refs[:n_y], TM) + m_ref[2:3, :] * mix, lng_ref[0:1, :], lnb_ref[0:1, :])
    h = (y1 * (1.0 + m_ref[4:5, :]) + m_ref[3:4, :]).astype(BF16)
    acc = None
    for j in range(D_FF // TFF):
        cols = slice(TFF * j, TFF * (j + 1))
        act = (_silu(_dot(h, wg_ref[:, cols])) * _dot(h, wu_ref[:, cols])).astype(BF16)
        part = _dot(act, wd_ref[cols, :])
        acc = part if acc is None else acc + part
    out = _layer_norm(ALPHA * y1 + m_ref[5:6, :] * acc, lng_ref[1:2, :], lnb_ref[1:2, :])
    if not split_out:
        o_refs[0][...] = out
    else:
        is_prompt = pl.program_id(0) < N_PROMPT // TM

        @pl.when(is_prompt)
        def _():
            o_refs[0][...] = out

        @pl.when(jnp.logical_not(is_prompt))
        def _():
            o_refs[1][...] = out


def _post_call(y, a, mods, ln_g, ln_b, wo, wg, wu, wd, split_out=False):
    has_proj = wo is not None
    const = lambda i: (0, 0)
    resident = pl.Buffered(1)
    y_specs, y_args = _act_specs(y, TM)
    row = pl.BlockSpec((TM, D), lambda i: (i, 0))
    if isinstance(a, tuple):
        mix_specs = [row] * 4 + [pl.BlockSpec((1, D), const)] * 2 + [pl.BlockSpec((256, 256), const)]
        mix_args = list(a)
    else:
        mix_specs, mix_args = [row], [a]
    in_specs = y_specs + mix_specs + [
        pl.BlockSpec((None, SUBLANES, D), lambda i: (_cond_of_tile(i, TM), 0, 0)),
        pl.BlockSpec((2, D), const),
        pl.BlockSpec((2, D), const),
    ]
    args = y_args + mix_args + [mods, ln_g, ln_b]
    if has_proj:
        in_specs.append(pl.BlockSpec((D, D), const, pipeline_mode=resident))
        args.append(wo)
    in_specs += [
        pl.BlockSpec((D, D_FF), const, pipeline_mode=resident),
        pl.BlockSpec((D, D_FF), const, pipeline_mode=resident),
        pl.BlockSpec((D_FF, D), const, pipeline_mode=resident),
    ]
    args += [wg, wu, wd]
    if split_out:
        n_p = N_PROMPT // TM
        out_specs = [pl.BlockSpec((TM, D), lambda i: (jnp.minimum(i, n_p - 1), 0)),
                     pl.BlockSpec((TM, D), lambda i: (jnp.maximum(i - n_p, 0), 0))]
        out_shape = [jax.ShapeDtypeStruct((N_PROMPT, D), F32), jax.ShapeDtypeStruct((N_SAMPLE, D), F32)]
    else:
        out_specs = pl.BlockSpec((TM, D), lambda i: (i, 0))
        out_shape = jax.ShapeDtypeStruct((N_TOK, D), F32)
    return pl.pallas_call(
        functools.partial(_post_kernel, len(y_args), len(mix_args), has_proj, split_out),
        grid=(N_TOK // TM,),
        in_specs=in_specs,
        out_specs=out_specs,
        out_shape=out_shape,
        compiler_params=_cparams(("arbitrary",) if split_out else ("parallel",)),
        name="post_proj" if has_proj else "post_noproj",
    )(*args)


def _mix_tile_flags(i):
    first_sample = N_PROMPT // TMIX
    tiles_prompt = SEQ // TMIX
    tiles_sample = DEC_SEQ // TMIX
    is_p = i < first_sample
    j = jnp.where(is_p, i % tiles_prompt, (i - first_sample) % tiles_sample)
    n = jnp.where(is_p, tiles_prompt, tiles_sample)
    return j > 0, j < n - 1, j * TMIX, n * TMIX


def _halo_specs():
    blocks_per_tile = TMIX // HALO
    last = N_TOK // HALO - 1
    return [
        pl.BlockSpec((HALO, D), lambda i: (jnp.maximum(i * blocks_per_tile - 1, 0), 0)),
        pl.BlockSpec((TMIX, D), lambda i: (i, 0)),
        pl.BlockSpec((HALO, D), lambda i: (jnp.minimum((i + 1) * blocks_per_tile, last), 0)),
    ]


def _fill_ext(ext_ref, yp_ref, y_ref, yn_ref, m_ref, has_prev, has_next):
    scale1 = 1.0 + m_ref[1:2, :]
    shift = m_ref[0:1, :]
    h = y_ref[...] * scale1 + shift
    ext_ref[0:HALO, :] = jnp.where(has_prev, yp_ref[...] * scale1 + shift, 0.0)
    ext_ref[HALO:HALO + TMIX, :] = h
    ext_ref[HALO + TMIX:, :] = jnp.where(has_next, yn_ref[...] * scale1 + shift, 0.0)
    return h


def _rwkv_pre_kernel(yp_ref, y_ref, yn_ref, m_ref, mu_ref, wr_ref, wk_ref, wv_ref, g1_ref, g2_ref,
                     w1_ref, w2_ref, a1_ref, a2_ref, w0_ref, a0_ref, kk_ref, ka_ref, rk_ref, ones_ref,
                     r_out, v_out, g_out, kk_out, kd_out, b_out, ld_out, bonus_out, ext_ref):
    i = pl.program_id(0)
    has_prev, has_next, _, _ = _mix_tile_flags(i)
    h = _fill_ext(ext_ref, yp_ref, y_ref, yn_ref, m_ref, has_prev, has_next)
    prev = ext_ref[HALO - 1:HALO - 1 + TMIX, :]
    nxt = ext_ref[HALO + 1:HALO + 1 + TMIX, :]
    xx = 0.5 * (prev + nxt) - h

    def mix(j):
        return (h + xx * mu_ref[j:j + 1, :]).astype(BF16)

    r = _dot(mix(0), wr_ref[...])
    r_out[...] = r
    k = _dot(mix(2), wk_ref[...])
    v = _dot(mix(3), wv_ref[...])
    v_out[...] = v
    g_out[...] = _dot(_sigmoid(_dot(mix(5), g1_ref[...])).astype(BF16), g2_ref[...])
    tw = jnp.tanh(_dot(mix(1), w1_ref[...])).astype(BF16)
    ta = _dot(mix(4), a1_ref[...]).astype(BF16)

    kkf = k * kk_ref[...]
    kk = kkf * jnp.minimum(lax.rsqrt(_seg_sum(kkf * kkf, ones_ref[...])), 1e12)
    kk_out[...] = kk
    kd_sum = None
    for d in range(2):
        ld_out[d] = -EXP_M_HALF * _sigmoid(w0_ref[d:d + 1, :] + _dot(tw, w2_ref[d]))
        a = _sigmoid(a0_ref[d:d + 1, :] + _dot(ta, a2_ref[d]))
        kd = k * (1.0 + (a - 1.0) * ka_ref[...])
        kd_out[d] = kd
        b_out[d] = kk * a
        kd_sum = kd if kd_sum is None else kd_sum + kd
    bonus_out[...] = _seg_sum(r * kd_sum * rk_ref[...], ones_ref[...]) * v


def _rwkv_pre_call(y, mods, w):
    const2 = lambda i: (0, 0)
    const3 = lambda i: (0, 0, 0)
    row = pl.BlockSpec((TMIX, D), lambda i: (i, 0))
    row2 = pl.BlockSpec((2, TMIX, D), lambda i: (0, i, 0))
    one = jax.ShapeDtypeStruct((N_TOK, D), F32)
    two = jax.ShapeDtypeStruct((2, N_TOK, D), F32)
    return pl.pallas_call(
        _rwkv_pre_kernel,
        grid=(N_TOK // TMIX,),
        in_specs=_halo_specs() + [
            pl.BlockSpec((None, SUBLANES, D), lambda i: (_cond_of_tile(i, TMIX), 0, 0)),
            pl.BlockSpec((6, D), const2),
            pl.BlockSpec((D, D), const2), pl.BlockSpec((D, D), const2), pl.BlockSpec((D, D), const2),
            pl.BlockSpec((D, 128), const2), pl.BlockSpec((128, D), const2),
            pl.BlockSpec((D, 128), const2), pl.BlockSpec((2, 128, D), const3),
            pl.BlockSpec((D, 128), const2), pl.BlockSpec((2, 128, D), const3),
            pl.BlockSpec((2, D), const2), pl.BlockSpec((2, D), const2),
            pl.BlockSpec((1, D), const2), pl.BlockSpec((1, D), const2), pl.BlockSpec((1, D), const2),
            pl.BlockSpec((256, 256), const2),
        ],
        out_specs=[row, row, row, row, row2, row2, row2, row],
        out_shape=[one, one, one, one, two, two, two, one],
        scratch_shapes=[pltpu.VMEM((TMIX + 2 * HALO, D), F32)],
        compiler_params=_cparams(("parallel",)),
        name="rwkv_pre",
    )(y, y, y, mods, w["mu"], w["wr"], w["wk"], w["wv"], w["g1"], w["g2"], w["w1"], w["w2"],
      w["a1"], w["a2"], w["w0"], w["a0"], w["k_k"], w["k_a"], w["r_k"], w["ones_bd"])


def _scan_kernel(n_chunks, has_s0, aliased, *refs):
    ins = [refs[0:6], refs[6:12]]
    pos = 12
    s0_ref = refs[pos] if has_s0 else None
    pos += (1 if has_s0 else 0) + (2 if aliased else 0)
    y_refs = refs[pos:pos + 2]
    sf_ref, s_ref = refs[pos + 2], refs[pos + 3]
    n = pl.program_id(1)
    C = CHUNK
    P2 = 2 * C
    units = [(d, p) for d in range(2) for p in range(HEADS // 2)]
    lanes = [slice(2 * RWKV_N * p, 2 * RWKV_N * (p + 1)) for p in range(HEADS // 2)]

    @pl.when(n == 0)
    def _():
        s_ref[...] = jnp.zeros_like(s_ref)
        if has_s0:
            for d in range(2):
                for hd in range(HEADS):
                    o = RWKV_N * (hd % 2)
                    s_ref[d, hd // 2, o:o + RWKV_N, o:o + RWKV_N] = s0_ref[d, hd]

    lane = lax.broadcasted_iota(jnp.int32, (C, 2 * RWKV_N), 1)
    even = lane < RWKV_N
    ti = lax.broadcasted_iota(jnp.int32, (C, C), 0)
    si = lax.broadcasted_iota(jnp.int32, (C, C), 1)
    r2 = lax.broadcasted_iota(jnp.int32, (P2, P2), 0)
    c2 = lax.broadcasted_iota(jnp.int32, (P2, P2), 1)
    same = (r2 // C) == (c2 // C)
    eye = (r2 == c2).astype(F32)

    def same_block(size):
        return (r2 // size) == (c2 // size)

    leaf = same_block(2)
    merges = [same_block(2 * size) & jnp.logical_not(same_block(size)) for size in (2, 4, 8, 16, 32)]

    def split(x):
        return jnp.concatenate([jnp.where(even, x, 0.0), jnp.where(even, 0.0, x)], axis=0)

    def twice(x):
        return jnp.concatenate([x, x], axis=0)

    m_incl, m_strict, at, rt, bt, kt, bh, kh, pc, v = ([None, None] for _ in range(10))
    for d in range(2):
        r_ref, v_ref, kk_ref, kd_ref, b_ref, ld_ref = ins[d]
        sign = 1 - 2 * d
        cum = (sign * (si - ti) <= 0).astype(BF16)
        order = sign * (c2 % C - r2 % C)
        m_incl[d] = same & (order <= 0)
        m_strict[d] = same & (order < 0)
        ld = ld_ref[...]
        ld_hi, ld_mid, ld_lo = _split3(ld)
        cs = _dot(cum, ld_hi) + _dot(cum, ld_mid) + _dot(cum, ld_lo)
        last = cs[0:1, :] if d == 1 else cs[C - 1:C, :]
        pin = jnp.exp(cs)
        pinv = jnp.exp(-cs)
        at[d] = -kk_ref[...] * jnp.exp(cs - ld)
        rt[d] = r_ref[...] * pin
        bt[d] = b_ref[...] * pinv
        kt[d] = kd_ref[...] * pinv
        tail = jnp.exp(last - cs)
        bh[d] = b_ref[...] * tail
        kh[d] = kd_ref[...] * tail
        pc[d] = jnp.exp(last)
        v[d] = v_ref[...]

    s0 = [s_ref[d, p] for d, p in units]
    lhs = [jnp.concatenate([split(at[d][:, lanes[p]]), split(rt[d][:, lanes[p]])], axis=0).astype(BF16)
           for d, p in units]
    rhs = [jnp.concatenate([twice(bt[d][:, lanes[p]]), twice(kt[d][:, lanes[p]])], axis=0).astype(BF16)
           for d, p in units]
    v2b = [split(v[d][:, lanes[p]]).astype(BF16) for d, p in units]
    idx = range(len(units))
    g = [_dot_nt(lhs[u], rhs[u]) for u in idx]
    xs = [_dot_nt(lhs[u], s0[u].astype(BF16)) for u in idx]
    a_ab = [jnp.where(m_strict[units[u][0]], g[u][:P2, :P2], 0.0) for u in idx]
    inv = [eye + jnp.where(leaf, a_ab[u], 0.0) for u in idx]
    for mk in merges:
        invb = [inv[u].astype(BF16) for u in idx]
        binv = [_dot(jnp.where(mk, a_ab[u], 0.0).astype(BF16), invb[u]).astype(BF16) for u in idx]
        inv = [inv[u] + _dot(invb[u], binv[u]) for u in idx]
    a_ak = [jnp.where(m_strict[units[u][0]], g[u][:P2, P2:], 0.0).astype(BF16) for u in idx]
    rhs_u = [(xs[u][:P2] + _dot(a_ak[u], v2b[u])).astype(BF16) for u in idx]
    u2b = [_dot(inv[u].astype(BF16), rhs_u[u]).astype(BF16) for u in idx]
    m_rb = [jnp.where(m_incl[units[u][0]], g[u][P2:, :P2], 0.0).astype(BF16) for u in idx]
    m_rk = [jnp.where(m_incl[units[u][0]], g[u][P2:, P2:], 0.0).astype(BF16) for u in idx]
    for u, (d, p) in enumerate(units):
        y2 = xs[u][P2:] + _dot(m_rb[u], u2b[u]) + _dot(m_rk[u], v2b[u])
        y_refs[d][:, lanes[p]] = y2[:C] + y2[C:]
    for u, (d, p) in enumerate(units):
        bh2 = split(bh[d][:, lanes[p]]).astype(BF16)
        kh2 = split(kh[d][:, lanes[p]]).astype(BF16)
        s_ref[d, p] = s0[u] * pc[d][:, lanes[p]] + _dot_tn(u2b[u], bh2) + _dot_tn(v2b[u], kh2)

    @pl.when(n == n_chunks - 1)
    def _():
        for d in range(2):
            for hd in range(HEADS):
                o = RWKV_N * (hd % 2)
                sf_ref[d, hd] = s_ref[d, hd // 2, o:o + RWKV_N, o:o + RWKV_N]


def _scan_call(r, v, kk, kd, b, ld, s0, y_bufs, n_seq, t_seq, row0, name):
    n_chunks = t_seq // CHUNK
    blk0 = row0 // CHUNK
    chunk = [lambda bb, nn: blk0 + bb * n_chunks + nn,
             lambda bb, nn: blk0 + bb * n_chunks + n_chunks - 1 - nn]
    in_specs, args = [], []
    for d in range(2):
        row = pl.BlockSpec((CHUNK, D), lambda bb, nn, d=d: (chunk[d](bb, nn), 0))
        row2 = pl.BlockSpec((None, CHUNK, D), lambda bb, nn, d=d: (d, chunk[d](bb, nn), 0))
        in_specs += [row, row, row, row2, row2, row2]
        args += [r, v, kk, kd, b, ld]
    st = pl.BlockSpec((None, 2, HEADS, RWKV_N, RWKV_N), lambda bb, nn: (bb, 0, 0, 0, 0))
    if s0 is not None:
        in_specs.append(st)
        args.append(s0)
    aliases = {}
    if y_bufs is not None:
        aliases = {len(args): 0, len(args) + 1: 1}
        in_specs += [pl.BlockSpec(memory_space=pl.ANY)] * 2
        args += list(y_bufs)
    out = pl.pallas_call(
        functools.partial(_scan_kernel, n_chunks, s0 is not None, y_bufs is not None),
        grid=(n_seq, n_chunks),
        in_specs=in_specs,
        out_specs=[
            pl.BlockSpec((CHUNK, D), lambda bb, nn: (chunk[0](bb, nn), 0)),
            pl.BlockSpec((CHUNK, D), lambda bb, nn: (chunk[1](bb, nn), 0)),
            st,
        ],
        out_shape=[
            jax.ShapeDtypeStruct((N_TOK, D), F32),
            jax.ShapeDtypeStruct((N_TOK, D), F32),
            jax.ShapeDtypeStruct((n_seq, 2, HEADS, RWKV_N, RWKV_N), F32),
        ],
        scratch_shapes=[pltpu.VMEM((2, HEADS // 2, 2 * RWKV_N, 2 * RWKV_N), F32)],
        input_output_aliases=aliases,
        compiler_params=_cparams(("parallel", "arbitrary")),
        name=name,
    )(*args)
    return (out[0], out[1]), out[2]


def _pool_kernel(yp_ref, y_ref, yn_ref, m_ref, w_ref, sc_ref, o_ref, ext_ref):
    i = pl.program_id(0)
    has_prev, has_next, pos0, seq_len = _mix_tile_flags(i)
    _fill_ext(ext_ref, yp_ref, y_ref, yn_ref, m_ref, has_prev, has_next)
    t = pos0 + lax.broadcasted_iota(jnp.int32, (TMIX, 1), 0)
    for gi, win in enumerate(POOL_WINDOWS):
        cols = slice(POOL_C * gi, POOL_C * (gi + 1))
        acc = None
        for j in range(-(win // 2), win - win // 2):
            x = ext_ref[HALO + j:HALO + j + TMIX, cols]
            acc = x if acc is None else acc + x
        lo = jnp.maximum(t - win // 2, 0)
        hi = jnp.minimum(t - win // 2 + win, seq_len)
        cnt = (hi - lo).astype(F32)
        pooled = (acc / cnt - ext_ref[HALO:HALO + TMIX, cols]).astype(BF16)
        o_ref[:, cols] = _dot(pooled, w_ref[gi]) * sc_ref[:, cols]


def _pool_call(y, mods, pool_w, pool_scale):
    return pl.pallas_call(
        _pool_kernel,
        grid=(N_TOK // TMIX,),
        in_specs=_halo_specs() + [
            pl.BlockSpec((None, SUBLANES, D), lambda i: (_cond_of_tile(i, TMIX), 0, 0)),
            pl.BlockSpec((4, POOL_C, POOL_C), lambda i: (0, 0, 0)),
            pl.BlockSpec((1, D), lambda i: (0, 0)),
        ],
        out_specs=pl.BlockSpec((TMIX, D), lambda i: (i, 0)),
        out_shape=jax.ShapeDtypeStruct((N_TOK, D), F32),
        scratch_shapes=[pltpu.VMEM((TMIX + 2 * HALO, D), F32)],
        compiler_params=_cparams(("parallel",)),
        name="pool",
    )(y, y, y, mods, pool_w, pool_scale)


def _rope_tables():
    rows = DEC_SEQ // GRID_W
    row = jnp.repeat(jnp.arange(rows, dtype=F32), GRID_W)
    col = jnp.tile(jnp.arange(GRID_W, dtype=F32), rows)
    n_freq = ROPE // 4
    inv_freq = ROPE_THETA ** (-jnp.arange(n_freq, dtype=F32) / n_freq)
    ang = jnp.stack([row[:, None] * inv_freq, col[:, None] * inv_freq], axis=1)
    cos, sin = jnp.cos(ang), jnp.sin(ang)
    cos32 = jnp.concatenate([cos, cos], axis=-1).reshape(DEC_SEQ, ROPE)
    sin32 = jnp.concatenate([-sin, sin], axis=-1).reshape(DEC_SEQ, ROPE)
    ones = jnp.ones((DEC_SEQ, NOPE), F32)
    zeros = jnp.zeros((DEC_SEQ, NOPE), F32)
    tail = HEAD_PAD - NOPE - ROPE
    cos_t = jnp.concatenate([ones, cos32, jnp.zeros((DEC_SEQ, tail), F32)], axis=1)
    sin_t = jnp.concatenate([zeros, sin32, jnp.zeros((DEC_SEQ, tail), F32)], axis=1)
    return cos_t, sin_t


def _swap_halves(w):
    s = w.reshape(*w.shape[:-1], 2, 2, ROPE // 4)
    return s[..., ::-1, :].reshape(w.shape)


def _pad_heads(w):
    k, _, width = w.shape
    return jnp.pad(w, ((0, 0), (0, 0), (0, HEAD_PAD - width))).reshape(k, HEADS * HEAD_PAD)


def _pad_heads_alternating(w):
    k = w.shape[0]
    pair = w.reshape(k, HEADS // 2, 2, VDIM)
    z = jnp.zeros((k, HEADS // 2, VDIM), w.dtype)
    return jnp.concatenate([pair[:, :, 0], z, z, pair[:, :, 1]], axis=-1).reshape(k, HEADS * HEAD_PAD)


def _mla_weights(wq_a, q_norm, wq_b, wkv_a, kv_norm, wkv_b, wo, cos_t, sin_t):
    wq_b3 = wq_b.reshape(Q_LORA, HEADS, NOPE + ROPE)
    wq_b4 = jnp.concatenate([wq_b3, _swap_halves(wq_b3[..., NOPE:])], axis=-1)
    w_pe = wkv_a[:, KV_LORA:]
    w_pe4 = jnp.concatenate([jnp.zeros((D, NOPE), F32), w_pe, _swap_halves(w_pe)], axis=1)
    wkv_b3 = wkv_b.reshape(KV_LORA, HEADS, NOPE + VDIM)
    return {
        "w_a": jnp.concatenate([wq_a, wkv_a[:, :KV_LORA], w_pe4], axis=1).astype(BF16),
        "q_norm": q_norm.reshape(1, Q_LORA),
        "wq_b": _pad_heads(wq_b4).astype(BF16),
        "kv_norm": kv_norm.reshape(1, KV_LORA),
        "wkv_b_k": _pad_heads(wkv_b3[..., :NOPE]).astype(BF16),
        "wkv_b_v": _pad_heads_alternating(wkv_b3[..., NOPE:]).astype(BF16),
        "v_one": jnp.asarray(_V_ONE),
        "wo": wo.astype(BF16),
        "cos": cos_t,
        "sin": sin_t,
    }


def _dir_pad(w):
    z = jnp.zeros_like(w[0])
    return jnp.stack([jnp.concatenate([w[0], z], axis=0), jnp.concatenate([z, w[1]], axis=0)])


def _rwkv_weights(mu, wr, wk, wv, w0, w1, w2, a0, a1, a2, g1, g2, k_k, k_a, r_k, lnx_g, lnx_b, wo):
    blk = np.arange(256) // RWKV_N
    return {
        "mu": mu,
        "wr": wr.astype(BF16), "wk": wk.astype(BF16), "wv": wv.astype(BF16),
        "g1": g1.astype(BF16), "g2": g2.astype(BF16),
        "w1": jnp.concatenate([w1[0], w1[1]], axis=1).astype(BF16),
        "w2": _dir_pad(w2).astype(BF16),
        "a1": jnp.concatenate([a1[0], a1[1]], axis=1).astype(BF16),
        "a2": _dir_pad(a2).astype(BF16),
        "w0": w0, "a0": a0,
        "k_k": k_k.reshape(1, D), "k_a": k_a.reshape(1, D),
        "r_k": r_k.reshape(1, D), "lnx_g": lnx_g.reshape(1, D), "lnx_b": lnx_b.reshape(1, D),
        "wo": wo.astype(BF16),
        "ones_bd": jnp.asarray(blk[:, None] == blk[None, :], BF16),
    }


def _mla_layer(y, mods, w, cache_ckv, cache_kpe):
    q, ckv, kpe = _mla_pre_call(y, mods, w)
    cache_kpe_pad = jnp.pad(cache_kpe, ((0, 0), (0, 0), (NOPE, HEAD_PAD - NOPE - ROPE)))
    t_k = PAST + DEC_SEQ
    k_all, v_all = _kv_expand_call(cache_ckv.reshape(DEC_BATCH * PAST, KV_LORA),
                                   cache_kpe_pad.reshape(DEC_BATCH * PAST, HEAD_PAD), ckv, kpe, w)
    o = _attn_call(q, k_all, v_all, None, BATCH, SEQ, SEQ, 0, DEC_BATCH * t_k, "attn_prompt")
    o = _attn_call(q, k_all, v_all, o, DEC_BATCH, DEC_SEQ, t_k, N_PROMPT, 0, "attn_sample")
    new_ckv = ckv[:N_PROMPT].reshape(BATCH, SEQ, KV_LORA)
    new_kpe = kpe[:N_PROMPT, NOPE:NOPE + ROPE].reshape(BATCH, SEQ, ROPE)
    return o, new_ckv, new_kpe


def _rwkv_layer(y, mods, w, state):
    r, v, g, kk, kd, b, ld, bonus = _rwkv_pre_call(y, mods, w)
    y_fb, s_p = _scan_call(r, v, kk, kd, b, ld, None, None, BATCH, SEQ, 0, "scan_prompt")
    y_fb, _ = _scan_call(r, v, kk, kd, b, ld, state, y_fb, DEC_BATCH, DEC_SEQ, N_PROMPT, "scan_sample")
    return (y_fb[0], y_fb[1], bonus, g, w["lnx_g"], w["lnx_b"], w["ones_bd"]), s_p


def kernel(x_prompt, x_sample, cache_ckv, cache_kpe, state_wkv, c, c_ctx, ada_w, ada_b, ln_g, ln_b, ffn_wg, ffn_wu, ffn_wd, mla_wq_a, mla_q_norm, mla_wq_b, mla_wkv_a, mla_kv_norm, mla_wkv_b, mla_wo, rwkv_mu, rwkv_wr, rwkv_wk, rwkv_wv, rwkv_w0, rwkv_w1, rwkv_w2, rwkv_a0, rwkv_a1, rwkv_a2, rwkv_g1, rwkv_g2, rwkv_k_k, rwkv_k_a, rwkv_r_k, rwkv_lnx_g, rwkv_lnx_b, rwkv_wo, pool_w, pool_scale):
    y = (x_prompt.reshape(N_PROMPT, D), x_sample.reshape(N_SAMPLE, D))
    cond8 = jnp.concatenate([c_ctx[None, :], c, jnp.zeros((SUBLANES - 1 - DEC_BATCH, D), F32)], axis=0)
    mods_all = _ada_call(cond8, ada_w, ada_b)
    mods_all = jnp.pad(mods_all.reshape(DEPTH, SUBLANES, 6, D)[:, :1 + DEC_BATCH],
                       ((0, 0), (0, 0), (0, SUBLANES - 6), (0, 0)))
    cos_t, sin_t = _rope_tables()
    new_ckv, new_kpe, new_wkv = [], [], []
    for layer in range(DEPTH):
        kind, j = layer % 3, layer // 3
        mods = mods_all[layer]
        wo = None
        if kind == 0:
            w = _mla_weights(mla_wq_a[j], mla_q_norm[j], mla_wq_b[j], mla_wkv_a[j], mla_kv_norm[j],
                             mla_wkv_b[j], mla_wo[j], cos_t, sin_t)
            a, ckv, kpe = _mla_layer(y, mods, w, cache_ckv[:, j], cache_kpe[:, j])
            new_ckv.append(ckv)
            new_kpe.append(kpe)
            wo = w["wo"]
        elif kind == 1:
            w = _rwkv_weights(rwkv_mu[j], rwkv_wr[j], rwkv_wk[j], rwkv_wv[j], rwkv_w0[j], rwkv_w1[j],
                              rwkv_w2[j], rwkv_a0[j], rwkv_a1[j], rwkv_a2[j], rwkv_g1[j], rwkv_g2[j],
                              rwkv_k_k[j], rwkv_k_a[j], rwkv_r_k[j], rwkv_lnx_g[j], rwkv_lnx_b[j], rwkv_wo[j])
            a, s_new = _rwkv_layer(y, mods, w, state_wkv[:, j])
            new_wkv.append(s_new)
            wo = w["wo"]
        else:
            a = _pool_call(y, mods, pool_w[j].astype(BF16), pool_scale[j].reshape(1, D))
        y = _post_call(y, a, mods, ln_g[layer], ln_b[layer], wo,
                       ffn_wg[layer].astype(BF16), ffn_wu[layer].astype(BF16), ffn_wd[layer].astype(BF16),
                       split_out=layer == DEPTH - 1)
    return (y[0].reshape(BATCH, SEQ, D), y[1].reshape(DEC_BATCH, DEC_SEQ, D),
            jnp.stack(new_ckv, axis=1), jnp.stack(new_kpe, axis=1), jnp.stack(new_wkv, axis=1))
```

```python
import functools

import jax
import jax.numpy as jnp
import numpy as np
from jax import lax
from jax.experimental import pallas as pl
from jax.experimental.pallas import tpu as pltpu

F32 = jnp.float32
BF16 = jnp.bfloat16

D = 1024
BATCH, SEQ = 32, 256
DEC_BATCH, DEC_SEQ = 2, 4096
PAST = 512
DEPTH = 4
GRID_W = 64
HEADS = 16
Q_LORA, KV_LORA = 384, 256
NOPE, ROPE, VDIM = 64, 32, 64
ROPE_THETA = 10000.0
RWKV_N = 64
RWKV_GN_EPS = 64e-5
POOL_WINDOWS = (2, 4, 8, 16)
POOL_C = D // 4
D_FF = 2816
ALPHA = (2 * DEPTH) ** 0.25
LN_EPS = 1e-5
RMS_EPS = 1e-6
LOG2_E = 1.4426950408889634
EXP_M_HALF = 0.6065306597126334

N_PROMPT = BATCH * SEQ
N_SAMPLE = DEC_BATCH * DEC_SEQ
N_TOK = N_PROMPT + N_SAMPLE

LANES = 128
SUBLANES = 8
HEAD_PAD = 128
VMEM_LIMIT = 56 * 1024 * 1024

TM = 512
TMIX = 256
HALO = SUBLANES
TQ = 256
ATTN_HEADS = 4
ATTN_KC = 512
ATTN_SKEW = 2
ATTN_GATE = 3
CHUNK = 64
TFF = 256
POST_TM = 1024
POST_SUB = 512
POST_PRO_AT = 5
POST_EPI_AT = 2


def _cparams(sem):
    return pltpu.CompilerParams(dimension_semantics=sem, vmem_limit_bytes=VMEM_LIMIT)


def _dot(a, b):
    return jnp.dot(a, b, preferred_element_type=F32)


def _dot_nt(a, b):
    return lax.dot_general(a, b, (((1,), (1,)), ((), ())), preferred_element_type=F32)


def _dot_tn(a, b):
    return lax.dot_general(a, b, (((0,), (0,)), ((), ())), preferred_element_type=F32)


def _layer_norm(x, g, b):
    mu = jnp.mean(x, axis=-1, keepdims=True)
    xc = x - mu
    var = jnp.mean(xc * xc, axis=-1, keepdims=True)
    return xc * lax.rsqrt(var + LN_EPS) * g + b


def _rms_norm(x, g):
    return x * lax.rsqrt(jnp.mean(x * x, axis=-1, keepdims=True) + RMS_EPS) * g


def _sigmoid(x):
    return 0.5 * jnp.tanh(0.5 * x) + 0.5


def _silu(x):
    return x * _sigmoid(x)


def _cond_of_tile(i, tile):
    first_sample = N_PROMPT // tile
    per_seq = DEC_SEQ // tile
    return jnp.where(i < first_sample, 0, 1 + (i - first_sample) // per_seq)


def _split3(x):
    hi = x.astype(BF16)
    r1 = x - hi.astype(F32)
    mid = r1.astype(BF16)
    lo = (r1 - mid.astype(F32)).astype(BF16)
    return hi, mid, lo


def _seg_sum(x, ones_bd):
    outs = []
    for g in range(D // 256):
        hi, mid, lo = _split3(x[:, 256 * g:256 * (g + 1)])
        outs.append(_dot(hi, ones_bd) + _dot(mid, ones_bd) + _dot(lo, ones_bd))
    return jnp.concatenate(outs, axis=1)


def _ada_kernel(c_ref, w_ref, b_ref, o_ref):
    a = _silu(c_ref[...]).astype(BF16)
    o_ref[...] = _dot(a, w_ref[...].astype(BF16)) + b_ref[...]


def _ada_call(cond8, ada_w, ada_b):
    tn = 1536
    return pl.pallas_call(
        _ada_kernel,
        grid=(DEPTH, 6 * D // tn),
        in_specs=[
            pl.BlockSpec((SUBLANES, D), lambda l, n: (0, 0)),
            pl.BlockSpec((None, D, tn), lambda l, n: (l, 0, n)),
            pl.BlockSpec((None, 1, tn), lambda l, n: (l, 0, n)),
        ],
        out_specs=pl.BlockSpec((None, SUBLANES, tn), lambda l, n: (l, 0, n)),
        out_shape=jax.ShapeDtypeStruct((DEPTH, SUBLANES, 6 * D), F32),
        compiler_params=_cparams(("parallel", "parallel")),
        name="adaln",
    )(cond8, ada_w, ada_b.reshape(DEPTH, 1, 6 * D))


def _act_specs(y, tile):
    if not isinstance(y, tuple):
        return [pl.BlockSpec((tile, D), lambda i: (i, 0))], [y]
    n_p = N_PROMPT // tile
    return [pl.BlockSpec((tile, D), lambda i: (jnp.minimum(i, n_p - 1), 0)),
            pl.BlockSpec((tile, D), lambda i: (jnp.maximum(i - n_p, 0), 0))], list(y)


def _act_tile(y_refs, tile):
    if len(y_refs) == 1:
        return y_refs[0][...]
    return jnp.where(pl.program_id(0) < N_PROMPT // tile, y_refs[0][...], y_refs[1][...])


def _mla_pre_kernel(n_y, *refs):
    m_ref, wa_ref, qn_ref, wqb_ref, kvn_ref, cos_ref, sin_ref, q_ref, ckv_ref, kpe_ref = refs[n_y:]
    i = pl.program_id(0)
    h = (_act_tile(refs[:n_y], TM) * (1.0 + m_ref[1:2, :]) + m_ref[0:1, :]).astype(BF16)
    x = _dot(h, wa_ref[...])
    qa = _rms_norm(x[:, :Q_LORA], qn_ref[...]).astype(BF16)
    ckv_ref[...] = _rms_norm(x[:, Q_LORA:Q_LORA + KV_LORA], kvn_ref[...])
    kpe = x[:, Q_LORA + KV_LORA:]
    q = _dot(qa, wqb_ref[...])
    is_latent = i >= N_PROMPT // TM
    swap_shift = HEAD_PAD - ROPE

    @pl.when(jnp.logical_not(is_latent))
    def _():
        q_ref[...] = q.astype(BF16)
        lane = lax.broadcasted_iota(jnp.int32, kpe.shape, 1)
        kpe_ref[...] = jnp.where(lane < NOPE + ROPE, kpe, 0.0)

    @pl.when(is_latent)
    def _():
        cos = cos_ref[...]
        sin = sin_ref[...]
        kpe_ref[...] = kpe * cos + pltpu.roll(kpe, swap_shift, 1) * sin
        for hd in range(HEADS):
            sl = slice(HEAD_PAD * hd, HEAD_PAD * (hd + 1))
            q_ref[:, sl] = (q[:, sl] * cos + pltpu.roll(q[:, sl], swap_shift, 1) * sin).astype(BF16)


def _mla_pre_call(y, mods, w):
    first_sample = N_PROMPT // TM
    per_seq = DEC_SEQ // TM
    const = lambda i: (0, 0)
    rope_map = lambda i: (jnp.where(i < first_sample, 0, (i - first_sample) % per_seq), 0)
    y_specs, y_args = _act_specs(y, TM)
    return pl.pallas_call(
        functools.partial(_mla_pre_kernel, len(y_args)),
        grid=(N_TOK // TM,),
        in_specs=y_specs + [
            pl.BlockSpec((None, SUBLANES, D), lambda i: (_cond_of_tile(i, TM), 0, 0)),
            pl.BlockSpec((D, Q_LORA + KV_LORA + HEAD_PAD), const),
            pl.BlockSpec((1, Q_LORA), const),
            pl.BlockSpec((Q_LORA, HEADS * HEAD_PAD), const),
            pl.BlockSpec((1, KV_LORA), const),
            pl.BlockSpec((TM, HEAD_PAD), rope_map),
            pl.BlockSpec((TM, HEAD_PAD), rope_map),
        ],
        out_specs=[
            pl.BlockSpec((TM, HEADS * HEAD_PAD), lambda i: (i, 0)),
            pl.BlockSpec((TM, KV_LORA), lambda i: (i, 0)),
            pl.BlockSpec((TM, HEAD_PAD), lambda i: (i, 0)),
        ],
        out_shape=[
            jax.ShapeDtypeStruct((N_TOK, HEADS * HEAD_PAD), BF16),
            jax.ShapeDtypeStruct((N_TOK, KV_LORA), F32),
            jax.ShapeDtypeStruct((N_TOK, HEAD_PAD), F32),
        ],
        compiler_params=_cparams(("parallel",)),
        name="mla_pre",
    )(*y_args, mods, w["w_a"], w["q_norm"], w["wq_b"], w["kv_norm"], w["cos"], w["sin"])


KV_TILES_PER_SEQ = (PAST + DEC_SEQ) // TM
KV_ROWS = DEC_BATCH * (PAST + DEC_SEQ) + N_PROMPT


def _kv_tile_source(i):
    n_lat = DEC_BATCH * KV_TILES_PER_SEQ
    b = jnp.minimum(i // KV_TILES_PER_SEQ, DEC_BATCH - 1)
    j = i - b * KV_TILES_PER_SEQ
    is_cache = (i < n_lat) & (j < PAST // TM)
    new_tile = N_PROMPT // TM + b * (DEC_SEQ // TM) + jnp.maximum(j - PAST // TM, 0)
    return is_cache, b * (PAST // TM) + jnp.minimum(j, PAST // TM - 1), jnp.where(i < n_lat, new_tile, i - n_lat)


def _kv_expand_kernel(cckv_ref, ckpe_ref, ckv_ref, kpe_ref, wk_ref, wv_ref, vone_ref, k_ref, v_ref):
    is_cache, _, _ = _kv_tile_source(pl.program_id(0))
    c = jnp.where(is_cache, cckv_ref[...], ckv_ref[...]).astype(BF16)
    kpe = jnp.where(is_cache, ckpe_ref[...], kpe_ref[...])
    kn = _dot(c, wk_ref[...])
    for hd in range(HEADS):
        sl = slice(HEAD_PAD * hd, HEAD_PAD * (hd + 1))
        k_ref[:, sl] = (kn[:, sl] + kpe).astype(BF16)
    v_ref[...] = (_dot(c, wv_ref[...]) + vone_ref[...]).astype(BF16)


def _kv_expand_call(cache_ckv, cache_kpe, ckv, kpe, w):
    rows = KV_ROWS
    const = lambda i: (0, 0)
    return pl.pallas_call(
        _kv_expand_kernel,
        grid=(rows // TM,),
        in_specs=[
            pl.BlockSpec((TM, KV_LORA), lambda i: (_kv_tile_source(i)[1], 0)),
            pl.BlockSpec((TM, HEAD_PAD), lambda i: (_kv_tile_source(i)[1], 0)),
            pl.BlockSpec((TM, KV_LORA), lambda i: (_kv_tile_source(i)[2], 0)),
            pl.BlockSpec((TM, HEAD_PAD), lambda i: (_kv_tile_source(i)[2], 0)),
            pl.BlockSpec((KV_LORA, HEADS * HEAD_PAD), const),
            pl.BlockSpec((KV_LORA, HEADS * HEAD_PAD), const),
            pl.BlockSpec((1, HEADS * HEAD_PAD), const),
        ],
        out_specs=[
            pl.BlockSpec((TM, HEADS * HEAD_PAD), lambda i: (i, 0)),
            pl.BlockSpec((TM, HEADS * HEAD_PAD), lambda i: (i, 0)),
        ],
        out_shape=[jax.ShapeDtypeStruct((rows, HEADS * HEAD_PAD), BF16)] * 2,
        compiler_params=_cparams(("parallel",)),
        name="kv_expand",
    )(cache_ckv, cache_kpe, ckv, kpe, w["wkv_b_k"], w["wkv_b_v"], w["v_one"])


def _sum_lane(h):
    return HEAD_PAD - 1 if h % 2 == 0 else 0


_V_ONE = np.zeros((1, HEADS * HEAD_PAD), np.float32)
_V_ONE[0, [HEAD_PAD * h + _sum_lane(h) for h in range(HEADS)]] = 1.0


def _zero_like_col(x):
    bits = pltpu.bitcast(x, jnp.uint32)
    zero = lax.shift_right_logical(lax.shift_right_logical(bits, jnp.uint32(16)), jnp.uint32(16))
    return pltpu.bitcast(zero, F32)[:, 0:1]


def _attn_kernel(t_k, q_ref, k_ref, v_ref, *rest):
    o_ref = rest[-1]
    scale2 = (NOPE + ROPE) ** -0.5 * LOG2_E
    kc = min(ATTN_KC, t_k)
    chunks = [slice(c * kc, (c + 1) * kc) for c in range(t_k // kc)]
    n_heads = q_ref.shape[1] // HEAD_PAD
    head = [slice(HEAD_PAD * e, HEAD_PAD * (e + 1)) for e in range(n_heads)]

    def scores(e, c):
        return _dot_nt(q_ref[:, head[e]], k_ref[c, head[e]])

    lane = lax.broadcasted_iota(jnp.int32, (TQ, HEAD_PAD), 1)

    def normalised(e, acc):
        sl = _sum_lane(e)
        return jnp.where(lane == sl, 0.0, acc) * (1.0 / acc[:, sl:sl + 1])

    def store(outs):
        for j in range(n_heads // 2):
            o_ref[:, 2 * VDIM * j:2 * VDIM * (j + 1)] = (outs[2 * j] + outs[2 * j + 1]).astype(BF16)

    if len(chunks) == 1:
        s_all = [scores(e, chunks[0]) for e in range(n_heads)]
        p_all = [jnp.exp2((s - jnp.max(s, axis=-1, keepdims=True)) * scale2).astype(BF16) for s in s_all]
        store([normalised(e, _dot(p_all[e], v_ref[:, head[e]])) for e in range(n_heads)])
        return

    s_cur = [scores(0, c) for c in chunks]
    outs = []
    for e in range(n_heads):
        m = functools.reduce(jnp.maximum, [jnp.max(s, axis=-1, keepdims=True) for s in s_cur])
        s_next, accs = [], []
        for i in range(len(chunks) + ATTN_SKEW):
            if e + 1 < n_heads and i < len(chunks):
                s_next.append(scores(e + 1, chunks[i]))
            if i >= ATTN_SKEW:
                j = i - ATTN_SKEW
                mj = m + _zero_like_col(accs[j - ATTN_GATE]) if j >= ATTN_GATE else m
                p = jnp.exp2((s_cur[j] - mj) * scale2).astype(BF16)
                pv = _dot(p, v_ref[chunks[j], head[e]])
                accs.append(pv if not accs else accs[-1] + pv)
        outs.append(normalised(e, accs[-1]))
        s_cur = s_next
    store(outs)


def _attn_call(q, k, v, o_buf, n_seq, t_q, t_k, q_row0, k_row0, name):
    nq = t_q // TQ
    q_blk0 = q_row0 // TQ
    k_blk0 = k_row0 // t_k
    heads = ATTN_HEADS if t_k > ATTN_KC else HEADS
    width = heads * HEAD_PAD
    in_specs = [
        pl.BlockSpec((TQ, width), lambda b, p, i: (q_blk0 + b * nq + i, p)),
        pl.BlockSpec((t_k, width), lambda b, p, i: (k_blk0 + b, p)),
        pl.BlockSpec((t_k, width), lambda b, p, i: (k_blk0 + b, p)),
    ]
    args = [q, k, v]
    if o_buf is not None:
        in_specs.append(pl.BlockSpec(memory_space=pl.ANY))
        args.append(o_buf)
    return pl.pallas_call(
        functools.partial(_attn_kernel, t_k),
        grid=(n_seq, HEADS // heads, nq),
        in_specs=in_specs,
        out_specs=pl.BlockSpec((TQ, heads * VDIM), lambda b, p, i: (q_blk0 + b * nq + i, p)),
        out_shape=jax.ShapeDtypeStruct((N_TOK, HEADS * VDIM), BF16),
        input_output_aliases={} if o_buf is None else {3: 0},
        compiler_params=_cparams(("parallel", "parallel", "arbitrary")),
        name=name,
    )(*args)


def _rwkv_gate(rows, yf_ref, yb_ref, bonus_ref, g_ref, lg_ref, lb_ref, ones_ref):
    ones = ones_ref[...]
    y = yf_ref[rows, :] + yb_ref[rows, :]
    mu = _seg_sum(y, ones) * (1.0 / RWKV_N)
    yc = y - mu
    var = _seg_sum(yc * yc, ones) * (1.0 / RWKV_N)
    yn = yc * lax.rsqrt(var + RWKV_GN_EPS) * lg_ref[...] + lb_ref[...]
    return ((yn + bonus_ref[rows, :]) * g_ref[rows, :]).astype(BF16)


def _post_kernel(tile, n_y, n_mix, has_proj, split_out, *refs):
    y_refs = refs[:n_y]
    mix_refs = refs[n_y:n_y + n_mix]
    m_ref, lng_ref, lnb_ref = refs[n_y + n_mix:n_y + n_mix + 3]
    rest = refs[n_y + n_mix + 3:]
    if has_proj:
        wo_ref, rest = rest[0], rest[1:]
    wg_ref, wu_ref, wd_ref = rest[:3]
    o_refs = rest[3:]
    is_prompt = pl.program_id(0) < N_PROMPT // tile
    blocks = [slice(POST_SUB * s, POST_SUB * (s + 1)) for s in range(tile // POST_SUB)]
    n_chunks = D_FF // TFF

    def prologue(rows):
        a = mix_refs[0][rows, :] if n_mix == 1 else _rwkv_gate(rows, *mix_refs)
        mix = _dot(a, wo_ref[...]) if has_proj else a
        y = y_refs[0][rows, :] if n_y == 1 else jnp.where(is_prompt, y_refs[0][rows, :], y_refs[1][rows, :])
        y1 = _layer_norm(ALPHA * y + m_ref[2:3, :] * mix, lng_ref[0:1, :], lnb_ref[0:1, :])
        return y1, (y1 * (1.0 + m_ref[4:5, :]) + m_ref[3:4, :]).astype(BF16)

    def ffn_chunk(h, j):
        cols = slice(TFF * j, TFF * (j + 1))
        act = (_silu(_dot(h, wg_ref[:, cols])) * _dot(h, wu_ref[:, cols])).astype(BF16)
        return _dot(act, wd_ref[cols, :])

    def epilogue(rows, y1, acc):
        out = _layer_norm(ALPHA * y1 + m_ref[5:6, :] * acc, lng_ref[1:2, :], lnb_ref[1:2, :])
        if not split_out:
            o_refs[0][rows, :] = out
        else:
            @pl.when(is_prompt)
            def _():
                o_refs[0][rows, :] = out

            @pl.when(jnp.logical_not(is_prompt))
            def _():
                o_refs[1][rows, :] = out

    state = [None] * len(blocks)
    state[0] = prologue(blocks[0])
    pending = None
    for s, rows in enumerate(blocks):
        y1, h = state[s]
        acc = None
        for j in range(n_chunks):
            part = ffn_chunk(h, j)
            acc = part if acc is None else acc + part
            if j == POST_PRO_AT and s + 1 < len(blocks):
                state[s + 1] = prologue(blocks[s + 1])
            if j == POST_EPI_AT and pending is not None:
                epilogue(*pending)
                pending = None
        pending = (rows, y1, acc)
    epilogue(*pending)


def _post_call(y, a, mods, ln_g, ln_b, wo, wg, wu, wd, layer, split_out=False):
    has_proj = wo is not None
    const = lambda i: (0, 0)
    this_layer = lambda i: (layer, 0, 0)
    resident = pl.Buffered(1)
    tile = POST_SUB if isinstance(a, tuple) else POST_TM
    y_specs, y_args = _act_specs(y, tile)
    row = pl.BlockSpec((tile, D), lambda i: (i, 0))
    if isinstance(a, tuple):
        mix_specs = [row] * 4 + [pl.BlockSpec((1, D), const)] * 2 + [pl.BlockSpec((256, 256), const)]
        mix_args = list(a)
    else:
        mix_specs, mix_args = [row], [a]
    in_specs = y_specs + mix_specs + [
        pl.BlockSpec((None, SUBLANES, D), lambda i: (_cond_of_tile(i, tile), 0, 0)),
        pl.BlockSpec((2, D), const),
        pl.BlockSpec((2, D), const),
    ]
    args = y_args + mix_args + [mods, ln_g, ln_b]
    if has_proj:
        in_specs.append(pl.BlockSpec((D, D), const, pipeline_mode=resident))
        args.append(wo)
    in_specs += [
        pl.BlockSpec((None, D, D_FF), this_layer, pipeline_mode=resident),
        pl.BlockSpec((None, D, D_FF), this_layer, pipeline_mode=resident),
        pl.BlockSpec((None, D_FF, D), this_layer, pipeline_mode=resident),
    ]
    args += [wg, wu, wd]
    if split_out:
        n_p = N_PROMPT // tile
        out_specs = [pl.BlockSpec((tile, D), lambda i: (jnp.minimum(i, n_p - 1), 0)),
                     pl.BlockSpec((tile, D), lambda i: (jnp.maximum(i - n_p, 0), 0))]
        out_shape = [jax.ShapeDtypeStruct((N_PROMPT, D), F32), jax.ShapeDtypeStruct((N_SAMPLE, D), F32)]
    else:
        out_specs = pl.BlockSpec((tile, D), lambda i: (i, 0))
        out_shape = jax.ShapeDtypeStruct((N_TOK, D), F32)
    return pl.pallas_call(
        functools.partial(_post_kernel, tile, len(y_args), len(mix_args), has_proj, split_out),
        grid=(N_TOK // tile,),
        in_specs=in_specs,
        out_specs=out_specs,
        out_shape=out_shape,
        compiler_params=_cparams(("arbitrary",) if split_out else ("parallel",)),
        name="post_proj" if has_proj else "post_noproj",
    )(*args)


def _mix_tile_flags(i):
    first_sample = N_PROMPT // TMIX
    tiles_prompt = SEQ // TMIX
    tiles_sample = DEC_SEQ // TMIX
    is_p = i < first_sample
    j = jnp.where(is_p, i % tiles_prompt, (i - first_sample) % tiles_sample)
    n = jnp.where(is_p, tiles_prompt, tiles_sample)
    return j > 0, j < n - 1, j * TMIX, n * TMIX


def _halo_specs():
    blocks_per_tile = TMIX // HALO
    last = N_TOK // HALO - 1
    return [
        pl.BlockSpec((HALO, D), lambda i: (jnp.maximum(i * blocks_per_tile - 1, 0), 0)),
        pl.BlockSpec((TMIX, D), lambda i: (i, 0)),
        pl.BlockSpec((HALO, D), lambda i: (jnp.minimum((i + 1) * blocks_per_tile, last), 0)),
    ]


EXT = TMIX + 2 * HALO


def _ext_tile(yp_ref, y_ref, yn_ref, m_ref, has_prev, has_next):
    scale1 = 1.0 + m_ref[1:2, :]
    shift = m_ref[0:1, :]
    h = y_ref[...] * scale1 + shift
    before = jnp.where(has_prev, yp_ref[...] * scale1 + shift, 0.0)
    after = jnp.where(has_next, yn_ref[...] * scale1 + shift, 0.0)
    return h, jnp.concatenate([before, h, after], axis=0)


def _shifted(ext, j):
    if j == 0:
        return ext[HALO:HALO + TMIX]
    return pltpu.roll(ext, (-j) % EXT, 0)[HALO:HALO + TMIX]


def _rwkv_pre_kernel(yp_ref, y_ref, yn_ref, m_ref, mu_ref, wr_ref, wk_ref, wv_ref, g1_ref, g2_ref,
                     w1_ref, w2_ref, a1_ref, a2_ref, w0_ref, a0_ref, kk_ref, ka_ref, rk_ref, ones_ref,
                     r_out, v_out, g_out, kk_out, kd_out, b_out, ld_out, bonus_out):
    i = pl.program_id(0)
    has_prev, has_next, _, _ = _mix_tile_flags(i)
    h, ext = _ext_tile(yp_ref, y_ref, yn_ref, m_ref, has_prev, has_next)
    xx = 0.5 * (_shifted(ext, -1) + _shifted(ext, 1)) - h

    def mix(j):
        return (h + xx * mu_ref[j:j + 1, :]).astype(BF16)

    r = _dot(mix(0), wr_ref[...])
    r_out[...] = r
    k = _dot(mix(2), wk_ref[...])
    v = _dot(mix(3), wv_ref[...])
    v_out[...] = v
    g_out[...] = _dot(_sigmoid(_dot(mix(5), g1_ref[...])).astype(BF16), g2_ref[...])
    tw = jnp.tanh(_dot(mix(1), w1_ref[...])).astype(BF16)
    ta = _dot(mix(4), a1_ref[...]).astype(BF16)

    kkf = k * kk_ref[...]
    kk = kkf * jnp.minimum(lax.rsqrt(_seg_sum(kkf * kkf, ones_ref[...])), 1e12)
    kk_out[...] = kk
    kd_sum = None
    for d in range(2):
        ld_out[d] = -EXP_M_HALF * _sigmoid(w0_ref[d:d + 1, :] + _dot(tw, w2_ref[d]))
        a = _sigmoid(a0_ref[d:d + 1, :] + _dot(ta, a2_ref[d]))
        kd = k * (1.0 + (a - 1.0) * ka_ref[...])
        kd_out[d] = kd
        b_out[d] = kk * a
        kd_sum = kd if kd_sum is None else kd_sum + kd
    bonus_out[...] = _seg_sum(r * kd_sum * rk_ref[...], ones_ref[...]) * v


def _rwkv_pre_call(y, mods, w):
    const2 = lambda i: (0, 0)
    const3 = lambda i: (0, 0, 0)
    row = pl.BlockSpec((TMIX, D), lambda i: (i, 0))
    row2 = pl.BlockSpec((2, TMIX, D), lambda i: (0, i, 0))
    one = jax.ShapeDtypeStruct((N_TOK, D), F32)
    two = jax.ShapeDtypeStruct((2, N_TOK, D), F32)
    return pl.pallas_call(
        _rwkv_pre_kernel,
        grid=(N_TOK // TMIX,),
        in_specs=_halo_specs() + [
            pl.BlockSpec((None, SUBLANES, D), lambda i: (_cond_of_tile(i, TMIX), 0, 0)),
            pl.BlockSpec((6, D), const2),
            pl.BlockSpec((D, D), const2), pl.BlockSpec((D, D), const2), pl.BlockSpec((D, D), const2),
            pl.BlockSpec((D, 128), const2), pl.BlockSpec((128, D), const2),
            pl.BlockSpec((D, 128), const2), pl.BlockSpec((2, 128, D), const3),
            pl.BlockSpec((D, 128), const2), pl.BlockSpec((2, 128, D), const3),
            pl.BlockSpec((2, D), const2), pl.BlockSpec((2, D), const2),
            pl.BlockSpec((1, D), const2), pl.BlockSpec((1, D), const2), pl.BlockSpec((1, D), const2),
            pl.BlockSpec((256, 256), const2),
        ],
        out_specs=[row, row, row, row, row2, row2, row2, row],
        out_shape=[one, one, one, one, two, two, two, one],
        compiler_params=_cparams(("parallel",)),
        name="rwkv_pre",
    )(y, y, y, mods, w["mu"], w["wr"], w["wk"], w["wv"], w["g1"], w["g2"], w["w1"], w["w2"],
      w["a1"], w["a2"], w["w0"], w["a0"], w["k_k"], w["k_a"], w["r_k"], w["ones_bd"])


def _scan_kernel(n_chunks, has_s0, aliased, *refs):
    ins = [refs[0:6], refs[6:12]]
    pos = 12
    s0_ref = refs[pos] if has_s0 else None
    pos += (1 if has_s0 else 0) + (2 if aliased else 0)
    y_refs = refs[pos:pos + 2]
    sf_ref, s_ref = refs[pos + 2], refs[pos + 3]
    n = pl.program_id(1)
    C = CHUNK
    P2 = 2 * C
    units = [(d, p) for d in range(2) for p in range(HEADS // 2)]
    lanes = [slice(2 * RWKV_N * p, 2 * RWKV_N * (p + 1)) for p in range(HEADS // 2)]

    @pl.when(n == 0)
    def _():
        s_ref[...] = jnp.zeros_like(s_ref)
        if has_s0:
            for d in range(2):
                for hd in range(HEADS):
                    o = RWKV_N * (hd % 2)
                    s_ref[d, hd // 2, o:o + RWKV_N, o:o + RWKV_N] = s0_ref[d, hd]

    lane = lax.broadcasted_iota(jnp.int32, (C, 2 * RWKV_N), 1)
    even = lane < RWKV_N
    ti = lax.broadcasted_iota(jnp.int32, (C, C), 0)
    si = lax.broadcasted_iota(jnp.int32, (C, C), 1)
    r2 = lax.broadcasted_iota(jnp.int32, (P2, P2), 0)
    c2 = lax.broadcasted_iota(jnp.int32, (P2, P2), 1)
    same = (r2 // C) == (c2 // C)
    eye = (r2 == c2).astype(F32)

    def same_block(size):
        return (r2 // size) == (c2 // size)

    leaf = same_block(2)
    merges = [same_block(2 * size) & jnp.logical_not(same_block(size)) for size in (2, 4, 8, 16, 32)]

    def split(x):
        return jnp.concatenate([jnp.where(even, x, 0.0), jnp.where(even, 0.0, x)], axis=0)

    def twice(x):
        return jnp.concatenate([x, x], axis=0)

    m_incl, m_strict, at, rt, bt, kt, bh, kh, pc, v = ([None, None] for _ in range(10))
    for d in range(2):
        r_ref, v_ref, kk_ref, kd_ref, b_ref, ld_ref = ins[d]
        sign = 1 - 2 * d
        cum = (sign * (si - ti) <= 0).astype(BF16)
        order = sign * (c2 % C - r2 % C)
        m_incl[d] = same & (order <= 0)
        m_strict[d] = same & (order < 0)
        ld = ld_ref[...]
        ld_hi, ld_mid, ld_lo = _split3(ld)
        cs = _dot(cum, ld_hi) + _dot(cum, ld_mid) + _dot(cum, ld_lo)
        last = cs[0:1, :] if d == 1 else cs[C - 1:C, :]
        pin = jnp.exp(cs)
        pinv = jnp.exp(-cs)
        at[d] = -kk_ref[...] * jnp.exp(cs - ld)
        rt[d] = r_ref[...] * pin
        bt[d] = b_ref[...] * pinv
        kt[d] = kd_ref[...] * pinv
        tail = jnp.exp(last - cs)
        bh[d] = b_ref[...] * tail
        kh[d] = kd_ref[...] * tail
        pc[d] = jnp.exp(last)
        v[d] = v_ref[...]

    s0 = [s_ref[d, p] for d, p in units]
    lhs = [jnp.concatenate([split(at[d][:, lanes[p]]), split(rt[d][:, lanes[p]])], axis=0).astype(BF16)
           for d, p in units]
    rhs = [jnp.concatenate([twice(bt[d][:, lanes[p]]), twice(kt[d][:, lanes[p]])], axis=0).astype(BF16)
           for d, p in units]
    v2b = [split(v[d][:, lanes[p]]).astype(BF16) for d, p in units]
    idx = range(len(units))
    g = [_dot_nt(lhs[u], rhs[u]) for u in idx]
    xs = [_dot_nt(lhs[u], s0[u].astype(BF16)) for u in idx]
    a_ab = [jnp.where(m_strict[units[u][0]], g[u][:P2, :P2], 0.0) for u in idx]
    inv = [eye + jnp.where(leaf, a_ab[u], 0.0) for u in idx]
    for mk in merges:
        invb = [inv[u].astype(BF16) for u in idx]
        binv = [_dot(jnp.where(mk, a_ab[u], 0.0).astype(BF16), invb[u]).astype(BF16) for u in idx]
        inv = [inv[u] + _dot(invb[u], binv[u]) for u in idx]
    a_ak = [jnp.where(m_strict[units[u][0]], g[u][:P2, P2:], 0.0).astype(BF16) for u in idx]
    rhs_u = [(xs[u][:P2] + _dot(a_ak[u], v2b[u])).astype(BF16) for u in idx]
    u2b = [_dot(inv[u].astype(BF16), rhs_u[u]).astype(BF16) for u in idx]
    m_rb = [jnp.where(m_incl[units[u][0]], g[u][P2:, :P2], 0.0).astype(BF16) for u in idx]
    m_rk = [jnp.where(m_incl[units[u][0]], g[u][P2:, P2:], 0.0).astype(BF16) for u in idx]
    for u, (d, p) in enumerate(units):
        y2 = xs[u][P2:] + _dot(m_rb[u], u2b[u]) + _dot(m_rk[u], v2b[u])
        y_refs[d][:, lanes[p]] = y2[:C] + y2[C:]
    for u, (d, p) in enumerate(units):
        bh2 = split(bh[d][:, lanes[p]]).astype(BF16)
        kh2 = split(kh[d][:, lanes[p]]).astype(BF16)
        s_ref[d, p] = s0[u] * pc[d][:, lanes[p]] + _dot_tn(u2b[u], bh2) + _dot_tn(v2b[u], kh2)

    @pl.when(n == n_chunks - 1)
    def _():
        for d in range(2):
            for hd in range(HEADS):
                o = RWKV_N * (hd % 2)
                sf_ref[d, hd] = s_ref[d, hd // 2, o:o + RWKV_N, o:o + RWKV_N]


def _scan_call(r, v, kk, kd, b, ld, s0, y_bufs, n_seq, t_seq, row0, name):
    n_chunks = t_seq // CHUNK
    blk0 = row0 // CHUNK
    chunk = [lambda bb, nn: blk0 + bb * n_chunks + nn,
             lambda bb, nn: blk0 + bb * n_chunks + n_chunks - 1 - nn]
    in_specs, args = [], []
    for d in range(2):
        row = pl.BlockSpec((CHUNK, D), lambda bb, nn, d=d: (chunk[d](bb, nn), 0))
        row2 = pl.BlockSpec((None, CHUNK, D), lambda bb, nn, d=d: (d, chunk[d](bb, nn), 0))
        in_specs += [row, row, row, row2, row2, row2]
        args += [r, v, kk, kd, b, ld]
    st = pl.BlockSpec((None, 2, HEADS, RWKV_N, RWKV_N), lambda bb, nn: (bb, 0, 0, 0, 0))
    if s0 is not None:
        in_specs.append(st)
        args.append(s0)
    aliases = {}
    if y_bufs is not None:
        aliases = {len(args): 0, len(args) + 1: 1}
        in_specs += [pl.BlockSpec(memory_space=pl.ANY)] * 2
        args += list(y_bufs)
    out = pl.pallas_call(
        functools.partial(_scan_kernel, n_chunks, s0 is not None, y_bufs is not None),
        grid=(n_seq, n_chunks),
        in_specs=in_specs,
        out_specs=[
            pl.BlockSpec((CHUNK, D), lambda bb, nn: (chunk[0](bb, nn), 0)),
            pl.BlockSpec((CHUNK, D), lambda bb, nn: (chunk[1](bb, nn), 0)),
            st,
        ],
        out_shape=[
            jax.ShapeDtypeStruct((N_TOK, D), F32),
            jax.ShapeDtypeStruct((N_TOK, D), F32),
            jax.ShapeDtypeStruct((n_seq, 2, HEADS, RWKV_N, RWKV_N), F32),
        ],
        scratch_shapes=[pltpu.VMEM((2, HEADS // 2, 2 * RWKV_N, 2 * RWKV_N), F32)],
        input_output_aliases=aliases,
        compiler_params=_cparams(("parallel", "arbitrary")),
        name=name,
    )(*args)
    return (out[0], out[1]), out[2]


def _pool_kernel(yp_ref, y_ref, yn_ref, m_ref, w_ref, sc_ref, o_ref):
    i = pl.program_id(0)
    has_prev, has_next, pos0, seq_len = _mix_tile_flags(i)
    h, ext = _ext_tile(yp_ref, y_ref, yn_ref, m_ref, has_prev, has_next)
    t = pos0 + lax.broadcasted_iota(jnp.int32, (TMIX, 1), 0)
    for gi, win in enumerate(POOL_WINDOWS):
        cols = slice(POOL_C * gi, POOL_C * (gi + 1))
        ext_g = ext[:, cols]
        acc = None
        for j in range(-(win // 2), win - win // 2):
            x = _shifted(ext_g, j)
            acc = x if acc is None else acc + x
        lo = jnp.maximum(t - win // 2, 0)
        hi = jnp.minimum(t - win // 2 + win, seq_len)
        cnt = (hi - lo).astype(F32)
        pooled = (acc / cnt - h[:, cols]).astype(BF16)
        o_ref[:, cols] = _dot(pooled, w_ref[gi]) * sc_ref[:, cols]


def _pool_call(y, mods, pool_w, pool_scale):
    return pl.pallas_call(
        _pool_kernel,
        grid=(N_TOK // TMIX,),
        in_specs=_halo_specs() + [
            pl.BlockSpec((None, SUBLANES, D), lambda i: (_cond_of_tile(i, TMIX), 0, 0)),
            pl.BlockSpec((4, POOL_C, POOL_C), lambda i: (0, 0, 0)),
            pl.BlockSpec((1, D), lambda i: (0, 0)),
        ],
        out_specs=pl.BlockSpec((TMIX, D), lambda i: (i, 0)),
        out_shape=jax.ShapeDtypeStruct((N_TOK, D), F32),
        compiler_params=_cparams(("parallel",)),
        name="pool",
    )(y, y, y, mods, pool_w, pool_scale)


def _rope_tables():
    rows = DEC_SEQ // GRID_W
    row = jnp.repeat(jnp.arange(rows, dtype=F32), GRID_W)
    col = jnp.tile(jnp.arange(GRID_W, dtype=F32), rows)
    n_freq = ROPE // 4
    inv_freq = ROPE_THETA ** (-jnp.arange(n_freq, dtype=F32) / n_freq)
    ang = jnp.stack([row[:, None] * inv_freq, col[:, None] * inv_freq], axis=1)
    cos, sin = jnp.cos(ang), jnp.sin(ang)
    cos32 = jnp.concatenate([cos, cos], axis=-1).reshape(DEC_SEQ, ROPE)
    sin32 = jnp.concatenate([-sin, sin], axis=-1).reshape(DEC_SEQ, ROPE)
    ones = jnp.ones((DEC_SEQ, NOPE), F32)
    zeros = jnp.zeros((DEC_SEQ, NOPE), F32)
    tail = HEAD_PAD - NOPE - ROPE
    cos_t = jnp.concatenate([ones, cos32, jnp.zeros((DEC_SEQ, tail), F32)], axis=1)
    sin_t = jnp.concatenate([zeros, sin32, jnp.zeros((DEC_SEQ, tail), F32)], axis=1)
    return cos_t, sin_t


def _swap_halves(w):
    s = w.reshape(*w.shape[:-1], 2, 2, ROPE // 4)
    return s[..., ::-1, :].reshape(w.shape)


def _pad_heads(w):
    k, _, width = w.shape
    return jnp.pad(w, ((0, 0), (0, 0), (0, HEAD_PAD - width))).reshape(k, HEADS * HEAD_PAD)


def _pad_heads_alternating(w):
    k = w.shape[0]
    pair = w.reshape(k, HEADS // 2, 2, VDIM)
    z = jnp.zeros((k, HEADS // 2, VDIM), w.dtype)
    return jnp.concatenate([pair[:, :, 0], z, z, pair[:, :, 1]], axis=-1).reshape(k, HEADS * HEAD_PAD)


def _mla_weights(wq_a, q_norm, wq_b, wkv_a, kv_norm, wkv_b, wo, cos_t, sin_t):
    wq_b3 = wq_b.reshape(Q_LORA, HEADS, NOPE + ROPE)
    wq_b4 = jnp.concatenate([wq_b3, _swap_halves(wq_b3[..., NOPE:])], axis=-1)
    w_pe = wkv_a[:, KV_LORA:]
    w_pe4 = jnp.concatenate([jnp.zeros((D, NOPE), F32), w_pe, _swap_halves(w_pe)], axis=1)
    wkv_b3 = wkv_b.reshape(KV_LORA, HEADS, NOPE + VDIM)
    return {
        "w_a": jnp.concatenate([wq_a, wkv_a[:, :KV_LORA], w_pe4], axis=1).astype(BF16),
        "q_norm": q_norm.reshape(1, Q_LORA),
        "wq_b": _pad_heads(wq_b4).astype(BF16),
        "kv_norm": kv_norm.reshape(1, KV_LORA),
        "wkv_b_k": _pad_heads(wkv_b3[..., :NOPE]).astype(BF16),
        "wkv_b_v": _pad_heads_alternating(wkv_b3[..., NOPE:]).astype(BF16),
        "v_one": jnp.asarray(_V_ONE),
        "wo": wo.astype(BF16),
        "cos": cos_t,
        "sin": sin_t,
    }


def _dir_pad(w):
    z = jnp.zeros_like(w[0])
    return jnp.stack([jnp.concatenate([w[0], z], axis=0), jnp.concatenate([z, w[1]], axis=0)])


def _rwkv_weights(mu, wr, wk, wv, w0, w1, w2, a0, a1, a2, g1, g2, k_k, k_a, r_k, lnx_g, lnx_b, wo):
    blk = np.arange(256) // RWKV_N
    return {
        "mu": mu,
        "wr": wr.astype(BF16), "wk": wk.astype(BF16), "wv": wv.astype(BF16),
        "g1": g1.astype(BF16), "g2": g2.astype(BF16),
        "w1": jnp.concatenate([w1[0], w1[1]], axis=1).astype(BF16),
        "w2": _dir_pad(w2).astype(BF16),
        "a1": jnp.concatenate([a1[0], a1[1]], axis=1).astype(BF16),
        "a2": _dir_pad(a2).astype(BF16),
        "w0": w0, "a0": a0,
        "k_k": k_k.reshape(1, D), "k_a": k_a.reshape(1, D),
        "r_k": r_k.reshape(1, D), "lnx_g": lnx_g.reshape(1, D), "lnx_b": lnx_b.reshape(1, D),
        "wo": wo.astype(BF16),
        "ones_bd": jnp.asarray(blk[:, None] == blk[None, :], BF16),
    }


def _mla_layer(y, mods, w, cache_ckv, cache_kpe):
    q, ckv, kpe = _mla_pre_call(y, mods, w)
    cache_kpe_pad = jnp.pad(cache_kpe, ((0, 0), (0, 0), (NOPE, HEAD_PAD - NOPE - ROPE)))
    t_k = PAST + DEC_SEQ
    k_all, v_all = _kv_expand_call(cache_ckv.reshape(DEC_BATCH * PAST, KV_LORA),
                                   cache_kpe_pad.reshape(DEC_BATCH * PAST, HEAD_PAD), ckv, kpe, w)
    o = _attn_call(q, k_all, v_all, None, BATCH, SEQ, SEQ, 0, DEC_BATCH * t_k, "attn_prompt")
    o = _attn_call(q, k_all, v_all, o, DEC_BATCH, DEC_SEQ, t_k, N_PROMPT, 0, "attn_sample")
    new_ckv = ckv[:N_PROMPT].reshape(BATCH, SEQ, KV_LORA)
    new_kpe = kpe[:N_PROMPT, NOPE:NOPE + ROPE].reshape(BATCH, SEQ, ROPE)
    return o, new_ckv, new_kpe


def _rwkv_layer(y, mods, w, state):
    r, v, g, kk, kd, b, ld, bonus = _rwkv_pre_call(y, mods, w)
    y_fb, s_p = _scan_call(r, v, kk, kd, b, ld, None, None, BATCH, SEQ, 0, "scan_prompt")
    y_fb, _ = _scan_call(r, v, kk, kd, b, ld, state, y_fb, DEC_BATCH, DEC_SEQ, N_PROMPT, "scan_sample")
    return (y_fb[0], y_fb[1], bonus, g, w["lnx_g"], w["lnx_b"], w["ones_bd"]), s_p


def kernel(x_prompt, x_sample, cache_ckv, cache_kpe, state_wkv, c, c_ctx, ada_w, ada_b, ln_g, ln_b, ffn_wg, ffn_wu, ffn_wd, mla_wq_a, mla_q_norm, mla_wq_b, mla_wkv_a, mla_kv_norm, mla_wkv_b, mla_wo, rwkv_mu, rwkv_wr, rwkv_wk, rwkv_wv, rwkv_w0, rwkv_w1, rwkv_w2, rwkv_a0, rwkv_a1, rwkv_a2, rwkv_g1, rwkv_g2, rwkv_k_k, rwkv_k_a, rwkv_r_k, rwkv_lnx_g, rwkv_lnx_b, rwkv_wo, pool_w, pool_scale):
    y = (x_prompt.reshape(N_PROMPT, D), x_sample.reshape(N_SAMPLE, D))
    cond8 = jnp.concatenate([c_ctx[None, :], c, jnp.zeros((SUBLANES - 1 - DEC_BATCH, D), F32)], axis=0)
    mods_all = _ada_call(cond8, ada_w, ada_b)
    mods_all = jnp.pad(mods_all.reshape(DEPTH, SUBLANES, 6, D)[:, :1 + DEC_BATCH],
                       ((0, 0), (0, 0), (0, SUBLANES - 6), (0, 0)))
    cos_t, sin_t = _rope_tables()
    ffn_wg_bf, ffn_wu_bf, ffn_wd_bf = ffn_wg.astype(BF16), ffn_wu.astype(BF16), ffn_wd.astype(BF16)
    new_ckv, new_kpe, new_wkv = [], [], []
    for layer in range(DEPTH):
        kind, j = layer % 3, layer // 3
        mods = mods_all[layer]
        wo = None
        if kind == 0:
            w = _mla_weights(mla_wq_a[j], mla_q_norm[j], mla_wq_b[j], mla_wkv_a[j], mla_kv_norm[j],
                             mla_wkv_b[j], mla_wo[j], cos_t, sin_t)
            a, ckv, kpe = _mla_layer(y, mods, w, cache_ckv[:, j], cache_kpe[:, j])
            new_ckv.append(ckv)
            new_kpe.append(kpe)
            wo = w["wo"]
        elif kind == 1:
            w = _rwkv_weights(rwkv_mu[j], rwkv_wr[j], rwkv_wk[j], rwkv_wv[j], rwkv_w0[j], rwkv_w1[j],
                              rwkv_w2[j], rwkv_a0[j], rwkv_a1[j], rwkv_a2[j], rwkv_g1[j], rwkv_g2[j],
                              rwkv_k_k[j], rwkv_k_a[j], rwkv_r_k[j], rwkv_lnx_g[j], rwkv_lnx_b[j], rwkv_wo[j])
            a, s_new = _rwkv_layer(y, mods, w, state_wkv[:, j])
            new_wkv.append(s_new)
            wo = w["wo"]
        else:
            a = _pool_call(y, mods, pool_w[j].astype(BF16), pool_scale[j].reshape(1, D))
        y = _post_call(y, a, mods, ln_g[layer], ln_b[layer], wo, ffn_wg_bf, ffn_wu_bf, ffn_wd_bf, layer,
                       split_out=layer == DEPTH - 1)
    return (y[0].reshape(BATCH, SEQ, D), y[1].reshape(DEC_BATCH, DEC_SEQ, D),
            jnp.stack(new_ckv, axis=1), jnp.stack(new_kpe, axis=1), jnp.stack(new_wkv, axis=1))
```

```python
import functools

import jax
import jax.numpy as jnp
import numpy as np
from jax import lax
from jax.experimental import pallas as pl
from jax.experimental.pallas import tpu as pltpu

F32 = jnp.float32
BF16 = jnp.bfloat16

D = 1024
BATCH, SEQ = 32, 256
DEC_BATCH, DEC_SEQ = 2, 4096
PAST = 512
DEPTH = 4
GRID_W = 64
HEADS = 16
Q_LORA, KV_LORA = 384, 256
NOPE, ROPE, VDIM = 64, 32, 64
ROPE_THETA = 10000.0
RWKV_N = 64
RWKV_GN_EPS = 64e-5
POOL_WINDOWS = (2, 4, 8, 16)
POOL_C = D // 4
D_FF = 2816
ALPHA = (2 * DEPTH) ** 0.25
LN_EPS = 1e-5
RMS_EPS = 1e-6
LOG2_E = 1.4426950408889634
EXP_M_HALF = 0.6065306597126334

N_PROMPT = BATCH * SEQ
N_SAMPLE = DEC_BATCH * DEC_SEQ
N_TOK = N_PROMPT + N_SAMPLE

LANES = 128
SUBLANES = 8
HEAD_PAD = 128
VMEM_LIMIT = 56 * 1024 * 1024

TM = 512
TMIX = 256
HALO = SUBLANES
TQ = 256
ATTN_HEADS = 4
ATTN_KC = 512
ATTN_SKEW = 2
ATTN_GATE = 3
CHUNK = 64
SCAN_SEQS = 2
TFF = 256
POST_TM = 1024
POST_SUB = 512
POST_PRO_AT = 5
POST_EPI_AT = 2


def _cparams(sem):
    return pltpu.CompilerParams(dimension_semantics=sem, vmem_limit_bytes=VMEM_LIMIT)


def _dot(a, b):
    return jnp.dot(a, b, preferred_element_type=F32)


def _dot_nt(a, b):
    return lax.dot_general(a, b, (((1,), (1,)), ((), ())), preferred_element_type=F32)


def _dot_tn(a, b):
    return lax.dot_general(a, b, (((0,), (0,)), ((), ())), preferred_element_type=F32)


def _layer_norm(x, g, b):
    mu = jnp.mean(x, axis=-1, keepdims=True)
    xc = x - mu
    var = jnp.mean(xc * xc, axis=-1, keepdims=True)
    return xc * lax.rsqrt(var + LN_EPS) * g + b


def _rms_norm(x, g):
    return x * lax.rsqrt(jnp.mean(x * x, axis=-1, keepdims=True) + RMS_EPS) * g


def _sigmoid(x):
    return 0.5 * jnp.tanh(0.5 * x) + 0.5


def _silu(x):
    return x * _sigmoid(x)


def _cond_of_tile(i, tile):
    first_sample = N_PROMPT // tile
    per_seq = DEC_SEQ // tile
    return jnp.where(i < first_sample, 0, 1 + (i - first_sample) // per_seq)


def _split3(x):
    hi = x.astype(BF16)
    r1 = x - hi.astype(F32)
    mid = r1.astype(BF16)
    lo = (r1 - mid.astype(F32)).astype(BF16)
    return hi, mid, lo


def _seg_sum(x, ones_bd):
    outs = []
    for g in range(D // 256):
        hi, mid, lo = _split3(x[:, 256 * g:256 * (g + 1)])
        outs.append(_dot(hi, ones_bd) + _dot(mid, ones_bd) + _dot(lo, ones_bd))
    return jnp.concatenate(outs, axis=1)


def _ada_kernel(c_ref, w_ref, b_ref, o_ref):
    a = _silu(c_ref[...]).astype(BF16)
    o_ref[...] = _dot(a, w_ref[...].astype(BF16)) + b_ref[...]


def _ada_call(cond8, ada_w, ada_b):
    tn = 1536
    return pl.pallas_call(
        _ada_kernel,
        grid=(DEPTH, 6 * D // tn),
        in_specs=[
            pl.BlockSpec((SUBLANES, D), lambda l, n: (0, 0)),
            pl.BlockSpec((None, D, tn), lambda l, n: (l, 0, n)),
            pl.BlockSpec((None, 1, tn), lambda l, n: (l, 0, n)),
        ],
        out_specs=pl.BlockSpec((None, SUBLANES, tn), lambda l, n: (l, 0, n)),
        out_shape=jax.ShapeDtypeStruct((DEPTH, SUBLANES, 6 * D), F32),
        compiler_params=_cparams(("parallel", "parallel")),
        name="adaln",
    )(cond8, ada_w, ada_b.reshape(DEPTH, 1, 6 * D))


def _act_specs(y, tile):
    if not isinstance(y, tuple):
        return [pl.BlockSpec((tile, D), lambda i: (i, 0))], [y]
    n_p = N_PROMPT // tile
    return [pl.BlockSpec((tile, D), lambda i: (jnp.minimum(i, n_p - 1), 0)),
            pl.BlockSpec((tile, D), lambda i: (jnp.maximum(i - n_p, 0), 0))], list(y)


def _act_tile(y_refs, tile):
    if len(y_refs) == 1:
        return y_refs[0][...]
    return jnp.where(pl.program_id(0) < N_PROMPT // tile, y_refs[0][...], y_refs[1][...])


def _mla_pre_kernel(n_y, *refs):
    m_ref, wa_ref, qn_ref, wqb_ref, kvn_ref, cos_ref, sin_ref, q_ref, ckv_ref, kpe_ref = refs[n_y:]
    i = pl.program_id(0)
    h = (_act_tile(refs[:n_y], TM) * (1.0 + m_ref[1:2, :]) + m_ref[0:1, :]).astype(BF16)
    x = _dot(h, wa_ref[...])
    qa = _rms_norm(x[:, :Q_LORA], qn_ref[...]).astype(BF16)
    ckv_ref[...] = _rms_norm(x[:, Q_LORA:Q_LORA + KV_LORA], kvn_ref[...])
    kpe = x[:, Q_LORA + KV_LORA:]
    q = _dot(qa, wqb_ref[...])
    is_latent = i >= N_PROMPT // TM
    swap_shift = HEAD_PAD - ROPE

    @pl.when(jnp.logical_not(is_latent))
    def _():
        q_ref[...] = q.astype(BF16)
        lane = lax.broadcasted_iota(jnp.int32, kpe.shape, 1)
        kpe_ref[...] = jnp.where(lane < NOPE + ROPE, kpe, 0.0)

    @pl.when(is_latent)
    def _():
        cos = cos_ref[...]
        sin = sin_ref[...]
        kpe_ref[...] = kpe * cos + pltpu.roll(kpe, swap_shift, 1) * sin
        for hd in range(HEADS):
            sl = slice(HEAD_PAD * hd, HEAD_PAD * (hd + 1))
            q_ref[:, sl] = (q[:, sl] * cos + pltpu.roll(q[:, sl], swap_shift, 1) * sin).astype(BF16)


def _mla_pre_call(y, mods, w):
    first_sample = N_PROMPT // TM
    per_seq = DEC_SEQ // TM
    const = lambda i: (0, 0)
    rope_map = lambda i: (jnp.where(i < first_sample, 0, (i - first_sample) % per_seq), 0)
    y_specs, y_args = _act_specs(y, TM)
    return pl.pallas_call(
        functools.partial(_mla_pre_kernel, len(y_args)),
        grid=(N_TOK // TM,),
        in_specs=y_specs + [
            pl.BlockSpec((None, SUBLANES, D), lambda i: (_cond_of_tile(i, TM), 0, 0)),
            pl.BlockSpec((D, Q_LORA + KV_LORA + HEAD_PAD), const),
            pl.BlockSpec((1, Q_LORA), const),
            pl.BlockSpec((Q_LORA, HEADS * HEAD_PAD), const),
            pl.BlockSpec((1, KV_LORA), const),
            pl.BlockSpec((TM, HEAD_PAD), rope_map),
            pl.BlockSpec((TM, HEAD_PAD), rope_map),
        ],
        out_specs=[
            pl.BlockSpec((TM, HEADS * HEAD_PAD), lambda i: (i, 0)),
            pl.BlockSpec((TM, KV_LORA), lambda i: (i, 0)),
            pl.BlockSpec((TM, HEAD_PAD), lambda i: (i, 0)),
        ],
        out_shape=[
            jax.ShapeDtypeStruct((N_TOK, HEADS * HEAD_PAD), BF16),
            jax.ShapeDtypeStruct((N_TOK, KV_LORA), F32),
            jax.ShapeDtypeStruct((N_TOK, HEAD_PAD), F32),
        ],
        compiler_params=_cparams(("parallel",)),
        name="mla_pre",
    )(*y_args, mods, w["w_a"], w["q_norm"], w["wq_b"], w["kv_norm"], w["cos"], w["sin"])


KV_TILES_PER_SEQ = (PAST + DEC_SEQ) // TM
KV_ROWS = DEC_BATCH * (PAST + DEC_SEQ) + N_PROMPT


def _kv_tile_source(i):
    n_lat = DEC_BATCH * KV_TILES_PER_SEQ
    b = jnp.minimum(i // KV_TILES_PER_SEQ, DEC_BATCH - 1)
    j = i - b * KV_TILES_PER_SEQ
    is_cache = (i < n_lat) & (j < PAST // TM)
    new_tile = N_PROMPT // TM + b * (DEC_SEQ // TM) + jnp.maximum(j - PAST // TM, 0)
    return is_cache, b * (PAST // TM) + jnp.minimum(j, PAST // TM - 1), jnp.where(i < n_lat, new_tile, i - n_lat)


def _kv_expand_kernel(cckv_ref, ckpe_ref, ckv_ref, kpe_ref, wk_ref, wv_ref, vone_ref, k_ref, v_ref):
    is_cache, _, _ = _kv_tile_source(pl.program_id(0))
    c = jnp.where(is_cache, cckv_ref[...], ckv_ref[...]).astype(BF16)
    kpe = jnp.where(is_cache, ckpe_ref[...], kpe_ref[...])
    kn = _dot(c, wk_ref[...])
    for hd in range(HEADS):
        sl = slice(HEAD_PAD * hd, HEAD_PAD * (hd + 1))
        k_ref[:, sl] = (kn[:, sl] + kpe).astype(BF16)
    v_ref[...] = (_dot(c, wv_ref[...]) + vone_ref[...]).astype(BF16)


def _kv_expand_call(cache_ckv, cache_kpe, ckv, kpe, w):
    rows = KV_ROWS
    const = lambda i: (0, 0)
    return pl.pallas_call(
        _kv_expand_kernel,
        grid=(rows // TM,),
        in_specs=[
            pl.BlockSpec((TM, KV_LORA), lambda i: (_kv_tile_source(i)[1], 0)),
            pl.BlockSpec((TM, HEAD_PAD), lambda i: (_kv_tile_source(i)[1], 0)),
            pl.BlockSpec((TM, KV_LORA), lambda i: (_kv_tile_source(i)[2], 0)),
            pl.BlockSpec((TM, HEAD_PAD), lambda i: (_kv_tile_source(i)[2], 0)),
            pl.BlockSpec((KV_LORA, HEADS * HEAD_PAD), const),
            pl.BlockSpec((KV_LORA, HEADS * HEAD_PAD), const),
            pl.BlockSpec((1, HEADS * HEAD_PAD), const),
        ],
        out_specs=[
            pl.BlockSpec((TM, HEADS * HEAD_PAD), lambda i: (i, 0)),
            pl.BlockSpec((TM, HEADS * HEAD_PAD), lambda i: (i, 0)),
        ],
        out_shape=[jax.ShapeDtypeStruct((rows, HEADS * HEAD_PAD), BF16)] * 2,
        compiler_params=_cparams(("parallel",)),
        name="kv_expand",
    )(cache_ckv, cache_kpe, ckv, kpe, w["wkv_b_k"], w["wkv_b_v"], w["v_one"])


def _sum_lane(h):
    return HEAD_PAD - 1 if h % 2 == 0 else 0


_V_ONE = np.zeros((1, HEADS * HEAD_PAD), np.float32)
_V_ONE[0, [HEAD_PAD * h + _sum_lane(h) for h in range(HEADS)]] = 1.0


def _zero_like_col(x):
    bits = pltpu.bitcast(x, jnp.uint32)
    zero = lax.shift_right_logical(lax.shift_right_logical(bits, jnp.uint32(16)), jnp.uint32(16))
    return pltpu.bitcast(zero, F32)[:, 0:1]


def _attn_kernel(t_k, q_ref, k_ref, v_ref, *rest):
    o_ref = rest[-1]
    scale2 = (NOPE + ROPE) ** -0.5 * LOG2_E
    kc = min(ATTN_KC, t_k)
    chunks = [slice(c * kc, (c + 1) * kc) for c in range(t_k // kc)]
    n_heads = q_ref.shape[1] // HEAD_PAD
    head = [slice(HEAD_PAD * e, HEAD_PAD * (e + 1)) for e in range(n_heads)]

    def scores(e, c):
        return _dot_nt(q_ref[:, head[e]], k_ref[c, head[e]])

    lane = lax.broadcasted_iota(jnp.int32, (TQ, HEAD_PAD), 1)

    def normalised(e, acc):
        sl = _sum_lane(e)
        return jnp.where(lane == sl, 0.0, acc) * (1.0 / acc[:, sl:sl + 1])

    def store(outs):
        for j in range(n_heads // 2):
            o_ref[:, 2 * VDIM * j:2 * VDIM * (j + 1)] = (outs[2 * j] + outs[2 * j + 1]).astype(BF16)

    if len(chunks) == 1:
        s_all = [scores(e, chunks[0]) for e in range(n_heads)]
        p_all = [jnp.exp2((s - jnp.max(s, axis=-1, keepdims=True)) * scale2).astype(BF16) for s in s_all]
        store([normalised(e, _dot(p_all[e], v_ref[:, head[e]])) for e in range(n_heads)])
        return

    s_cur = [scores(0, c) for c in chunks]
    outs = []
    for e in range(n_heads):
        m = functools.reduce(jnp.maximum, [jnp.max(s, axis=-1, keepdims=True) for s in s_cur])
        s_next, accs = [], []
        for i in range(len(chunks) + ATTN_SKEW):
            if e + 1 < n_heads and i < len(chunks):
                s_next.append(scores(e + 1, chunks[i]))
            if i >= ATTN_SKEW:
                j = i - ATTN_SKEW
                mj = m + _zero_like_col(accs[j - ATTN_GATE]) if j >= ATTN_GATE else m
                p = jnp.exp2((s_cur[j] - mj) * scale2).astype(BF16)
                pv = _dot(p, v_ref[chunks[j], head[e]])
                accs.append(pv if not accs else accs[-1] + pv)
        outs.append(normalised(e, accs[-1]))
        s_cur = s_next
    store(outs)


def _attn_call(q, k, v, o_buf, n_seq, t_q, t_k, q_row0, k_row0, name):
    nq = t_q // TQ
    q_blk0 = q_row0 // TQ
    k_blk0 = k_row0 // t_k
    heads = ATTN_HEADS if t_k > ATTN_KC else HEADS
    width = heads * HEAD_PAD
    in_specs = [
        pl.BlockSpec((TQ, width), lambda b, p, i: (q_blk0 + b * nq + i, p)),
        pl.BlockSpec((t_k, width), lambda b, p, i: (k_blk0 + b, p)),
        pl.BlockSpec((t_k, width), lambda b, p, i: (k_blk0 + b, p)),
    ]
    args = [q, k, v]
    if o_buf is not None:
        in_specs.append(pl.BlockSpec(memory_space=pl.ANY))
        args.append(o_buf)
    return pl.pallas_call(
        functools.partial(_attn_kernel, t_k),
        grid=(n_seq, HEADS // heads, nq),
        in_specs=in_specs,
        out_specs=pl.BlockSpec((TQ, heads * VDIM), lambda b, p, i: (q_blk0 + b * nq + i, p)),
        out_shape=jax.ShapeDtypeStruct((N_TOK, HEADS * VDIM), BF16),
        input_output_aliases={} if o_buf is None else {3: 0},
        compiler_params=_cparams(("parallel", "parallel", "arbitrary")),
        name=name,
    )(*args)


def _rwkv_gate(rows, yf_ref, yb_ref, bonus_ref, g_ref, lg_ref, lb_ref, ones_ref):
    ones = ones_ref[...]
    y = yf_ref[rows, :] + yb_ref[rows, :]
    mu = _seg_sum(y, ones) * (1.0 / RWKV_N)
    yc = y - mu
    var = _seg_sum(yc * yc, ones) * (1.0 / RWKV_N)
    yn = yc * lax.rsqrt(var + RWKV_GN_EPS) * lg_ref[...] + lb_ref[...]
    return ((yn + bonus_ref[rows, :]) * g_ref[rows, :]).astype(BF16)


def _post_kernel(tile, n_y, n_mix, has_proj, split_out, *refs):
    y_refs = refs[:n_y]
    mix_refs = refs[n_y:n_y + n_mix]
    m_ref, lng_ref, lnb_ref = refs[n_y + n_mix:n_y + n_mix + 3]
    rest = refs[n_y + n_mix + 3:]
    if has_proj:
        wo_ref, rest = rest[0], rest[1:]
    wg_ref, wu_ref, wd_ref = rest[:3]
    o_refs = rest[3:]
    is_prompt = pl.program_id(0) < N_PROMPT // tile
    blocks = [slice(POST_SUB * s, POST_SUB * (s + 1)) for s in range(tile // POST_SUB)]
    n_chunks = D_FF // TFF

    def prologue(rows):
        a = mix_refs[0][rows, :] if n_mix == 1 else _rwkv_gate(rows, *mix_refs)
        mix = _dot(a, wo_ref[...]) if has_proj else a
        y = y_refs[0][rows, :] if n_y == 1 else jnp.where(is_prompt, y_refs[0][rows, :], y_refs[1][rows, :])
        y1 = _layer_norm(ALPHA * y + m_ref[2:3, :] * mix, lng_ref[0:1, :], lnb_ref[0:1, :])
        return y1, (y1 * (1.0 + m_ref[4:5, :]) + m_ref[3:4, :]).astype(BF16)

    def ffn_chunk(h, j):
        cols = slice(TFF * j, TFF * (j + 1))
        act = (_silu(_dot(h, wg_ref[:, cols])) * _dot(h, wu_ref[:, cols])).astype(BF16)
        return _dot(act, wd_ref[cols, :])

    def epilogue(rows, y1, acc):
        out = _layer_norm(ALPHA * y1 + m_ref[5:6, :] * acc, lng_ref[1:2, :], lnb_ref[1:2, :])
        if not split_out:
            o_refs[0][rows, :] = out
        else:
            @pl.when(is_prompt)
            def _():
                o_refs[0][rows, :] = out

            @pl.when(jnp.logical_not(is_prompt))
            def _():
                o_refs[1][rows, :] = out

    state = [None] * len(blocks)
    state[0] = prologue(blocks[0])
    pending = None
    for s, rows in enumerate(blocks):
        y1, h = state[s]
        acc = None
        for j in range(n_chunks):
            part = ffn_chunk(h, j)
            acc = part if acc is None else acc + part
            if j == POST_PRO_AT and s + 1 < len(blocks):
                state[s + 1] = prologue(blocks[s + 1])
            if j == POST_EPI_AT and pending is not None:
                epilogue(*pending)
                pending = None
        pending = (rows, y1, acc)
    epilogue(*pending)


def _post_call(y, a, mods, ln_g, ln_b, wo, wg, wu, wd, layer, split_out=False):
    has_proj = wo is not None
    const = lambda i: (0, 0)
    this_layer = lambda i: (layer, 0, 0)
    resident = pl.Buffered(1)
    tile = POST_SUB if isinstance(a, tuple) else POST_TM
    y_specs, y_args = _act_specs(y, tile)
    row = pl.BlockSpec((tile, D), lambda i: (i, 0))
    if isinstance(a, tuple):
        mix_specs = [row] * 4 + [pl.BlockSpec((1, D), const)] * 2 + [pl.BlockSpec((256, 256), const)]
        mix_args = list(a)
    else:
        mix_specs, mix_args = [row], [a]
    in_specs = y_specs + mix_specs + [
        pl.BlockSpec((None, SUBLANES, D), lambda i: (_cond_of_tile(i, tile), 0, 0)),
        pl.BlockSpec((2, D), const),
        pl.BlockSpec((2, D), const),
    ]
    args = y_args + mix_args + [mods, ln_g, ln_b]
    if has_proj:
        in_specs.append(pl.BlockSpec((D, D), const, pipeline_mode=resident))
        args.append(wo)
    in_specs += [
        pl.BlockSpec((None, D, D_FF), this_layer, pipeline_mode=resident),
        pl.BlockSpec((None, D, D_FF), this_layer, pipeline_mode=resident),
        pl.BlockSpec((None, D_FF, D), this_layer, pipeline_mode=resident),
    ]
    args += [wg, wu, wd]
    if split_out:
        n_p = N_PROMPT // tile
        out_specs = [pl.BlockSpec((tile, D), lambda i: (jnp.minimum(i, n_p - 1), 0)),
                     pl.BlockSpec((tile, D), lambda i: (jnp.maximum(i - n_p, 0), 0))]
        out_shape = [jax.ShapeDtypeStruct((N_PROMPT, D), F32), jax.ShapeDtypeStruct((N_SAMPLE, D), F32)]
    else:
        out_specs = pl.BlockSpec((tile, D), lambda i: (i, 0))
        out_shape = jax.ShapeDtypeStruct((N_TOK, D), F32)
    return pl.pallas_call(
        functools.partial(_post_kernel, tile, len(y_args), len(mix_args), has_proj, split_out),
        grid=(N_TOK // tile,),
        in_specs=in_specs,
        out_specs=out_specs,
        out_shape=out_shape,
        compiler_params=_cparams(("arbitrary",) if split_out else ("parallel",)),
        name="post_proj" if has_proj else "post_noproj",
    )(*args)


def _mix_tile_flags(i):
    first_sample = N_PROMPT // TMIX
    tiles_prompt = SEQ // TMIX
    tiles_sample = DEC_SEQ // TMIX
    is_p = i < first_sample
    j = jnp.where(is_p, i % tiles_prompt, (i - first_sample) % tiles_sample)
    n = jnp.where(is_p, tiles_prompt, tiles_sample)
    return j > 0, j < n - 1, j * TMIX, n * TMIX


def _halo_specs():
    blocks_per_tile = TMIX // HALO
    last = N_TOK // HALO - 1
    return [
        pl.BlockSpec((HALO, D), lambda i: (jnp.maximum(i * blocks_per_tile - 1, 0), 0)),
        pl.BlockSpec((TMIX, D), lambda i: (i, 0)),
        pl.BlockSpec((HALO, D), lambda i: (jnp.minimum((i + 1) * blocks_per_tile, last), 0)),
    ]


EXT = TMIX + 2 * HALO


def _ext_tile(yp_ref, y_ref, yn_ref, m_ref, has_prev, has_next):
    scale1 = 1.0 + m_ref[1:2, :]
    shift = m_ref[0:1, :]
    h = y_ref[...] * scale1 + shift
    before = jnp.where(has_prev, yp_ref[...] * scale1 + shift, 0.0)
    after = jnp.where(has_next, yn_ref[...] * scale1 + shift, 0.0)
    return h, jnp.concatenate([before, h, after], axis=0)


def _shifted(ext, j):
    if j == 0:
        return ext[HALO:HALO + TMIX]
    return pltpu.roll(ext, (-j) % EXT, 0)[HALO:HALO + TMIX]


def _rwkv_pre_kernel(yp_ref, y_ref, yn_ref, m_ref, mu_ref, wr_ref, wk_ref, wv_ref, g1_ref, g2_ref,
                     w1_ref, w2_ref, a1_ref, a2_ref, w0_ref, a0_ref, kk_ref, ka_ref, rk_ref, ones_ref,
                     r_out, v_out, g_out, kk_out, kd_out, b_out, ld_out, bonus_out):
    i = pl.program_id(0)
    has_prev, has_next, _, _ = _mix_tile_flags(i)
    h, ext = _ext_tile(yp_ref, y_ref, yn_ref, m_ref, has_prev, has_next)
    xx = 0.5 * (_shifted(ext, -1) + _shifted(ext, 1)) - h

    def mix(j):
        return (h + xx * mu_ref[j:j + 1, :]).astype(BF16)

    r = _dot(mix(0), wr_ref[...])
    r_out[...] = r
    k = _dot(mix(2), wk_ref[...])
    v = _dot(mix(3), wv_ref[...])
    v_out[...] = v
    g_out[...] = _dot(_sigmoid(_dot(mix(5), g1_ref[...])).astype(BF16), g2_ref[...])
    tw = jnp.tanh(_dot(mix(1), w1_ref[...])).astype(BF16)
    ta = _dot(mix(4), a1_ref[...]).astype(BF16)

    kkf = k * kk_ref[...]
    kk = kkf * jnp.minimum(lax.rsqrt(_seg_sum(kkf * kkf, ones_ref[...])), 1e12)
    kk_out[...] = kk
    kd_sum = None
    for d in range(2):
        ld_out[d] = -EXP_M_HALF * _sigmoid(w0_ref[d:d + 1, :] + _dot(tw, w2_ref[d]))
        a = _sigmoid(a0_ref[d:d + 1, :] + _dot(ta, a2_ref[d]))
        kd = k * (1.0 + (a - 1.0) * ka_ref[...])
        kd_out[d] = kd
        b_out[d] = kk * a
        kd_sum = kd if kd_sum is None else kd_sum + kd
    bonus_out[...] = _seg_sum(r * kd_sum * rk_ref[...], ones_ref[...]) * v


def _rwkv_pre_call(y, mods, w):
    const2 = lambda i: (0, 0)
    const3 = lambda i: (0, 0, 0)
    row = pl.BlockSpec((TMIX, D), lambda i: (i, 0))
    row2 = pl.BlockSpec((2, TMIX, D), lambda i: (0, i, 0))
    one = jax.ShapeDtypeStruct((N_TOK, D), F32)
    two = jax.ShapeDtypeStruct((2, N_TOK, D), F32)
    return pl.pallas_call(
        _rwkv_pre_kernel,
        grid=(N_TOK // TMIX,),
        in_specs=_halo_specs() + [
            pl.BlockSpec((None, SUBLANES, D), lambda i: (_cond_of_tile(i, TMIX), 0, 0)),
            pl.BlockSpec((6, D), const2),
            pl.BlockSpec((D, D), const2), pl.BlockSpec((D, D), const2), pl.BlockSpec((D, D), const2),
            pl.BlockSpec((D, 128), const2), pl.BlockSpec((128, D), const2),
            pl.BlockSpec((D, 128), const2), pl.BlockSpec((2, 128, D), const3),
            pl.BlockSpec((D, 128), const2), pl.BlockSpec((2, 128, D), const3),
            pl.BlockSpec((2, D), const2), pl.BlockSpec((2, D), const2),
            pl.BlockSpec((1, D), const2), pl.BlockSpec((1, D), const2), pl.BlockSpec((1, D), const2),
            pl.BlockSpec((256, 256), const2),
        ],
        out_specs=[row, row, row, row, row2, row2, row2, row],
        out_shape=[one, one, one, one, two, two, two, one],
        compiler_params=_cparams(("parallel",)),
        name="rwkv_pre",
    )(y, y, y, mods, w["mu"], w["wr"], w["wk"], w["wv"], w["g1"], w["g2"], w["w1"], w["w2"],
      w["a1"], w["a2"], w["w0"], w["a0"], w["k_k"], w["k_a"], w["r_k"], w["ones_bd"])


def _scan_kernel(n_chunks, has_s0, aliased, *refs):
    ins = [refs[0:6], refs[6:12]]
    pos = 12
    s0_ref = refs[pos] if has_s0 else None
    pos += (1 if has_s0 else 0) + (2 if aliased else 0)
    y_refs = refs[pos:pos + 2]
    sf_ref, s_ref = refs[pos + 2], refs[pos + 3]
    n = pl.program_id(1)
    C = CHUNK
    P2 = 2 * C
    streams = [(s, d) for s in range(SCAN_SEQS) for d in range(2)]
    units = [(q, p) for q in range(len(streams)) for p in range(HEADS // 2)]
    lanes = [slice(2 * RWKV_N * p, 2 * RWKV_N * (p + 1)) for p in range(HEADS // 2)]

    @pl.when(n == 0)
    def _():
        s_ref[...] = jnp.zeros_like(s_ref)
        if has_s0:
            for q, (s, d) in enumerate(streams):
                for hd in range(HEADS):
                    o = RWKV_N * (hd % 2)
                    s_ref[q, hd // 2, o:o + RWKV_N, o:o + RWKV_N] = s0_ref[s, d, hd]

    lane = lax.broadcasted_iota(jnp.int32, (C, 2 * RWKV_N), 1)
    even = lane < RWKV_N
    ti = lax.broadcasted_iota(jnp.int32, (C, C), 0)
    si = lax.broadcasted_iota(jnp.int32, (C, C), 1)
    r2 = lax.broadcasted_iota(jnp.int32, (P2, P2), 0)
    c2 = lax.broadcasted_iota(jnp.int32, (P2, P2), 1)
    same = (r2 // C) == (c2 // C)
    eye = (r2 == c2).astype(F32)

    def same_block(size):
        return (r2 // size) == (c2 // size)

    leaf = same_block(2)
    merges = [same_block(2 * size) & jnp.logical_not(same_block(size)) for size in (2, 4, 8, 16, 32)]

    def split(x):
        return jnp.concatenate([jnp.where(even, x, 0.0), jnp.where(even, 0.0, x)], axis=0)

    def twice(x):
        return jnp.concatenate([x, x], axis=0)

    cum, m_incl, m_strict = [None, None], [None, None], [None, None]
    for d in range(2):
        sign = 1 - 2 * d
        cum[d] = (sign * (si - ti) <= 0).astype(BF16)
        order = sign * (c2 % C - r2 % C)
        m_incl[d] = same & (order <= 0)
        m_strict[d] = same & (order < 0)
    at, rt, bt, kt, bh, kh, pc, v = ([None] * len(streams) for _ in range(8))
    for q, (s, d) in enumerate(streams):
        r_ref, v_ref, kk_ref, kd_ref, b_ref, ld_ref = ins[d]
        ld = ld_ref[s]
        ld_hi, ld_mid, ld_lo = _split3(ld)
        cs = _dot(cum[d], ld_hi) + _dot(cum[d], ld_mid) + _dot(cum[d], ld_lo)
        last = cs[0:1, :] if d == 1 else cs[C - 1:C, :]
        pin = jnp.exp(cs)
        pinv = jnp.exp(-cs)
        at[q] = -kk_ref[s] * jnp.exp(cs - ld)
        rt[q] = r_ref[s] * pin
        bt[q] = b_ref[s] * pinv
        kt[q] = kd_ref[s] * pinv
        tail = jnp.exp(last - cs)
        bh[q] = b_ref[s] * tail
        kh[q] = kd_ref[s] * tail
        pc[q] = jnp.exp(last)
        v[q] = v_ref[s]
    m_incl = [m_incl[d] for _, d in streams]
    m_strict = [m_strict[d] for _, d in streams]

    s0 = [s_ref[q, p] for q, p in units]
    lhs = [jnp.concatenate([split(at[q][:, lanes[p]]), split(rt[q][:, lanes[p]])], axis=0).astype(BF16)
           for q, p in units]
    rhs = [jnp.concatenate([twice(bt[q][:, lanes[p]]), twice(kt[q][:, lanes[p]])], axis=0).astype(BF16)
           for q, p in units]
    v2b = [split(v[q][:, lanes[p]]).astype(BF16) for q, p in units]
    idx = range(len(units))
    g = [_dot_nt(lhs[u], rhs[u]) for u in idx]
    xs = [_dot_nt(lhs[u], s0[u].astype(BF16)) for u in idx]
    a_ab = [jnp.where(m_strict[units[u][0]], g[u][:P2, :P2], 0.0) for u in idx]
    inv = [eye + jnp.where(leaf, a_ab[u], 0.0) for u in idx]
    for mk in merges:
        invb = [inv[u].astype(BF16) for u in idx]
        binv = [_dot(jnp.where(mk, a_ab[u], 0.0).astype(BF16), invb[u]).astype(BF16) for u in idx]
        inv = [inv[u] + _dot(invb[u], binv[u]) for u in idx]
    a_ak = [jnp.where(m_strict[units[u][0]], g[u][:P2, P2:], 0.0).astype(BF16) for u in idx]
    rhs_u = [(xs[u][:P2] + _dot(a_ak[u], v2b[u])).astype(BF16) for u in idx]
    u2b = [_dot(inv[u].astype(BF16), rhs_u[u]).astype(BF16) for u in idx]
    m_r = [jnp.concatenate([jnp.where(m_incl[units[u][0]], g[u][P2:, :P2], 0.0),
                            jnp.where(m_incl[units[u][0]], g[u][P2:, P2:], 0.0)], axis=1).astype(BF16)
           for u in idx]
    uv = [jnp.concatenate([u2b[u], v2b[u]], axis=0) for u in idx]
    for u, (q, p) in enumerate(units):
        s, d = streams[q]
        y2 = xs[u][P2:] + _dot(m_r[u], uv[u])
        y_refs[d][s, :, lanes[p]] = y2[:C] + y2[C:]
    for u, (q, p) in enumerate(units):
        bk = jnp.concatenate([split(bh[q][:, lanes[p]]), split(kh[q][:, lanes[p]])], axis=0).astype(BF16)
        s_ref[q, p] = s0[u] * pc[q][:, lanes[p]] + _dot_tn(uv[u], bk)

    @pl.when(n == n_chunks - 1)
    def _():
        for q, (s, d) in enumerate(streams):
            for hd in range(HEADS):
                o = RWKV_N * (hd % 2)
                sf_ref[s, d, hd] = s_ref[q, hd // 2, o:o + RWKV_N, o:o + RWKV_N]


def _scan_call(r, v, kk, kd, b, ld, s0, y_bufs, n_seq, t_seq, row0, name):
    n_chunks = t_seq // CHUNK
    rows3 = (N_TOK // t_seq, t_seq, D)
    blk0 = row0 // t_seq // SCAN_SEQS
    chunk = [lambda nn: nn, lambda nn: n_chunks - 1 - nn]
    in_specs, args = [], []
    for d in range(2):
        row = pl.BlockSpec((SCAN_SEQS, CHUNK, D), lambda bb, nn, d=d: (blk0 + bb, chunk[d](nn), 0))
        row2 = pl.BlockSpec((None, SCAN_SEQS, CHUNK, D), lambda bb, nn, d=d: (d, blk0 + bb, chunk[d](nn), 0))
        in_specs += [row, row, row, row2, row2, row2]
        args += [r.reshape(rows3), v.reshape(rows3), kk.reshape(rows3),
                 kd.reshape(2, *rows3), b.reshape(2, *rows3), ld.reshape(2, *rows3)]
    st = pl.BlockSpec((SCAN_SEQS, 2, HEADS, RWKV_N, RWKV_N), lambda bb, nn: (bb, 0, 0, 0, 0))
    if s0 is not None:
        in_specs.append(st)
        args.append(s0)
    aliases = {}
    if y_bufs is not None:
        aliases = {len(args): 0, len(args) + 1: 1}
        in_specs += [pl.BlockSpec(memory_space=pl.ANY)] * 2
        args += [y.reshape(rows3) for y in y_bufs]
    out = pl.pallas_call(
        functools.partial(_scan_kernel, n_chunks, s0 is not None, y_bufs is not None),
        grid=(n_seq // SCAN_SEQS, n_chunks),
        in_specs=in_specs,
        out_specs=[
            pl.BlockSpec((SCAN_SEQS, CHUNK, D), lambda bb, nn: (blk0 + bb, chunk[0](nn), 0)),
            pl.BlockSpec((SCAN_SEQS, CHUNK, D), lambda bb, nn: (blk0 + bb, chunk[1](nn), 0)),
            st,
        ],
        out_shape=[
            jax.ShapeDtypeStruct(rows3, F32),
            jax.ShapeDtypeStruct(rows3, F32),
            jax.ShapeDtypeStruct((n_seq, 2, HEADS, RWKV_N, RWKV_N), F32),
        ],
        scratch_shapes=[pltpu.VMEM((2 * SCAN_SEQS, HEADS // 2, 2 * RWKV_N, 2 * RWKV_N), F32)],
        input_output_aliases=aliases,
        compiler_params=_cparams(("parallel", "arbitrary")),
        name=name,
    )(*args)
    return (out[0].reshape(N_TOK, D), out[1].reshape(N_TOK, D)), out[2]


def _pool_kernel(yp_ref, y_ref, yn_ref, m_ref, w_ref, sc_ref, o_ref):
    i = pl.program_id(0)
    has_prev, has_next, pos0, seq_len = _mix_tile_flags(i)
    h, ext = _ext_tile(yp_ref, y_ref, yn_ref, m_ref, has_prev, has_next)
    t = pos0 + lax.broadcasted_iota(jnp.int32, (TMIX, 1), 0)
    for gi, win in enumerate(POOL_WINDOWS):
        cols = slice(POOL_C * gi, POOL_C * (gi + 1))
        ext_g = ext[:, cols]
        acc = None
        for j in range(-(win // 2), win - win // 2):
            x = _shifted(ext_g, j)
            acc = x if acc is None else acc + x
        lo = jnp.maximum(t - win // 2, 0)
        hi = jnp.minimum(t - win // 2 + win, seq_len)
        cnt = (hi - lo).astype(F32)
        pooled = (acc / cnt - h[:, cols]).astype(BF16)
        o_ref[:, cols] = _dot(pooled, w_ref[gi]) * sc_ref[:, cols]


def _pool_call(y, mods, pool_w, pool_scale):
    return pl.pallas_call(
        _pool_kernel,
        grid=(N_TOK // TMIX,),
        in_specs=_halo_specs() + [
            pl.BlockSpec((None, SUBLANES, D), lambda i: (_cond_of_tile(i, TMIX), 0, 0)),
            pl.BlockSpec((4, POOL_C, POOL_C), lambda i: (0, 0, 0)),
            pl.BlockSpec((1, D), lambda i: (0, 0)),
        ],
        out_specs=pl.BlockSpec((TMIX, D), lambda i: (i, 0)),
        out_shape=jax.ShapeDtypeStruct((N_TOK, D), F32),
        compiler_params=_cparams(("parallel",)),
        name="pool",
    )(y, y, y, mods, pool_w, pool_scale)


def _rope_tables():
    rows = DEC_SEQ // GRID_W
    row = jnp.repeat(jnp.arange(rows, dtype=F32), GRID_W)
    col = jnp.tile(jnp.arange(GRID_W, dtype=F32), rows)
    n_freq = ROPE // 4
    inv_freq = ROPE_THETA ** (-jnp.arange(n_freq, dtype=F32) / n_freq)
    ang = jnp.stack([row[:, None] * inv_freq, col[:, None] * inv_freq], axis=1)
    cos, sin = jnp.cos(ang), jnp.sin(ang)
    cos32 = jnp.concatenate([cos, cos], axis=-1).reshape(DEC_SEQ, ROPE)
    sin32 = jnp.concatenate([-sin, sin], axis=-1).reshape(DEC_SEQ, ROPE)
    ones = jnp.ones((DEC_SEQ, NOPE), F32)
    zeros = jnp.zeros((DEC_SEQ, NOPE), F32)
    tail = HEAD_PAD - NOPE - ROPE
    cos_t = jnp.concatenate([ones, cos32, jnp.zeros((DEC_SEQ, tail), F32)], axis=1)
    sin_t = jnp.concatenate([zeros, sin32, jnp.zeros((DEC_SEQ, tail), F32)], axis=1)
    return cos_t, sin_t


def _swap_halves(w):
    s = w.reshape(*w.shape[:-1], 2, 2, ROPE // 4)
    return s[..., ::-1, :].reshape(w.shape)


def _pad_heads(w):
    k, _, width = w.shape
    return jnp.pad(w, ((0, 0), (0, 0), (0, HEAD_PAD - width))).reshape(k, HEADS * HEAD_PAD)


def _pad_heads_alternating(w):
    k = w.shape[0]
    pair = w.reshape(k, HEADS // 2, 2, VDIM)
    z = jnp.zeros((k, HEADS // 2, VDIM), w.dtype)
    return jnp.concatenate([pair[:, :, 0], z, z, pair[:, :, 1]], axis=-1).reshape(k, HEADS * HEAD_PAD)


def _mla_weights(wq_a, q_norm, wq_b, wkv_a, kv_norm, wkv_b, wo, cos_t, sin_t):
    wq_b3 = wq_b.reshape(Q_LORA, HEADS, NOPE + ROPE)
    wq_b4 = jnp.concatenate([wq_b3, _swap_halves(wq_b3[..., NOPE:])], axis=-1)
    w_pe = wkv_a[:, KV_LORA:]
    w_pe4 = jnp.concatenate([jnp.zeros((D, NOPE), F32), w_pe, _swap_halves(w_pe)], axis=1)
    wkv_b3 = wkv_b.reshape(KV_LORA, HEADS, NOPE + VDIM)
    return {
        "w_a": jnp.concatenate([wq_a, wkv_a[:, :KV_LORA], w_pe4], axis=1).astype(BF16),
        "q_norm": q_norm.reshape(1, Q_LORA),
        "wq_b": _pad_heads(wq_b4).astype(BF16),
        "kv_norm": kv_norm.reshape(1, KV_LORA),
        "wkv_b_k": _pad_heads(wkv_b3[..., :NOPE]).astype(BF16),
        "wkv_b_v": _pad_heads_alternating(wkv_b3[..., NOPE:]).astype(BF16),
        "v_one": jnp.asarray(_V_ONE),
        "wo": wo.astype(BF16),
        "cos": cos_t,
        "sin": sin_t,
    }


def _dir_pad(w):
    z = jnp.zeros_like(w[0])
    return jnp.stack([jnp.concatenate([w[0], z], axis=0), jnp.concatenate([z, w[1]], axis=0)])


def _rwkv_weights(mu, wr, wk, wv, w0, w1, w2, a0, a1, a2, g1, g2, k_k, k_a, r_k, lnx_g, lnx_b, wo):
    blk = np.arange(256) // RWKV_N
    return {
        "mu": mu,
        "wr": wr.astype(BF16), "wk": wk.astype(BF16), "wv": wv.astype(BF16),
        "g1": g1.astype(BF16), "g2": g2.astype(BF16),
        "w1": jnp.concatenate([w1[0], w1[1]], axis=1).astype(BF16),
        "w2": _dir_pad(w2).astype(BF16),
        "a1": jnp.concatenate([a1[0], a1[1]], axis=1).astype(BF16),
        "a2": _dir_pad(a2).astype(BF16),
        "w0": w0, "a0": a0,
        "k_k": k_k.reshape(1, D), "k_a": k_a.reshape(1, D),
        "r_k": r_k.reshape(1, D), "lnx_g": lnx_g.reshape(1, D), "lnx_b": lnx_b.reshape(1, D),
        "wo": wo.astype(BF16),
        "ones_bd": jnp.asarray(blk[:, None] == blk[None, :], BF16),
    }


def _mla_layer(y, mods, w, cache_ckv, cache_kpe):
    q, ckv, kpe = _mla_pre_call(y, mods, w)
    cache_kpe_pad = jnp.pad(cache_kpe, ((0, 0), (0, 0), (NOPE, HEAD_PAD - NOPE - ROPE)))
    t_k = PAST + DEC_SEQ
    k_all, v_all = _kv_expand_call(cache_ckv.reshape(DEC_BATCH * PAST, KV_LORA),
                                   cache_kpe_pad.reshape(DEC_BATCH * PAST, HEAD_PAD), ckv, kpe, w)
    o = _attn_call(q, k_all, v_all, None, BATCH, SEQ, SEQ, 0, DEC_BATCH * t_k, "attn_prompt")
    o = _attn_call(q, k_all, v_all, o, DEC_BATCH, DEC_SEQ, t_k, N_PROMPT, 0, "attn_sample")
    new_ckv = ckv[:N_PROMPT].reshape(BATCH, SEQ, KV_LORA)
    new_kpe = kpe[:N_PROMPT, NOPE:NOPE + ROPE].reshape(BATCH, SEQ, ROPE)
    return o, new_ckv, new_kpe


def _rwkv_layer(y, mods, w, state):
    r, v, g, kk, kd, b, ld, bonus = _rwkv_pre_call(y, mods, w)
    y_fb, s_p = _scan_call(r, v, kk, kd, b, ld, None, None, BATCH, SEQ, 0, "scan_prompt")
    y_fb, _ = _scan_call(r, v, kk, kd, b, ld, state, y_fb, DEC_BATCH, DEC_SEQ, N_PROMPT, "scan_sample")
    return (y_fb[0], y_fb[1], bonus, g, w["lnx_g"], w["lnx_b"], w["ones_bd"]), s_p


def kernel(x_prompt, x_sample, cache_ckv, cache_kpe, state_wkv, c, c_ctx, ada_w, ada_b, ln_g, ln_b, ffn_wg, ffn_wu, ffn_wd, mla_wq_a, mla_q_norm, mla_wq_b, mla_wkv_a, mla_kv_norm, mla_wkv_b, mla_wo, rwkv_mu, rwkv_wr, rwkv_wk, rwkv_wv, rwkv_w0, rwkv_w1, rwkv_w2, rwkv_a0, rwkv_a1, rwkv_a2, rwkv_g1, rwkv_g2, rwkv_k_k, rwkv_k_a, rwkv_r_k, rwkv_lnx_g, rwkv_lnx_b, rwkv_wo, pool_w, pool_scale):
    y = (x_prompt.reshape(N_PROMPT, D), x_sample.reshape(N_SAMPLE, D))
    cond8 = jnp.concatenate([c_ctx[None, :], c, jnp.zeros((SUBLANES - 1 - DEC_BATCH, D), F32)], axis=0)
    mods_all = _ada_call(cond8, ada_w, ada_b)
    mods_all = jnp.pad(mods_all.reshape(DEPTH, SUBLANES, 6, D)[:, :1 + DEC_BATCH],
                       ((0, 0), (0, 0), (0, SUBLANES - 6), (0, 0)))
    cos_t, sin_t = _rope_tables()
    ffn_wg_bf, ffn_wu_bf, ffn_wd_bf = ffn_wg.astype(BF16), ffn_wu.astype(BF16), ffn_wd.astype(BF16)
    new_ckv, new_kpe, new_wkv = [], [], []
    for layer in range(DEPTH):
        kind, j = layer % 3, layer // 3
        mods = mods_all[layer]
        wo = None
        if kind == 0:
            w = _mla_weights(mla_wq_a[j], mla_q_norm[j], mla_wq_b[j], mla_wkv_a[j], mla_kv_norm[j],
                             mla_wkv_b[j], mla_wo[j], cos_t, sin_t)
            a, ckv, kpe = _mla_layer(y, mods, w, cache_ckv[:, j], cache_kpe[:, j])
            new_ckv.append(ckv)
            new_kpe.append(kpe)
            wo = w["wo"]
        elif kind == 1:
            w = _rwkv_weights(rwkv_mu[j], rwkv_wr[j], rwkv_wk[j], rwkv_wv[j], rwkv_w0[j], rwkv_w1[j],
                              rwkv_w2[j], rwkv_a0[j], rwkv_a1[j], rwkv_a2[j], rwkv_g1[j], rwkv_g2[j],
                              rwkv_k_k[j], rwkv_k_a[j], rwkv_r_k[j], rwkv_lnx_g[j], rwkv_lnx_b[j], rwkv_wo[j])
            a, s_new = _rwkv_layer(y, mods, w, state_wkv[:, j])
            new_wkv.append(s_new)
            wo = w["wo"]
        else:
            a = _pool_call(y, mods, pool_w[j].astype(BF16), pool_scale[j].reshape(1, D))
        y = _post_call(y, a, mods, ln_g[layer], ln_b[layer], wo, ffn_wg_bf, ffn_wu_bf, ffn_wd_bf, layer,
                       split_out=layer == DEPTH - 1)
    return (y[0].reshape(BATCH, SEQ, D), y[1].reshape(DEC_BATCH, DEC_SEQ, D),
            jnp.stack(new_ckv, axis=1), jnp.stack(new_kpe, axis=1), jnp.stack(new_wkv, axis=1))
```

```python
import functools

import jax
import jax.numpy as jnp
import numpy as np
from jax import lax
from jax.experimental import pallas as pl
from jax.experimental.pallas import tpu as pltpu

F32 = jnp.float32
BF16 = jnp.bfloat16

D = 1024
BATCH, SEQ = 32, 256
DEC_BATCH, DEC_SEQ = 2, 4096
PAST = 512
DEPTH = 4
GRID_W = 64
HEADS = 16
Q_LORA, KV_LORA = 384, 256
NOPE, ROPE, VDIM = 64, 32, 64
ROPE_THETA = 10000.0
RWKV_N = 64
RWKV_GN_EPS = 64e-5
POOL_WINDOWS = (2, 4, 8, 16)
POOL_C = D // 4
D_FF = 2816
ALPHA = (2 * DEPTH) ** 0.25
LN_EPS = 1e-5
RMS_EPS = 1e-6
LOG2_E = 1.4426950408889634
EXP_M_HALF = 0.6065306597126334

N_PROMPT = BATCH * SEQ
N_SAMPLE = DEC_BATCH * DEC_SEQ
N_TOK = N_PROMPT + N_SAMPLE

LANES = 128
SUBLANES = 8
HEAD_PAD = 128
VMEM_LIMIT = 56 * 1024 * 1024

TM = 512
TMIX = 256
HALO = SUBLANES
TQ = 256
ATTN_HEADS = 4
ATTN_KC = 512
ATTN_SKEW = 2
ATTN_GATE = 3
CHUNK = 64
SCAN_SEQS = 2
TFF = 256
POST_TM = 512
POST_SUB = 512
POST_PRO_AT = 5
POST_EPI_AT = 2


def _cparams(sem):
    return pltpu.CompilerParams(dimension_semantics=sem, vmem_limit_bytes=VMEM_LIMIT)


def _dot(a, b):
    return jnp.dot(a, b, preferred_element_type=F32)


def _dot_nt(a, b):
    return lax.dot_general(a, b, (((1,), (1,)), ((), ())), preferred_element_type=F32)


def _dot_tn(a, b):
    return lax.dot_general(a, b, (((0,), (0,)), ((), ())), preferred_element_type=F32)


def _layer_norm(x, g, b):
    mu = jnp.mean(x, axis=-1, keepdims=True)
    xc = x - mu
    var = jnp.mean(xc * xc, axis=-1, keepdims=True)
    return xc * lax.rsqrt(var + LN_EPS) * g + b


def _rms_norm(x, g):
    return x * lax.rsqrt(jnp.mean(x * x, axis=-1, keepdims=True) + RMS_EPS) * g


def _sigmoid(x):
    return 0.5 * jnp.tanh(0.5 * x) + 0.5


def _silu(x):
    return x * _sigmoid(x)


def _cond_of_tile(i, tile):
    first_sample = N_PROMPT // tile
    per_seq = DEC_SEQ // tile
    return jnp.where(i < first_sample, 0, 1 + (i - first_sample) // per_seq)


def _split3(x):
    hi = x.astype(BF16)
    r1 = x - hi.astype(F32)
    mid = r1.astype(BF16)
    lo = (r1 - mid.astype(F32)).astype(BF16)
    return hi, mid, lo


def _seg_sum(x, ones_bd):
    outs = []
    for g in range(D // 256):
        hi, mid, lo = _split3(x[:, 256 * g:256 * (g + 1)])
        outs.append(_dot(hi, ones_bd) + _dot(mid, ones_bd) + _dot(lo, ones_bd))
    return jnp.concatenate(outs, axis=1)


def _ada_kernel(c_ref, w_ref, b_ref, o_ref):
    a = _silu(c_ref[...]).astype(BF16)
    o_ref[...] = _dot(a, w_ref[...].astype(BF16)) + b_ref[...]


def _ada_call(cond8, ada_w, ada_b):
    tn = 1536
    return pl.pallas_call(
        _ada_kernel,
        grid=(DEPTH, 6 * D // tn),
        in_specs=[
            pl.BlockSpec((SUBLANES, D), lambda l, n: (0, 0)),
            pl.BlockSpec((None, D, tn), lambda l, n: (l, 0, n)),
            pl.BlockSpec((None, 1, tn), lambda l, n: (l, 0, n)),
        ],
        out_specs=pl.BlockSpec((None, SUBLANES, tn), lambda l, n: (l, 0, n)),
        out_shape=jax.ShapeDtypeStruct((DEPTH, SUBLANES, 6 * D), F32),
        compiler_params=_cparams(("parallel", "parallel")),
        name="adaln",
    )(cond8, ada_w, ada_b.reshape(DEPTH, 1, 6 * D))


def _act_specs(y, tile):
    if not isinstance(y, tuple):
        return [pl.BlockSpec((tile, D), lambda i: (i, 0))], [y]
    n_p = N_PROMPT // tile
    return [pl.BlockSpec((tile, D), lambda i: (jnp.minimum(i, n_p - 1), 0)),
            pl.BlockSpec((tile, D), lambda i: (jnp.maximum(i - n_p, 0), 0))], list(y)


def _act_tile(y_refs, tile):
    if len(y_refs) == 1:
        return y_refs[0][...]
    return jnp.where(pl.program_id(0) < N_PROMPT // tile, y_refs[0][...], y_refs[1][...])


def _mla_pre_kernel(n_y, *refs):
    m_ref, wa_ref, qn_ref, wqb_ref, kvn_ref, cos_ref, sin_ref, q_ref, ckv_ref, kpe_ref = refs[n_y:]
    i = pl.program_id(0)
    h = (_act_tile(refs[:n_y], TM) * (1.0 + m_ref[1:2, :]) + m_ref[0:1, :]).astype(BF16)
    x = _dot(h, wa_ref[...])
    qa = _rms_norm(x[:, :Q_LORA], qn_ref[...]).astype(BF16)
    ckv_ref[...] = _rms_norm(x[:, Q_LORA:Q_LORA + KV_LORA], kvn_ref[...])
    kpe = x[:, Q_LORA + KV_LORA:]
    q = _dot(qa, wqb_ref[...])
    is_latent = i >= N_PROMPT // TM
    swap_shift = HEAD_PAD - ROPE

    @pl.when(jnp.logical_not(is_latent))
    def _():
        q_ref[...] = q.astype(BF16)
        lane = lax.broadcasted_iota(jnp.int32, kpe.shape, 1)
        kpe_ref[...] = jnp.where(lane < NOPE + ROPE, kpe, 0.0)

    @pl.when(is_latent)
    def _():
        cos = cos_ref[...]
        sin = sin_ref[...]
        kpe_ref[...] = kpe * cos + pltpu.roll(kpe, swap_shift, 1) * sin
        for hd in range(HEADS):
            sl = slice(HEAD_PAD * hd, HEAD_PAD * (hd + 1))
            q_ref[:, sl] = (q[:, sl] * cos + pltpu.roll(q[:, sl], swap_shift, 1) * sin).astype(BF16)


def _mla_pre_call(y, mods, w):
    first_sample = N_PROMPT // TM
    per_seq = DEC_SEQ // TM
    const = lambda i: (0, 0)
    rope_map = lambda i: (jnp.where(i < first_sample, 0, (i - first_sample) % per_seq), 0)
    y_specs, y_args = _act_specs(y, TM)
    return pl.pallas_call(
        functools.partial(_mla_pre_kernel, len(y_args)),
        grid=(N_TOK // TM,),
        in_specs=y_specs + [
            pl.BlockSpec((None, SUBLANES, D), lambda i: (_cond_of_tile(i, TM), 0, 0)),
            pl.BlockSpec((D, Q_LORA + KV_LORA + HEAD_PAD), const),
            pl.BlockSpec((1, Q_LORA), const),
            pl.BlockSpec((Q_LORA, HEADS * HEAD_PAD), const),
            pl.BlockSpec((1, KV_LORA), const),
            pl.BlockSpec((TM, HEAD_PAD), rope_map),
            pl.BlockSpec((TM, HEAD_PAD), rope_map),
        ],
        out_specs=[
            pl.BlockSpec((TM, HEADS * HEAD_PAD), lambda i: (i, 0)),
            pl.BlockSpec((TM, KV_LORA), lambda i: (i, 0)),
            pl.BlockSpec((TM, HEAD_PAD), lambda i: (i, 0)),
        ],
        out_shape=[
            jax.ShapeDtypeStruct((N_TOK, HEADS * HEAD_PAD), BF16),
            jax.ShapeDtypeStruct((N_TOK, KV_LORA), F32),
            jax.ShapeDtypeStruct((N_TOK, HEAD_PAD), F32),
        ],
        compiler_params=_cparams(("parallel",)),
        name="mla_pre",
    )(*y_args, mods, w["w_a"], w["q_norm"], w["wq_b"], w["kv_norm"], w["cos"], w["sin"])


KV_TILES_PER_SEQ = (PAST + DEC_SEQ) // TM
KV_ROWS = DEC_BATCH * (PAST + DEC_SEQ) + N_PROMPT


def _kv_tile_source(i):
    n_lat = DEC_BATCH * KV_TILES_PER_SEQ
    b = jnp.minimum(i // KV_TILES_PER_SEQ, DEC_BATCH - 1)
    j = i - b * KV_TILES_PER_SEQ
    is_cache = (i < n_lat) & (j < PAST // TM)
    new_tile = N_PROMPT // TM + b * (DEC_SEQ // TM) + jnp.maximum(j - PAST // TM, 0)
    return is_cache, b * (PAST // TM) + jnp.minimum(j, PAST // TM - 1), jnp.where(i < n_lat, new_tile, i - n_lat)


def _kv_expand_kernel(cckv_ref, ckpe_ref, ckv_ref, kpe_ref, wk_ref, wv_ref, vone_ref, k_ref, v_ref):
    is_cache, _, _ = _kv_tile_source(pl.program_id(0))
    c = jnp.where(is_cache, cckv_ref[...], ckv_ref[...]).astype(BF16)
    kpe = jnp.where(is_cache, ckpe_ref[...], kpe_ref[...])
    kn = _dot(c, wk_ref[...])
    for hd in range(HEADS):
        sl = slice(HEAD_PAD * hd, HEAD_PAD * (hd + 1))
        k_ref[:, sl] = (kn[:, sl] + kpe).astype(BF16)
    v_ref[...] = (_dot(c, wv_ref[...]) + vone_ref[...]).astype(BF16)


def _kv_expand_call(cache_ckv, cache_kpe, ckv, kpe, w):
    rows = KV_ROWS
    const = lambda i: (0, 0)
    return pl.pallas_call(
        _kv_expand_kernel,
        grid=(rows // TM,),
        in_specs=[
            pl.BlockSpec((TM, KV_LORA), lambda i: (_kv_tile_source(i)[1], 0)),
            pl.BlockSpec((TM, HEAD_PAD), lambda i: (_kv_tile_source(i)[1], 0)),
            pl.BlockSpec((TM, KV_LORA), lambda i: (_kv_tile_source(i)[2], 0)),
            pl.BlockSpec((TM, HEAD_PAD), lambda i: (_kv_tile_source(i)[2], 0)),
            pl.BlockSpec((KV_LORA, HEADS * HEAD_PAD), const),
            pl.BlockSpec((KV_LORA, HEADS * HEAD_PAD), const),
            pl.BlockSpec((1, HEADS * HEAD_PAD), const),
        ],
        out_specs=[
            pl.BlockSpec((TM, HEADS * HEAD_PAD), lambda i: (i, 0)),
            pl.BlockSpec((TM, HEADS * HEAD_PAD), lambda i: (i, 0)),
        ],
        out_shape=[jax.ShapeDtypeStruct((rows, HEADS * HEAD_PAD), BF16)] * 2,
        compiler_params=_cparams(("parallel",)),
        name="kv_expand",
    )(cache_ckv, cache_kpe, ckv, kpe, w["wkv_b_k"], w["wkv_b_v"], w["v_one"])


def _sum_lane(h):
    return HEAD_PAD - 1 if h % 2 == 0 else 0


_V_ONE = np.zeros((1, HEADS * HEAD_PAD), np.float32)
_V_ONE[0, [HEAD_PAD * h + _sum_lane(h) for h in range(HEADS)]] = 1.0


def _zero_like_col(x):
    bits = pltpu.bitcast(x, jnp.uint32)
    zero = lax.shift_right_logical(lax.shift_right_logical(bits, jnp.uint32(16)), jnp.uint32(16))
    return pltpu.bitcast(zero, F32)[:, 0:1]


def _attn_kernel(t_k, q_ref, k_ref, v_ref, *rest):
    o_ref = rest[-1]
    scale2 = (NOPE + ROPE) ** -0.5 * LOG2_E
    kc = min(ATTN_KC, t_k)
    chunks = [slice(c * kc, (c + 1) * kc) for c in range(t_k // kc)]
    n_heads = q_ref.shape[1] // HEAD_PAD
    head = [slice(HEAD_PAD * e, HEAD_PAD * (e + 1)) for e in range(n_heads)]

    def scores(e, c):
        return _dot_nt(q_ref[:, head[e]], k_ref[c, head[e]])

    lane = lax.broadcasted_iota(jnp.int32, (TQ, HEAD_PAD), 1)

    def normalised(e, acc):
        sl = _sum_lane(e)
        return jnp.where(lane == sl, 0.0, acc) * (1.0 / acc[:, sl:sl + 1])

    def store(outs):
        for j in range(n_heads // 2):
            o_ref[:, 2 * VDIM * j:2 * VDIM * (j + 1)] = (outs[2 * j] + outs[2 * j + 1]).astype(BF16)

    if len(chunks) == 1:
        s_all = [scores(e, chunks[0]) for e in range(n_heads)]
        p_all = [jnp.exp2((s - jnp.max(s, axis=-1, keepdims=True)) * scale2).astype(BF16) for s in s_all]
        store([normalised(e, _dot(p_all[e], v_ref[:, head[e]])) for e in range(n_heads)])
        return

    s_cur = [scores(0, c) for c in chunks]
    outs = []
    for e in range(n_heads):
        m = functools.reduce(jnp.maximum, [jnp.max(s, axis=-1, keepdims=True) for s in s_cur])
        s_next, accs = [], []
        for i in range(len(chunks) + ATTN_SKEW):
            if e + 1 < n_heads and i < len(chunks):
                s_next.append(scores(e + 1, chunks[i]))
            if i >= ATTN_SKEW:
                j = i - ATTN_SKEW
                mj = m + _zero_like_col(accs[j - ATTN_GATE]) if j >= ATTN_GATE else m
                p = jnp.exp2((s_cur[j] - mj) * scale2).astype(BF16)
                pv = _dot(p, v_ref[chunks[j], head[e]])
                accs.append(pv if not accs else accs[-1] + pv)
        outs.append(normalised(e, accs[-1]))
        s_cur = s_next
    store(outs)


def _attn_call(q, k, v, o_buf, n_seq, t_q, t_k, q_row0, k_row0, name):
    nq = t_q // TQ
    q_blk0 = q_row0 // TQ
    k_blk0 = k_row0 // t_k
    heads = ATTN_HEADS if t_k > ATTN_KC else HEADS
    width = heads * HEAD_PAD
    in_specs = [
        pl.BlockSpec((TQ, width), lambda b, p, i: (q_blk0 + b * nq + i, p)),
        pl.BlockSpec((t_k, width), lambda b, p, i: (k_blk0 + b, p)),
        pl.BlockSpec((t_k, width), lambda b, p, i: (k_blk0 + b, p)),
    ]
    args = [q, k, v]
    if o_buf is not None:
        in_specs.append(pl.BlockSpec(memory_space=pl.ANY))
        args.append(o_buf)
    return pl.pallas_call(
        functools.partial(_attn_kernel, t_k),
        grid=(n_seq, HEADS // heads, nq),
        in_specs=in_specs,
        out_specs=pl.BlockSpec((TQ, heads * VDIM), lambda b, p, i: (q_blk0 + b * nq + i, p)),
        out_shape=jax.ShapeDtypeStruct((N_TOK, HEADS * VDIM), BF16),
        input_output_aliases={} if o_buf is None else {3: 0},
        compiler_params=_cparams(("parallel", "parallel", "arbitrary")),
        name=name,
    )(*args)


def _rwkv_gate(rows, yf_ref, yb_ref, bonus_ref, g_ref, lg_ref, lb_ref, ones_ref):
    ones = ones_ref[...]
    y = yf_ref[rows, :] + yb_ref[rows, :]
    mu = _seg_sum(y, ones) * (1.0 / RWKV_N)
    yc = y - mu
    var = _seg_sum(yc * yc, ones) * (1.0 / RWKV_N)
    yn = yc * lax.rsqrt(var + RWKV_GN_EPS) * lg_ref[...] + lb_ref[...]
    return ((yn + bonus_ref[rows, :]) * g_ref[rows, :]).astype(BF16)


def _post_kernel(tile, n_y, n_mix, has_proj, split_out, *refs):
    y_refs = refs[:n_y]
    mix_refs = refs[n_y:n_y + n_mix]
    m_ref, lng_ref, lnb_ref = refs[n_y + n_mix:n_y + n_mix + 3]
    rest = refs[n_y + n_mix + 3:]
    if has_proj:
        wo_ref, rest = rest[0], rest[1:]
    wg_ref, wu_ref, wd_ref = rest[:3]
    o_refs = rest[3:]
    is_prompt = pl.program_id(0) < N_PROMPT // tile
    blocks = [slice(POST_SUB * s, POST_SUB * (s + 1)) for s in range(tile // POST_SUB)]
    n_chunks = D_FF // TFF

    def prologue(rows):
        a = mix_refs[0][rows, :] if n_mix == 1 else _rwkv_gate(rows, *mix_refs)
        mix = _dot(a, wo_ref[...]) if has_proj else a
        y = y_refs[0][rows, :] if n_y == 1 else jnp.where(is_prompt, y_refs[0][rows, :], y_refs[1][rows, :])
        y1 = _layer_norm(ALPHA * y + m_ref[2:3, :] * mix, lng_ref[0:1, :], lnb_ref[0:1, :])
        return y1, (y1 * (1.0 + m_ref[4:5, :]) + m_ref[3:4, :]).astype(BF16)

    def ffn_chunk(h, j):
        cols = slice(TFF * j, TFF * (j + 1))
        act = (_silu(_dot(h, wg_ref[:, cols].astype(BF16))) * _dot(h, wu_ref[:, cols].astype(BF16))).astype(BF16)
        return _dot(act, wd_ref[cols, :].astype(BF16))

    def epilogue(rows, y1, acc):
        out = _layer_norm(ALPHA * y1 + m_ref[5:6, :] * acc, lng_ref[1:2, :], lnb_ref[1:2, :])
        if not split_out:
            o_refs[0][rows, :] = out
        else:
            @pl.when(is_prompt)
            def _():
                o_refs[0][rows, :] = out

            @pl.when(jnp.logical_not(is_prompt))
            def _():
                o_refs[1][rows, :] = out

    state = [None] * len(blocks)
    state[0] = prologue(blocks[0])
    pending = None
    for s, rows in enumerate(blocks):
        y1, h = state[s]
        acc = None
        for j in range(n_chunks):
            part = ffn_chunk(h, j)
            acc = part if acc is None else acc + part
            if j == POST_PRO_AT and s + 1 < len(blocks):
                state[s + 1] = prologue(blocks[s + 1])
            if j == POST_EPI_AT and pending is not None:
                epilogue(*pending)
                pending = None
        pending = (rows, y1, acc)
    epilogue(*pending)


def _post_call(y, a, mods, ln_g, ln_b, wo, wg, wu, wd, layer, split_out=False):
    has_proj = wo is not None
    const = lambda i: (0, 0)
    this_layer = lambda i: (layer, 0, 0)
    resident = pl.Buffered(1)
    tile = POST_SUB if isinstance(a, tuple) else POST_TM
    y_specs, y_args = _act_specs(y, tile)
    row = pl.BlockSpec((tile, D), lambda i: (i, 0))
    if isinstance(a, tuple):
        mix_specs = [row] * 4 + [pl.BlockSpec((1, D), const)] * 2 + [pl.BlockSpec((256, 256), const)]
        mix_args = list(a)
    else:
        mix_specs, mix_args = [row], [a]
    in_specs = y_specs + mix_specs + [
        pl.BlockSpec((None, SUBLANES, D), lambda i: (_cond_of_tile(i, tile), 0, 0)),
        pl.BlockSpec((2, D), const),
        pl.BlockSpec((2, D), const),
    ]
    args = y_args + mix_args + [mods, ln_g, ln_b]
    if has_proj:
        in_specs.append(pl.BlockSpec((D, D), const, pipeline_mode=resident))
        args.append(wo)
    in_specs += [
        pl.BlockSpec((None, D, D_FF), this_layer, pipeline_mode=resident),
        pl.BlockSpec((None, D, D_FF), this_layer, pipeline_mode=resident),
        pl.BlockSpec((None, D_FF, D), this_layer, pipeline_mode=resident),
    ]
    args += [wg, wu, wd]
    if split_out:
        n_p = N_PROMPT // tile
        out_specs = [pl.BlockSpec((tile, D), lambda i: (jnp.minimum(i, n_p - 1), 0)),
                     pl.BlockSpec((tile, D), lambda i: (jnp.maximum(i - n_p, 0), 0))]
        out_shape = [jax.ShapeDtypeStruct((N_PROMPT, D), F32), jax.ShapeDtypeStruct((N_SAMPLE, D), F32)]
    else:
        out_specs = pl.BlockSpec((tile, D), lambda i: (i, 0))
        out_shape = jax.ShapeDtypeStruct((N_TOK, D), F32)
    return pl.pallas_call(
        functools.partial(_post_kernel, tile, len(y_args), len(mix_args), has_proj, split_out),
        grid=(N_TOK // tile,),
        in_specs=in_specs,
        out_specs=out_specs,
        out_shape=out_shape,
        compiler_params=_cparams(("arbitrary",) if split_out else ("parallel",)),
        name="post_proj" if has_proj else "post_noproj",
    )(*args)


def _mix_tile_flags(i):
    first_sample = N_PROMPT // TMIX
    tiles_prompt = SEQ // TMIX
    tiles_sample = DEC_SEQ // TMIX
    is_p = i < first_sample
    j = jnp.where(is_p, i % tiles_prompt, (i - first_sample) % tiles_sample)
    n = jnp.where(is_p, tiles_prompt, tiles_sample)
    return j > 0, j < n - 1, j * TMIX, n * TMIX


def _halo_specs():
    blocks_per_tile = TMIX // HALO
    last = N_TOK // HALO - 1
    return [
        pl.BlockSpec((HALO, D), lambda i: (jnp.maximum(i * blocks_per_tile - 1, 0), 0)),
        pl.BlockSpec((TMIX, D), lambda i: (i, 0)),
        pl.BlockSpec((HALO, D), lambda i: (jnp.minimum((i + 1) * blocks_per_tile, last), 0)),
    ]


EXT = TMIX + 2 * HALO


def _ext_tile(yp_ref, y_ref, yn_ref, m_ref, has_prev, has_next):
    scale1 = 1.0 + m_ref[1:2, :]
    shift = m_ref[0:1, :]
    h = y_ref[...] * scale1 + shift
    before = jnp.where(has_prev, yp_ref[...] * scale1 + shift, 0.0)
    after = jnp.where(has_next, yn_ref[...] * scale1 + shift, 0.0)
    return h, jnp.concatenate([before, h, after], axis=0)


def _shifted(ext, j):
    if j == 0:
        return ext[HALO:HALO + TMIX]
    return pltpu.roll(ext, (-j) % EXT, 0)[HALO:HALO + TMIX]


def _rwkv_pre_kernel(yp_ref, y_ref, yn_ref, m_ref, mu_ref, wr_ref, wk_ref, wv_ref, g1_ref, g2_ref,
                     w1_ref, w2_ref, a1_ref, a2_ref, w0_ref, a0_ref, kk_ref, ka_ref, rk_ref, ones_ref,
                     r_out, v_out, g_out, kk_out, kd_out, b_out, ld_out, bonus_out):
    i = pl.program_id(0)
    has_prev, has_next, _, _ = _mix_tile_flags(i)
    h, ext = _ext_tile(yp_ref, y_ref, yn_ref, m_ref, has_prev, has_next)
    xx = 0.5 * (_shifted(ext, -1) + _shifted(ext, 1)) - h

    def mix(j):
        return (h + xx * mu_ref[j:j + 1, :]).astype(BF16)

    r = _dot(mix(0), wr_ref[...])
    r_out[...] = r
    k = _dot(mix(2), wk_ref[...])
    v = _dot(mix(3), wv_ref[...])
    v_out[...] = v.astype(BF16)
    g_out[...] = _dot(_sigmoid(_dot(mix(5), g1_ref[...])).astype(BF16), g2_ref[...])
    tw = jnp.tanh(_dot(mix(1), w1_ref[...])).astype(BF16)
    ta = _dot(mix(4), a1_ref[...]).astype(BF16)

    kkf = k * kk_ref[...]
    kk = kkf * jnp.minimum(lax.rsqrt(_seg_sum(kkf * kkf, ones_ref[...])), 1e12)
    kk_out[...] = kk
    kd_sum = None
    for d in range(2):
        ld_out[d] = -EXP_M_HALF * _sigmoid(w0_ref[d:d + 1, :] + _dot(tw, w2_ref[d]))
        a = _sigmoid(a0_ref[d:d + 1, :] + _dot(ta, a2_ref[d]))
        kd = k * (1.0 + (a - 1.0) * ka_ref[...])
        kd_out[d] = kd
        b_out[d] = kk * a
        kd_sum = kd if kd_sum is None else kd_sum + kd
    bonus_out[...] = _seg_sum(r * kd_sum * rk_ref[...], ones_ref[...]) * v


def _rwkv_pre_call(y, mods, w):
    const2 = lambda i: (0, 0)
    const3 = lambda i: (0, 0, 0)
    row = pl.BlockSpec((TMIX, D), lambda i: (i, 0))
    row2 = pl.BlockSpec((2, TMIX, D), lambda i: (0, i, 0))
    one = jax.ShapeDtypeStruct((N_TOK, D), F32)
    two = jax.ShapeDtypeStruct((2, N_TOK, D), F32)
    return pl.pallas_call(
        _rwkv_pre_kernel,
        grid=(N_TOK // TMIX,),
        in_specs=_halo_specs() + [
            pl.BlockSpec((None, SUBLANES, D), lambda i: (_cond_of_tile(i, TMIX), 0, 0)),
            pl.BlockSpec((6, D), const2),
            pl.BlockSpec((D, D), const2), pl.BlockSpec((D, D), const2), pl.BlockSpec((D, D), const2),
            pl.BlockSpec((D, 128), const2), pl.BlockSpec((128, D), const2),
            pl.BlockSpec((D, 128), const2), pl.BlockSpec((2, 128, D), const3),
            pl.BlockSpec((D, 128), const2), pl.BlockSpec((2, 128, D), const3),
            pl.BlockSpec((2, D), const2), pl.BlockSpec((2, D), const2),
            pl.BlockSpec((1, D), const2), pl.BlockSpec((1, D), const2), pl.BlockSpec((1, D), const2),
            pl.BlockSpec((256, 256), const2),
        ],
        out_specs=[row, row, row, row, row2, row2, row2, row],
        out_shape=[one, jax.ShapeDtypeStruct((N_TOK, D), BF16), one, one, two, two, two, one],
        compiler_params=_cparams(("parallel",)),
        name="rwkv_pre",
    )(y, y, y, mods, w["mu"], w["wr"], w["wk"], w["wv"], w["g1"], w["g2"], w["w1"], w["w2"],
      w["a1"], w["a2"], w["w0"], w["a0"], w["k_k"], w["k_a"], w["r_k"], w["ones_bd"])


def _scan_kernel(n_chunks, has_s0, aliased, *refs):
    ins = [refs[0:6], refs[6:12]]
    pos = 12
    s0_ref = refs[pos] if has_s0 else None
    pos += (1 if has_s0 else 0) + (2 if aliased else 0)
    y_refs = refs[pos:pos + 2]
    sf_ref, s_ref = refs[pos + 2], refs[pos + 3]
    n = pl.program_id(1)
    C = CHUNK
    P2 = 2 * C
    streams = [(s, d) for s in range(SCAN_SEQS) for d in range(2)]
    units = [(q, p) for q in range(len(streams)) for p in range(HEADS // 2)]
    lanes = [slice(2 * RWKV_N * p, 2 * RWKV_N * (p + 1)) for p in range(HEADS // 2)]

    @pl.when(n == 0)
    def _():
        s_ref[...] = jnp.zeros_like(s_ref)
        if has_s0:
            for q, (s, d) in enumerate(streams):
                for hd in range(HEADS):
                    o = RWKV_N * (hd % 2)
                    s_ref[q, hd // 2, o:o + RWKV_N, o:o + RWKV_N] = s0_ref[s, d, hd]

    lane = lax.broadcasted_iota(jnp.int32, (C, 2 * RWKV_N), 1)
    even = lane < RWKV_N
    ti = lax.broadcasted_iota(jnp.int32, (C, C), 0)
    si = lax.broadcasted_iota(jnp.int32, (C, C), 1)
    r2 = lax.broadcasted_iota(jnp.int32, (P2, P2), 0)
    c2 = lax.broadcasted_iota(jnp.int32, (P2, P2), 1)
    same = (r2 // C) == (c2 // C)
    eye = (r2 == c2).astype(F32)

    def same_block(size):
        return (r2 // size) == (c2 // size)

    leaf = same_block(2)
    merges = [same_block(2 * size) & jnp.logical_not(same_block(size)) for size in (2, 4, 8, 16, 32)]

    def split(x):
        return jnp.concatenate([jnp.where(even, x, 0.0), jnp.where(even, 0.0, x)], axis=0)

    def twice(x):
        return jnp.concatenate([x, x], axis=0)

    cum, m_incl, m_strict = [None, None], [None, None], [None, None]
    for d in range(2):
        sign = 1 - 2 * d
        cum[d] = (sign * (si - ti) <= 0).astype(BF16)
        order = sign * (c2 % C - r2 % C)
        m_incl[d] = same & (order <= 0)
        m_strict[d] = same & (order < 0)
    at, rt, bt, kt, bh, kh, pc, v = ([None] * len(streams) for _ in range(8))
    for q, (s, d) in enumerate(streams):
        r_ref, v_ref, kk_ref, kd_ref, b_ref, ld_ref = ins[d]
        ld = ld_ref[s]
        ld_hi, ld_mid, ld_lo = _split3(ld)
        cs = _dot(cum[d], ld_hi) + _dot(cum[d], ld_mid) + _dot(cum[d], ld_lo)
        last = cs[0:1, :] if d == 1 else cs[C - 1:C, :]
        pin = jnp.exp(cs)
        pinv = jnp.exp(-cs)
        at[q] = -kk_ref[s] * jnp.exp(cs - ld)
        rt[q] = r_ref[s] * pin
        bt[q] = b_ref[s] * pinv
        kt[q] = kd_ref[s] * pinv
        pc[q] = jnp.exp(last)
        tail = pc[q] * pinv
        bh[q] = b_ref[s] * tail
        kh[q] = kd_ref[s] * tail
        v[q] = v_ref[s]
    m_incl = [m_incl[d] for _, d in streams]
    m_strict = [m_strict[d] for _, d in streams]

    s0 = [s_ref[q, p] for q, p in units]
    lhs = [jnp.concatenate([split(at[q][:, lanes[p]]), split(rt[q][:, lanes[p]])], axis=0).astype(BF16)
           for q, p in units]
    rhs = [jnp.concatenate([twice(bt[q][:, lanes[p]]), twice(kt[q][:, lanes[p]])], axis=0).astype(BF16)
           for q, p in units]
    v2b = [split(v[q][:, lanes[p]]).astype(BF16) for q, p in units]
    idx = range(len(units))
    g = [_dot_nt(lhs[u], rhs[u]) for u in idx]
    xs = [_dot_nt(lhs[u], s0[u].astype(BF16)) for u in idx]
    a_ab = [jnp.where(m_strict[units[u][0]], g[u][:P2, :P2], 0.0) for u in idx]
    inv = [eye + jnp.where(leaf, a_ab[u], 0.0) for u in idx]
    for mk in merges:
        invb = [inv[u].astype(BF16) for u in idx]
        binv = [_dot(jnp.where(mk, a_ab[u], 0.0).astype(BF16), invb[u]).astype(BF16) for u in idx]
        inv = [inv[u] + _dot(invb[u], binv[u]) for u in idx]
    a_ak = [jnp.where(m_strict[units[u][0]], g[u][:P2, P2:], 0.0).astype(BF16) for u in idx]
    rhs_u = [(xs[u][:P2] + _dot(a_ak[u], v2b[u])).astype(BF16) for u in idx]
    u2b = [_dot(inv[u].astype(BF16), rhs_u[u]).astype(BF16) for u in idx]
    m_r = [jnp.concatenate([jnp.where(m_incl[units[u][0]], g[u][P2:, :P2], 0.0),
                            jnp.where(m_incl[units[u][0]], g[u][P2:, P2:], 0.0)], axis=1).astype(BF16)
           for u in idx]
    uv = [jnp.concatenate([u2b[u], v2b[u]], axis=0) for u in idx]
    for u, (q, p) in enumerate(units):
        s, d = streams[q]
        y2 = xs[u][P2:] + _dot(m_r[u], uv[u])
        y_refs[d][s, :, lanes[p]] = y2[:C] + y2[C:]
    for u, (q, p) in enumerate(units):
        bk = jnp.concatenate([split(bh[q][:, lanes[p]]), split(kh[q][:, lanes[p]])], axis=0).astype(BF16)
        s_ref[q, p] = s0[u] * pc[q][:, lanes[p]] + _dot_tn(uv[u], bk)

    @pl.when(n == n_chunks - 1)
    def _():
        for q, (s, d) in enumerate(streams):
            for hd in range(HEADS):
                o = RWKV_N * (hd % 2)
                sf_ref[s, d, hd] = s_ref[q, hd // 2, o:o + RWKV_N, o:o + RWKV_N]


def _scan_call(r, v, kk, kd, b, ld, s0, y_bufs, n_seq, t_seq, row0, name):
    n_chunks = t_seq // CHUNK
    rows3 = (N_TOK // t_seq, t_seq, D)
    blk0 = row0 // t_seq // SCAN_SEQS
    chunk = [lambda nn: nn, lambda nn: n_chunks - 1 - nn]
    in_specs, args = [], []
    for d in range(2):
        row = pl.BlockSpec((SCAN_SEQS, CHUNK, D), lambda bb, nn, d=d: (blk0 + bb, chunk[d](nn), 0))
        row2 = pl.BlockSpec((None, SCAN_SEQS, CHUNK, D), lambda bb, nn, d=d: (d, blk0 + bb, chunk[d](nn), 0))
        in_specs += [row, row, row, row2, row2, row2]
        args += [r.reshape(rows3), v.reshape(rows3), kk.reshape(rows3),
                 kd.reshape(2, *rows3), b.reshape(2, *rows3), ld.reshape(2, *rows3)]
    st = pl.BlockSpec((SCAN_SEQS, 2, HEADS, RWKV_N, RWKV_N), lambda bb, nn: (bb, 0, 0, 0, 0))
    if s0 is not None:
        in_specs.append(st)
        args.append(s0)
    aliases = {}
    if y_bufs is not None:
        aliases = {len(args): 0, len(args) + 1: 1}
        in_specs += [pl.BlockSpec(memory_space=pl.ANY)] * 2
        args += [y.reshape(rows3) for y in y_bufs]
    out = pl.pallas_call(
        functools.partial(_scan_kernel, n_chunks, s0 is not None, y_bufs is not None),
        grid=(n_seq // SCAN_SEQS, n_chunks),
        in_specs=in_specs,
        out_specs=[
            pl.BlockSpec((SCAN_SEQS, CHUNK, D), lambda bb, nn: (blk0 + bb, chunk[0](nn), 0)),
            pl.BlockSpec((SCAN_SEQS, CHUNK, D), lambda bb, nn: (blk0 + bb, chunk[1](nn), 0)),
            st,
        ],
        out_shape=[
            jax.ShapeDtypeStruct(rows3, F32),
            jax.ShapeDtypeStruct(rows3, F32),
            jax.ShapeDtypeStruct((n_seq, 2, HEADS, RWKV_N, RWKV_N), F32),
        ],
        scratch_shapes=[pltpu.VMEM((2 * SCAN_SEQS, HEADS // 2, 2 * RWKV_N, 2 * RWKV_N), F32)],
        input_output_aliases=aliases,
        compiler_params=_cparams(("parallel", "arbitrary")),
        name=name,
    )(*args)
    return (out[0].reshape(N_TOK, D), out[1].reshape(N_TOK, D)), out[2]


def _pool_kernel(yp_ref, y_ref, yn_ref, m_ref, w_ref, sc_ref, o_ref):
    i = pl.program_id(0)
    has_prev, has_next, pos0, seq_len = _mix_tile_flags(i)
    h, ext = _ext_tile(yp_ref, y_ref, yn_ref, m_ref, has_prev, has_next)
    t = pos0 + lax.broadcasted_iota(jnp.int32, (TMIX, 1), 0)
    for gi, win in enumerate(POOL_WINDOWS):
        cols = slice(POOL_C * gi, POOL_C * (gi + 1))
        ext_g = ext[:, cols]
        acc = None
        for j in range(-(win // 2), win - win // 2):
            x = _shifted(ext_g, j)
            acc = x if acc is None else acc + x
        lo = jnp.maximum(t - win // 2, 0)
        hi = jnp.minimum(t - win // 2 + win, seq_len)
        cnt = (hi - lo).astype(F32)
        pooled = (acc / cnt - h[:, cols]).astype(BF16)
        o_ref[:, cols] = _dot(pooled, w_ref[gi]) * sc_ref[:, cols]


def _pool_call(y, mods, pool_w, pool_scale):
    return pl.pallas_call(
        _pool_kernel,
        grid=(N_TOK // TMIX,),
        in_specs=_halo_specs() + [
            pl.BlockSpec((None, SUBLANES, D), lambda i: (_cond_of_tile(i, TMIX), 0, 0)),
            pl.BlockSpec((4, POOL_C, POOL_C), lambda i: (0, 0, 0)),
            pl.BlockSpec((1, D), lambda i: (0, 0)),
        ],
        out_specs=pl.BlockSpec((TMIX, D), lambda i: (i, 0)),
        out_shape=jax.ShapeDtypeStruct((N_TOK, D), F32),
        compiler_params=_cparams(("parallel",)),
        name="pool",
    )(y, y, y, mods, pool_w, pool_scale)


def _rope_tables():
    rows = DEC_SEQ // GRID_W
    row = jnp.repeat(jnp.arange(rows, dtype=F32), GRID_W)
    col = jnp.tile(jnp.arange(GRID_W, dtype=F32), rows)
    n_freq = ROPE // 4
    inv_freq = ROPE_THETA ** (-jnp.arange(n_freq, dtype=F32) / n_freq)
    ang = jnp.stack([row[:, None] * inv_freq, col[:, None] * inv_freq], axis=1)
    cos, sin = jnp.cos(ang), jnp.sin(ang)
    cos32 = jnp.concatenate([cos, cos], axis=-1).reshape(DEC_SEQ, ROPE)
    sin32 = jnp.concatenate([-sin, sin], axis=-1).reshape(DEC_SEQ, ROPE)
    ones = jnp.ones((DEC_SEQ, NOPE), F32)
    zeros = jnp.zeros((DEC_SEQ, NOPE), F32)
    tail = HEAD_PAD - NOPE - ROPE
    cos_t = jnp.concatenate([ones, cos32, jnp.zeros((DEC_SEQ, tail), F32)], axis=1)
    sin_t = jnp.concatenate([zeros, sin32, jnp.zeros((DEC_SEQ, tail), F32)], axis=1)
    return cos_t, sin_t


def _swap_halves(w):
    s = w.reshape(*w.shape[:-1], 2, 2, ROPE // 4)
    return s[..., ::-1, :].reshape(w.shape)


def _pad_heads(w):
    k, _, width = w.shape
    return jnp.pad(w, ((0, 0), (0, 0), (0, HEAD_PAD - width))).reshape(k, HEADS * HEAD_PAD)


def _pad_heads_alternating(w):
    k = w.shape[0]
    pair = w.reshape(k, HEADS // 2, 2, VDIM)
    z = jnp.zeros((k, HEADS // 2, VDIM), w.dtype)
    return jnp.concatenate([pair[:, :, 0], z, z, pair[:, :, 1]], axis=-1).reshape(k, HEADS * HEAD_PAD)


def _mla_weights(wq_a, q_norm, wq_b, wkv_a, kv_norm, wkv_b, wo, cos_t, sin_t):
    wq_b3 = wq_b.reshape(Q_LORA, HEADS, NOPE + ROPE)
    wq_b4 = jnp.concatenate([wq_b3, _swap_halves(wq_b3[..., NOPE:])], axis=-1)
    w_pe = wkv_a[:, KV_LORA:]
    w_pe4 = jnp.concatenate([jnp.zeros((D, NOPE), F32), w_pe, _swap_halves(w_pe)], axis=1)
    wkv_b3 = wkv_b.reshape(KV_LORA, HEADS, NOPE + VDIM)
    return {
        "w_a": jnp.concatenate([wq_a, wkv_a[:, :KV_LORA], w_pe4], axis=1).astype(BF16),
        "q_norm": q_norm.reshape(1, Q_LORA),
        "wq_b": _pad_heads(wq_b4).astype(BF16),
        "kv_norm": kv_norm.reshape(1, KV_LORA),
        "wkv_b_k": _pad_heads(wkv_b3[..., :NOPE]).astype(BF16),
        "wkv_b_v": _pad_heads_alternating(wkv_b3[..., NOPE:]).astype(BF16),
        "v_one": jnp.asarray(_V_ONE),
        "wo": wo.astype(BF16),
        "cos": cos_t,
        "sin": sin_t,
    }


def _dir_pad(w):
    z = jnp.zeros_like(w[0])
    return jnp.stack([jnp.concatenate([w[0], z], axis=0), jnp.concatenate([z, w[1]], axis=0)])


def _rwkv_weights(mu, wr, wk, wv, w0, w1, w2, a0, a1, a2, g1, g2, k_k, k_a, r_k, lnx_g, lnx_b, wo):
    blk = np.arange(256) // RWKV_N
    return {
        "mu": mu,
        "wr": wr.astype(BF16), "wk": wk.astype(BF16), "wv": wv.astype(BF16),
        "g1": g1.astype(BF16), "g2": g2.astype(BF16),
        "w1": jnp.concatenate([w1[0], w1[1]], axis=1).astype(BF16),
        "w2": _dir_pad(w2).astype(BF16),
        "a1": jnp.concatenate([a1[0], a1[1]], axis=1).astype(BF16),
        "a2": _dir_pad(a2).astype(BF16),
        "w0": w0, "a0": a0,
        "k_k": k_k.reshape(1, D), "k_a": k_a.reshape(1, D),
        "r_k": r_k.reshape(1, D), "lnx_g": lnx_g.reshape(1, D), "lnx_b": lnx_b.reshape(1, D),
        "wo": wo.astype(BF16),
        "ones_bd": jnp.asarray(blk[:, None] == blk[None, :], BF16),
    }


def _mla_layer(y, mods, w, cache_ckv, cache_kpe):
    q, ckv, kpe = _mla_pre_call(y, mods, w)
    cache_kpe_pad = jnp.pad(cache_kpe, ((0, 0), (0, 0), (NOPE, HEAD_PAD - NOPE - ROPE)))
    t_k = PAST + DEC_SEQ
    k_all, v_all = _kv_expand_call(cache_ckv.reshape(DEC_BATCH * PAST, KV_LORA),
                                   cache_kpe_pad.reshape(DEC_BATCH * PAST, HEAD_PAD), ckv, kpe, w)
    o = _attn_call(q, k_all, v_all, None, BATCH, SEQ, SEQ, 0, DEC_BATCH * t_k, "attn_prompt")
    o = _attn_call(q, k_all, v_all, o, DEC_BATCH, DEC_SEQ, t_k, N_PROMPT, 0, "attn_sample")
    new_ckv = ckv[:N_PROMPT].reshape(BATCH, SEQ, KV_LORA)
    new_kpe = kpe[:N_PROMPT, NOPE:NOPE + ROPE].reshape(BATCH, SEQ, ROPE)
    return o, new_ckv, new_kpe


def _rwkv_layer(y, mods, w, state):
    r, v, g, kk, kd, b, ld, bonus = _rwkv_pre_call(y, mods, w)
    y_fb, s_p = _scan_call(r, v, kk, kd, b, ld, None, None, BATCH, SEQ, 0, "scan_prompt")
    y_fb, _ = _scan_call(r, v, kk, kd, b, ld, state, y_fb, DEC_BATCH, DEC_SEQ, N_PROMPT, "scan_sample")
    return (y_fb[0], y_fb[1], bonus, g, w["lnx_g"], w["lnx_b"], w["ones_bd"]), s_p


def kernel(x_prompt, x_sample, cache_ckv, cache_kpe, state_wkv, c, c_ctx, ada_w, ada_b, ln_g, ln_b, ffn_wg, ffn_wu, ffn_wd, mla_wq_a, mla_q_norm, mla_wq_b, mla_wkv_a, mla_kv_norm, mla_wkv_b, mla_wo, rwkv_mu, rwkv_wr, rwkv_wk, rwkv_wv, rwkv_w0, rwkv_w1, rwkv_w2, rwkv_a0, rwkv_a1, rwkv_a2, rwkv_g1, rwkv_g2, rwkv_k_k, rwkv_k_a, rwkv_r_k, rwkv_lnx_g, rwkv_lnx_b, rwkv_wo, pool_w, pool_scale):
    y = (x_prompt.reshape(N_PROMPT, D), x_sample.reshape(N_SAMPLE, D))
    cond8 = jnp.concatenate([c_ctx[None, :], c, jnp.zeros((SUBLANES - 1 - DEC_BATCH, D), F32)], axis=0)
    mods_all = _ada_call(cond8, ada_w, ada_b)
    mods_all = jnp.pad(mods_all.reshape(DEPTH, SUBLANES, 6, D)[:, :1 + DEC_BATCH],
                       ((0, 0), (0, 0), (0, SUBLANES - 6), (0, 0)))
    cos_t, sin_t = _rope_tables()
    new_ckv, new_kpe, new_wkv = [], [], []
    for layer in range(DEPTH):
        kind, j = layer % 3, layer // 3
        mods = mods_all[layer]
        wo = None
        if kind == 0:
            w = _mla_weights(mla_wq_a[j], mla_q_norm[j], mla_wq_b[j], mla_wkv_a[j], mla_kv_norm[j],
                             mla_wkv_b[j], mla_wo[j], cos_t, sin_t)
            a, ckv, kpe = _mla_layer(y, mods, w, cache_ckv[:, j], cache_kpe[:, j])
            new_ckv.append(ckv)
            new_kpe.append(kpe)
            wo = w["wo"]
        elif kind == 1:
            w = _rwkv_weights(rwkv_mu[j], rwkv_wr[j], rwkv_wk[j], rwkv_wv[j], rwkv_w0[j], rwkv_w1[j],
                              rwkv_w2[j], rwkv_a0[j], rwkv_a1[j], rwkv_a2[j], rwkv_g1[j], rwkv_g2[j],
                              rwkv_k_k[j], rwkv_k_a[j], rwkv_r_k[j], rwkv_lnx_g[j], rwkv_lnx_b[j], rwkv_wo[j])
            a, s_new = _rwkv_layer(y, mods, w, state_wkv[:, j])
            new_wkv.append(s_new)
            wo = w["wo"]
        else:
            a = _pool_call(y, mods, pool_w[j].astype(BF16), pool_scale[j].reshape(1, D))
        if isinstance(a, tuple):
            ffn_w = [w_[layer:layer + 1].astype(BF16) for w_ in (ffn_wg, ffn_wu, ffn_wd)]
            y = _post_call(y, a, mods, ln_g[layer], ln_b[layer], wo, *ffn_w, 0)
        else:
            y = _post_call(y, a, mods, ln_g[layer], ln_b[layer], wo, ffn_wg, ffn_wu, ffn_wd, layer,
                           split_out=layer == DEPTH - 1)
    return (y[0].reshape(BATCH, SEQ, D), y[1].reshape(DEC_BATCH, DEC_SEQ, D),
            jnp.stack(new_ckv, axis=1), jnp.stack(new_kpe, axis=1), jnp.stack(new_wkv, axis=1))
```

```python
import functools

import jax
import jax.numpy as jnp
import numpy as np
from jax import lax
from jax.experimental import pallas as pl
from jax.experimental.pallas import tpu as pltpu

F32 = jnp.float32
BF16 = jnp.bfloat16

D = 1024
BATCH, SEQ = 32, 256
DEC_BATCH, DEC_SEQ = 2, 4096
PAST = 512
DEPTH = 4
GRID_W = 64
HEADS = 16
Q_LORA, KV_LORA = 384, 256
NOPE, ROPE, VDIM = 64, 32, 64
ROPE_THETA = 10000.0
RWKV_N = 64
RWKV_GN_EPS = 64e-5
POOL_WINDOWS = (2, 4, 8, 16)
POOL_C = D // 4
D_FF = 2816
ALPHA = (2 * DEPTH) ** 0.25
LN_EPS = 1e-5
RMS_EPS = 1e-6
LOG2_E = 1.4426950408889634
EXP_M_HALF = 0.6065306597126334

N_PROMPT = BATCH * SEQ
N_SAMPLE = DEC_BATCH * DEC_SEQ
N_TOK = N_PROMPT + N_SAMPLE

LANES = 128
SUBLANES = 8
HEAD_PAD = 128
VMEM_LIMIT = 56 * 1024 * 1024

TM = 512
TMIX = 256
HALO = SUBLANES
TQ = 256
ATTN_HEADS = 4
ATTN_KC = 512
ATTN_SKEW = 0
ATTN_GATE = 3
CHUNK = 64
SCAN_SEQS = 2
TFF = 256
POST_TM = 512


def _cparams(sem):
    return pltpu.CompilerParams(dimension_semantics=sem, vmem_limit_bytes=VMEM_LIMIT)


def _dot(a, b):
    return jnp.dot(a, b, preferred_element_type=F32)


def _dot_nt(a, b):
    return lax.dot_general(a, b, (((1,), (1,)), ((), ())), preferred_element_type=F32)


def _dot_tn(a, b):
    return lax.dot_general(a, b, (((0,), (0,)), ((), ())), preferred_element_type=F32)


def _layer_norm(x, g, b):
    mu = jnp.mean(x, axis=-1, keepdims=True)
    xc = x - mu
    var = jnp.mean(xc * xc, axis=-1, keepdims=True)
    return xc * lax.rsqrt(var + LN_EPS) * g + b


def _rms_norm(x, g):
    return x * lax.rsqrt(jnp.mean(x * x, axis=-1, keepdims=True) + RMS_EPS) * g


def _sigmoid(x):
    return 0.5 * jnp.tanh(0.5 * x) + 0.5


def _silu(x):
    return x * _sigmoid(x)


def _cond_of_tile(i, tile):
    first_sample = N_PROMPT // tile
    per_seq = DEC_SEQ // tile
    return jnp.where(i < first_sample, 0, 1 + (i - first_sample) // per_seq)


def _split3(x):
    hi = x.astype(BF16)
    r1 = x - hi.astype(F32)
    mid = r1.astype(BF16)
    lo = (r1 - mid.astype(F32)).astype(BF16)
    return hi, mid, lo


def _seg_sum(x, ones_bd):
    outs = []
    for g in range(D // 256):
        hi, mid, lo = _split3(x[:, 256 * g:256 * (g + 1)])
        outs.append(_dot(hi, ones_bd) + _dot(mid, ones_bd) + _dot(lo, ones_bd))
    return jnp.concatenate(outs, axis=1)


def _ada_kernel(c_ref, w_ref, b_ref, o_ref):
    a = _silu(c_ref[...]).astype(BF16)
    o_ref[...] = _dot(a, w_ref[...].astype(BF16)) + b_ref[...]


def _ada_call(cond8, ada_w, ada_b):
    tn = 1536
    return pl.pallas_call(
        _ada_kernel,
        grid=(DEPTH, 6 * D // tn),
        in_specs=[
            pl.BlockSpec((SUBLANES, D), lambda l, n: (0, 0)),
            pl.BlockSpec((None, D, tn), lambda l, n: (l, 0, n)),
            pl.BlockSpec((None, 1, tn), lambda l, n: (l, 0, n)),
        ],
        out_specs=pl.BlockSpec((None, SUBLANES, tn), lambda l, n: (l, 0, n)),
        out_shape=jax.ShapeDtypeStruct((DEPTH, SUBLANES, 6 * D), F32),
        compiler_params=_cparams(("parallel", "parallel")),
        name="adaln",
    )(cond8, ada_w, ada_b.reshape(DEPTH, 1, 6 * D))


def _act_specs(y, tile):
    if not isinstance(y, tuple):
        return [pl.BlockSpec((tile, D), lambda i: (i, 0))], [y]
    n_p = N_PROMPT // tile
    return [pl.BlockSpec((tile, D), lambda i: (jnp.minimum(i, n_p - 1), 0)),
            pl.BlockSpec((tile, D), lambda i: (jnp.maximum(i - n_p, 0), 0))], list(y)


def _act_tile(y_refs, tile):
    if len(y_refs) == 1:
        return y_refs[0][...]
    return jnp.where(pl.program_id(0) < N_PROMPT // tile, y_refs[0][...], y_refs[1][...])


def _mla_pre_kernel(n_y, *refs):
    m_ref, wa_ref, qn_ref, wqb_ref, kvn_ref, cos_ref, sin_ref, q_ref, ckv_ref, kpe_ref = refs[n_y:]
    i = pl.program_id(0)
    h = (_act_tile(refs[:n_y], TM) * (1.0 + m_ref[1:2, :]) + m_ref[0:1, :]).astype(BF16)
    x = _dot(h, wa_ref[...])
    qa = _rms_norm(x[:, :Q_LORA], qn_ref[...]).astype(BF16)
    ckv_ref[...] = _rms_norm(x[:, Q_LORA:Q_LORA + KV_LORA], kvn_ref[...])
    kpe = x[:, Q_LORA + KV_LORA:]
    q = _dot(qa, wqb_ref[...])
    is_latent = i >= N_PROMPT // TM
    swap_shift = HEAD_PAD - ROPE

    @pl.when(jnp.logical_not(is_latent))
    def _():
        q_ref[...] = q.astype(BF16)
        lane = lax.broadcasted_iota(jnp.int32, kpe.shape, 1)
        kpe_ref[...] = jnp.where(lane < NOPE + ROPE, kpe, 0.0)

    @pl.when(is_latent)
    def _():
        cos = cos_ref[...]
        sin = sin_ref[...]
        kpe_ref[...] = kpe * cos + pltpu.roll(kpe, swap_shift, 1) * sin
        for hd in range(HEADS):
            sl = slice(HEAD_PAD * hd, HEAD_PAD * (hd + 1))
            q_ref[:, sl] = (q[:, sl] * cos + pltpu.roll(q[:, sl], swap_shift, 1) * sin).astype(BF16)


def _mla_pre_call(y, mods, w):
    first_sample = N_PROMPT // TM
    per_seq = DEC_SEQ // TM
    const = lambda i: (0, 0)
    rope_map = lambda i: (jnp.where(i < first_sample, 0, (i - first_sample) % per_seq), 0)
    y_specs, y_args = _act_specs(y, TM)
    return pl.pallas_call(
        functools.partial(_mla_pre_kernel, len(y_args)),
        grid=(N_TOK // TM,),
        in_specs=y_specs + [
            pl.BlockSpec((None, SUBLANES, D), lambda i: (_cond_of_tile(i, TM), 0, 0)),
            pl.BlockSpec((D, Q_LORA + KV_LORA + HEAD_PAD), const),
            pl.BlockSpec((1, Q_LORA), const),
            pl.BlockSpec((Q_LORA, HEADS * HEAD_PAD), const),
            pl.BlockSpec((1, KV_LORA), const),
            pl.BlockSpec((TM, HEAD_PAD), rope_map),
            pl.BlockSpec((TM, HEAD_PAD), rope_map),
        ],
        out_specs=[
            pl.BlockSpec((TM, HEADS * HEAD_PAD), lambda i: (i, 0)),
            pl.BlockSpec((TM, KV_LORA), lambda i: (i, 0)),
            pl.BlockSpec((TM, HEAD_PAD), lambda i: (i, 0)),
        ],
        out_shape=[
            jax.ShapeDtypeStruct((N_TOK, HEADS * HEAD_PAD), BF16),
            jax.ShapeDtypeStruct((N_TOK, KV_LORA), F32),
            jax.ShapeDtypeStruct((N_TOK, HEAD_PAD), F32),
        ],
        compiler_params=_cparams(("parallel",)),
        name="mla_pre",
    )(*y_args, mods, w["w_a"], w["q_norm"], w["wq_b"], w["kv_norm"], w["cos"], w["sin"])


KV_TILES_PER_SEQ = (PAST + DEC_SEQ) // TM
KV_ROWS = DEC_BATCH * (PAST + DEC_SEQ) + N_PROMPT


def _kv_tile_source(i):
    n_lat = DEC_BATCH * KV_TILES_PER_SEQ
    b = jnp.minimum(i // KV_TILES_PER_SEQ, DEC_BATCH - 1)
    j = i - b * KV_TILES_PER_SEQ
    is_cache = (i < n_lat) & (j < PAST // TM)
    new_tile = N_PROMPT // TM + b * (DEC_SEQ // TM) + jnp.maximum(j - PAST // TM, 0)
    return is_cache, b * (PAST // TM) + jnp.minimum(j, PAST // TM - 1), jnp.where(i < n_lat, new_tile, i - n_lat)


def _kv_expand_kernel(cckv_ref, ckpe_ref, ckv_ref, kpe_ref, wk_ref, wv_ref, vone_ref, k_ref, v_ref):
    is_cache, _, _ = _kv_tile_source(pl.program_id(0))
    c = jnp.where(is_cache, cckv_ref[...], ckv_ref[...]).astype(BF16)
    kpe = jnp.where(is_cache, ckpe_ref[...], kpe_ref[...])
    kn = _dot(c, wk_ref[...])
    for hd in range(HEADS):
        sl = slice(HEAD_PAD * hd, HEAD_PAD * (hd + 1))
        k_ref[:, sl] = (kn[:, sl] + kpe).astype(BF16)
    v_ref[...] = (_dot(c, wv_ref[...]) + vone_ref[...]).astype(BF16)


def _kv_expand_call(cache_ckv, cache_kpe, ckv, kpe, w):
    rows = KV_ROWS
    const = lambda i: (0, 0)
    return pl.pallas_call(
        _kv_expand_kernel,
        grid=(rows // TM,),
        in_specs=[
            pl.BlockSpec((TM, KV_LORA), lambda i: (_kv_tile_source(i)[1], 0)),
            pl.BlockSpec((TM, HEAD_PAD), lambda i: (_kv_tile_source(i)[1], 0)),
            pl.BlockSpec((TM, KV_LORA), lambda i: (_kv_tile_source(i)[2], 0)),
            pl.BlockSpec((TM, HEAD_PAD), lambda i: (_kv_tile_source(i)[2], 0)),
            pl.BlockSpec((KV_LORA, HEADS * HEAD_PAD), const),
            pl.BlockSpec((KV_LORA, HEADS * HEAD_PAD), const),
            pl.BlockSpec((1, HEADS * HEAD_PAD), const),
        ],
        out_specs=[
            pl.BlockSpec((TM, HEADS * HEAD_PAD), lambda i: (i, 0)),
            pl.BlockSpec((TM, HEADS * HEAD_PAD), lambda i: (i, 0)),
        ],
        out_shape=[jax.ShapeDtypeStruct((rows, HEADS * HEAD_PAD), BF16)] * 2,
        compiler_params=_cparams(("parallel",)),
        name="kv_expand",
    )(cache_ckv, cache_kpe, ckv, kpe, w["wkv_b_k"], w["wkv_b_v"], w["v_one"])


def _sum_lane(h):
    return HEAD_PAD - 1 if h % 2 == 0 else 0


_V_ONE = np.zeros((1, HEADS * HEAD_PAD), np.float32)
_V_ONE[0, [HEAD_PAD * h + _sum_lane(h) for h in range(HEADS)]] = 1.0


def _zero_like_col(x):
    bits = pltpu.bitcast(x, jnp.uint32)
    zero = lax.shift_right_logical(lax.shift_right_logical(bits, jnp.uint32(16)), jnp.uint32(16))
    return pltpu.bitcast(zero, F32)[:, 0:1]


def _attn_kernel(t_k, q_ref, k_ref, v_ref, *rest):
    o_ref = rest[-1]
    scale2 = (NOPE + ROPE) ** -0.5 * LOG2_E
    kc = min(ATTN_KC, t_k)
    chunks = [slice(c * kc, (c + 1) * kc) for c in range(t_k // kc)]
    n_heads = q_ref.shape[1] // HEAD_PAD
    head = [slice(HEAD_PAD * e, HEAD_PAD * (e + 1)) for e in range(n_heads)]

    def scores(e, c):
        return _dot_nt(q_ref[:, head[e]], k_ref[c, head[e]])

    lane = lax.broadcasted_iota(jnp.int32, (TQ, HEAD_PAD), 1)

    def normalised(e, acc):
        sl = _sum_lane(e)
        return jnp.where(lane == sl, 0.0, acc) * (1.0 / acc[:, sl:sl + 1])

    def store(outs):
        for j in range(n_heads // 2):
            o_ref[:, 2 * VDIM * j:2 * VDIM * (j + 1)] = (outs[2 * j] + outs[2 * j + 1]).astype(BF16)

    if len(chunks) == 1:
        s_all = [scores(e, chunks[0]) for e in range(n_heads)]
        p_all = [jnp.exp2((s - jnp.max(s, axis=-1, keepdims=True)) * scale2).astype(BF16) for s in s_all]
        store([normalised(e, _dot(p_all[e], v_ref[:, head[e]])) for e in range(n_heads)])
        return

    s_cur = [scores(0, c) for c in chunks]
    outs = []
    for e in range(n_heads):
        m = functools.reduce(jnp.maximum, [jnp.max(s, axis=-1, keepdims=True) for s in s_cur])
        s_next, accs = [], []
        for i in range(len(chunks) + ATTN_SKEW):
            if e + 1 < n_heads and i < len(chunks):
                s_next.append(scores(e + 1, chunks[i]))
            if i >= ATTN_SKEW:
                j = i - ATTN_SKEW
                mj = m + _zero_like_col(accs[j - ATTN_GATE]) if j >= ATTN_GATE else m
                p = jnp.exp2((s_cur[j] - mj) * scale2).astype(BF16)
                pv = _dot(p, v_ref[chunks[j], head[e]])
                accs.append(pv if not accs else accs[-1] + pv)
        outs.append(normalised(e, accs[-1]))
        s_cur = s_next
    store(outs)


def _attn_call(q, k, v, o_buf, n_seq, t_q, t_k, q_row0, k_row0, name):
    nq = t_q // TQ
    q_blk0 = q_row0 // TQ
    k_blk0 = k_row0 // t_k
    heads = ATTN_HEADS if t_k > ATTN_KC else HEADS
    width = heads * HEAD_PAD
    in_specs = [
        pl.BlockSpec((TQ, width), lambda b, p, i: (q_blk0 + b * nq + i, p)),
        pl.BlockSpec((t_k, width), lambda b, p, i: (k_blk0 + b, p)),
        pl.BlockSpec((t_k, width), lambda b, p, i: (k_blk0 + b, p)),
    ]
    args = [q, k, v]
    if o_buf is not None:
        in_specs.append(pl.BlockSpec(memory_space=pl.ANY))
        args.append(o_buf)
    return pl.pallas_call(
        functools.partial(_attn_kernel, t_k),
        grid=(n_seq, HEADS // heads, nq),
        in_specs=in_specs,
        out_specs=pl.BlockSpec((TQ, heads * VDIM), lambda b, p, i: (q_blk0 + b * nq + i, p)),
        out_shape=jax.ShapeDtypeStruct((N_TOK, HEADS * VDIM), BF16),
        input_output_aliases={} if o_buf is None else {3: 0},
        compiler_params=_cparams(("parallel", "parallel", "arbitrary")),
        name=name,
    )(*args)


def _rwkv_gate(yf_ref, yb_ref, bonus_ref, g_ref, lg_ref, lb_ref, ones_ref):
    ones = ones_ref[...]
    y = yf_ref[...] + yb_ref[...]
    mu = _seg_sum(y, ones) * (1.0 / RWKV_N)
    yc = y - mu
    var = _seg_sum(yc * yc, ones) * (1.0 / RWKV_N)
    yn = yc * lax.rsqrt(var + RWKV_GN_EPS) * lg_ref[...] + lb_ref[...]
    return ((yn + bonus_ref[...]) * g_ref[...]).astype(BF16)


def _post_kernel(n_y, n_mix, has_proj, split_out, *refs):
    y_refs = refs[:n_y]
    mix_refs = refs[n_y:n_y + n_mix]
    m_ref, lng_ref, lnb_ref = refs[n_y + n_mix:n_y + n_mix + 3]
    rest = refs[n_y + n_mix + 3:]
    if has_proj:
        wo_ref, rest = rest[0], rest[1:]
    wg_ref, wu_ref, wd_ref = rest[:3]
    o_refs = rest[3:]
    is_prompt = pl.program_id(0) < N_PROMPT // POST_TM
    a = mix_refs[0][...] if n_mix == 1 else _rwkv_gate(*mix_refs)
    mix = _dot(a, wo_ref[...]) if has_proj else a
    y = y_refs[0][...] if n_y == 1 else jnp.where(is_prompt, y_refs[0][...], y_refs[1][...])
    y1 = _layer_norm(ALPHA * y + m_ref[2:3, :] * mix, lng_ref[0:1, :], lnb_ref[0:1, :])
    h = (y1 * (1.0 + m_ref[4:5, :]) + m_ref[3:4, :]).astype(BF16)
    acc = None
    for j in range(D_FF // TFF):
        cols = slice(TFF * j, TFF * (j + 1))
        act = (_silu(_dot(h, wg_ref[:, cols].astype(BF16))) * _dot(h, wu_ref[:, cols].astype(BF16))).astype(BF16)
        part = _dot(act, wd_ref[cols, :].astype(BF16))
        acc = part if acc is None else acc + part
    out = _layer_norm(ALPHA * y1 + m_ref[5:6, :] * acc, lng_ref[1:2, :], lnb_ref[1:2, :])
    if not split_out:
        o_refs[0][...] = out
    else:
        @pl.when(is_prompt)
        def _():
            o_refs[0][...] = out

        @pl.when(jnp.logical_not(is_prompt))
        def _():
            o_refs[1][...] = out


def _post_call(y, a, mods, ln_g, ln_b, wo, wg, wu, wd, layer, split_out=False):
    has_proj = wo is not None
    const = lambda i: (0, 0)
    this_layer = lambda i: (layer, 0, 0)
    resident = pl.Buffered(1)
    tile = POST_TM
    y_specs, y_args = _act_specs(y, tile)
    row = pl.BlockSpec((tile, D), lambda i: (i, 0))
    if isinstance(a, tuple):
        mix_specs = [row] * 4 + [pl.BlockSpec((1, D), const)] * 2 + [pl.BlockSpec((256, 256), const)]
        mix_args = list(a)
    else:
        mix_specs, mix_args = [row], [a]
    in_specs = y_specs + mix_specs + [
        pl.BlockSpec((None, SUBLANES, D), lambda i: (_cond_of_tile(i, tile), 0, 0)),
        pl.BlockSpec((2, D), const),
        pl.BlockSpec((2, D), const),
    ]
    args = y_args + mix_args + [mods, ln_g, ln_b]
    if has_proj:
        in_specs.append(pl.BlockSpec((D, D), const, pipeline_mode=resident))
        args.append(wo)
    in_specs += [
        pl.BlockSpec((None, D, D_FF), this_layer, pipeline_mode=resident),
        pl.BlockSpec((None, D, D_FF), this_layer, pipeline_mode=resident),
        pl.BlockSpec((None, D_FF, D), this_layer, pipeline_mode=resident),
    ]
    args += [wg, wu, wd]
    if split_out:
        n_p = N_PROMPT // tile
        out_specs = [pl.BlockSpec((tile, D), lambda i: (jnp.minimum(i, n_p - 1), 0)),
                     pl.BlockSpec((tile, D), lambda i: (jnp.maximum(i - n_p, 0), 0))]
        out_shape = [jax.ShapeDtypeStruct((N_PROMPT, D), F32), jax.ShapeDtypeStruct((N_SAMPLE, D), F32)]
    else:
        out_specs = pl.BlockSpec((tile, D), lambda i: (i, 0))
        out_shape = jax.ShapeDtypeStruct((N_TOK, D), F32)
    return pl.pallas_call(
        functools.partial(_post_kernel, len(y_args), len(mix_args), has_proj, split_out),
        grid=(N_TOK // tile,),
        in_specs=in_specs,
        out_specs=out_specs,
        out_shape=out_shape,
        compiler_params=_cparams(("arbitrary",) if split_out else ("parallel",)),
        name="post_proj" if has_proj else "post_noproj",
    )(*args)


def _mix_tile_flags(i):
    first_sample = N_PROMPT // TMIX
    tiles_prompt = SEQ // TMIX
    tiles_sample = DEC_SEQ // TMIX
    is_p = i < first_sample
    j = jnp.where(is_p, i % tiles_prompt, (i - first_sample) % tiles_sample)
    n = jnp.where(is_p, tiles_prompt, tiles_sample)
    return j > 0, j < n - 1, j * TMIX, n * TMIX


def _halo_specs():
    blocks_per_tile = TMIX // HALO
    last = N_TOK // HALO - 1
    return [
        pl.BlockSpec((HALO, D), lambda i: (jnp.maximum(i * blocks_per_tile - 1, 0), 0)),
        pl.BlockSpec((TMIX, D), lambda i: (i, 0)),
        pl.BlockSpec((HALO, D), lambda i: (jnp.minimum((i + 1) * blocks_per_tile, last), 0)),
    ]


EXT = TMIX + 2 * HALO


def _ext_tile(yp_ref, y_ref, yn_ref, m_ref, has_prev, has_next):
    scale1 = 1.0 + m_ref[1:2, :]
    shift = m_ref[0:1, :]
    h = y_ref[...] * scale1 + shift
    before = jnp.where(has_prev, yp_ref[...] * scale1 + shift, 0.0)
    after = jnp.where(has_next, yn_ref[...] * scale1 + shift, 0.0)
    return h, jnp.concatenate([before, h, after], axis=0)


def _shifted(ext, j):
    if j == 0:
        return ext[HALO:HALO + TMIX]
    return pltpu.roll(ext, (-j) % EXT, 0)[HALO:HALO + TMIX]


def _rwkv_pre_kernel(yp_ref, y_ref, yn_ref, m_ref, mu_ref, wr_ref, wk_ref, wv_ref, g1_ref, g2_ref,
                     w1_ref, w2_ref, a1_ref, a2_ref, w0_ref, a0_ref, kk_ref, ka_ref, rk_ref, ones_ref,
                     r_out, v_out, g_out, kk_out, kd_out, b_out, ld_out, bonus_out):
    i = pl.program_id(0)
    has_prev, has_next, _, _ = _mix_tile_flags(i)
    h, ext = _ext_tile(yp_ref, y_ref, yn_ref, m_ref, has_prev, has_next)
    xx = 0.5 * (_shifted(ext, -1) + _shifted(ext, 1)) - h

    def mix(j):
        return (h + xx * mu_ref[j:j + 1, :]).astype(BF16)

    r = _dot(mix(0), wr_ref[...])
    r_out[...] = r
    k = _dot(mix(2), wk_ref[...])
    v = _dot(mix(3), wv_ref[...])
    v_out[...] = v.astype(BF16)
    g_out[...] = _dot(_sigmoid(_dot(mix(5), g1_ref[...])).astype(BF16), g2_ref[...])
    tw = jnp.tanh(_dot(mix(1), w1_ref[...])).astype(BF16)
    ta = _dot(mix(4), a1_ref[...]).astype(BF16)

    kkf = k * kk_ref[...]
    kk = kkf * jnp.minimum(lax.rsqrt(_seg_sum(kkf * kkf, ones_ref[...])), 1e12)
    kk_out[...] = kk
    kd_sum = None
    for d in range(2):
        ld_out[d] = -EXP_M_HALF * _sigmoid(w0_ref[d:d + 1, :] + _dot(tw, w2_ref[d]))
        a = _sigmoid(a0_ref[d:d + 1, :] + _dot(ta, a2_ref[d]))
        kd = k * (1.0 + (a - 1.0) * ka_ref[...])
        kd_out[d] = kd
        b_out[d] = kk * a
        kd_sum = kd if kd_sum is None else kd_sum + kd
    bonus_out[...] = _seg_sum(r * kd_sum * rk_ref[...], ones_ref[...]) * v


def _rwkv_pre_call(y, mods, w):
    const2 = lambda i: (0, 0)
    const3 = lambda i: (0, 0, 0)
    row = pl.BlockSpec((TMIX, D), lambda i: (i, 0))
    row2 = pl.BlockSpec((2, TMIX, D), lambda i: (0, i, 0))
    one = jax.ShapeDtypeStruct((N_TOK, D), F32)
    two = jax.ShapeDtypeStruct((2, N_TOK, D), F32)
    return pl.pallas_call(
        _rwkv_pre_kernel,
        grid=(N_TOK // TMIX,),
        in_specs=_halo_specs() + [
            pl.BlockSpec((None, SUBLANES, D), lambda i: (_cond_of_tile(i, TMIX), 0, 0)),
            pl.BlockSpec((6, D), const2),
            pl.BlockSpec((D, D), const2), pl.BlockSpec((D, D), const2), pl.BlockSpec((D, D), const2),
            pl.BlockSpec((D, 128), const2), pl.BlockSpec((128, D), const2),
            pl.BlockSpec((D, 128), const2), pl.BlockSpec((2, 128, D), const3),
            pl.BlockSpec((D, 128), const2), pl.BlockSpec((2, 128, D), const3),
            pl.BlockSpec((2, D), const2), pl.BlockSpec((2, D), const2),
            pl.BlockSpec((1, D), const2), pl.BlockSpec((1, D), const2), pl.BlockSpec((1, D), const2),
            pl.BlockSpec((256, 256), const2),
        ],
        out_specs=[row, row, row, row, row2, row2, row2, row],
        out_shape=[one, jax.ShapeDtypeStruct((N_TOK, D), BF16), one, one, two, two, two, one],
        compiler_params=_cparams(("parallel",)),
        name="rwkv_pre",
    )(y, y, y, mods, w["mu"], w["wr"], w["wk"], w["wv"], w["g1"], w["g2"], w["w1"], w["w2"],
      w["a1"], w["a2"], w["w0"], w["a0"], w["k_k"], w["k_a"], w["r_k"], w["ones_bd"])


def _scan_kernel(n_chunks, has_s0, aliased, *refs):
    ins = [refs[0:6], refs[6:12]]
    pos = 12
    s0_ref = refs[pos] if has_s0 else None
    pos += (1 if has_s0 else 0) + (2 if aliased else 0)
    y_refs = refs[pos:pos + 2]
    sf_ref, s_ref = refs[pos + 2], refs[pos + 3]
    n = pl.program_id(1)
    C = CHUNK
    P2 = 2 * C
    streams = [(s, d) for s in range(SCAN_SEQS) for d in range(2)]
    units = [(q, p) for q in range(len(streams)) for p in range(HEADS // 2)]
    lanes = [slice(2 * RWKV_N * p, 2 * RWKV_N * (p + 1)) for p in range(HEADS // 2)]

    @pl.when(n == 0)
    def _():
        s_ref[...] = jnp.zeros_like(s_ref)
        if has_s0:
            for q, (s, d) in enumerate(streams):
                for hd in range(HEADS):
                    o = RWKV_N * (hd % 2)
                    s_ref[q, hd // 2, o:o + RWKV_N, o:o + RWKV_N] = s0_ref[s, d, hd]

    lane = lax.broadcasted_iota(jnp.int32, (C, 2 * RWKV_N), 1)
    even = lane < RWKV_N
    ti = lax.broadcasted_iota(jnp.int32, (C, C), 0)
    si = lax.broadcasted_iota(jnp.int32, (C, C), 1)
    r2 = lax.broadcasted_iota(jnp.int32, (P2, P2), 0)
    c2 = lax.broadcasted_iota(jnp.int32, (P2, P2), 1)
    same = (r2 // C) == (c2 // C)
    eye = (r2 == c2).astype(F32)

    def same_block(size):
        return (r2 // size) == (c2 // size)

    leaf = same_block(2)
    merges = [same_block(2 * size) & jnp.logical_not(same_block(size)) for size in (2, 4, 8, 16, 32)]

    def split(x):
        return jnp.concatenate([jnp.where(even, x, 0.0), jnp.where(even, 0.0, x)], axis=0)

    def twice(x):
        return jnp.concatenate([x, x], axis=0)

    cum, m_incl, m_strict = [None, None], [None, None], [None, None]
    for d in range(2):
        sign = 1 - 2 * d
        cum[d] = (sign * (si - ti) <= 0).astype(BF16)
        order = sign * (c2 % C - r2 % C)
        m_incl[d] = same & (order <= 0)
        m_strict[d] = same & (order < 0)
    at, rt, bt, kt, bh, kh, pc, v = ([None] * len(streams) for _ in range(8))
    for q, (s, d) in enumerate(streams):
        r_ref, v_ref, kk_ref, kd_ref, b_ref, ld_ref = ins[d]
        ld = ld_ref[s]
        ld_hi, ld_mid, ld_lo = _split3(ld)
        cs = _dot(cum[d], ld_hi) + _dot(cum[d], ld_mid) + _dot(cum[d], ld_lo)
        last = cs[0:1, :] if d == 1 else cs[C - 1:C, :]
        pin = jnp.exp(cs)
        pinv = jnp.exp(-cs)
        at[q] = -kk_ref[s] * jnp.exp(cs - ld)
        rt[q] = r_ref[s] * pin
        bt[q] = b_ref[s] * pinv
        kt[q] = kd_ref[s] * pinv
        pc[q] = jnp.exp(last)
        tail = pc[q] * pinv
        bh[q] = b_ref[s] * tail
        kh[q] = kd_ref[s] * tail
        v[q] = v_ref[s]
    m_incl = [m_incl[d] for _, d in streams]
    m_strict = [m_strict[d] for _, d in streams]

    s0 = [s_ref[q, p] for q, p in units]
    lhs = [jnp.concatenate([split(at[q][:, lanes[p]]), split(rt[q][:, lanes[p]])], axis=0).astype(BF16)
           for q, p in units]
    rhs = [jnp.concatenate([twice(bt[q][:, lanes[p]]), twice(kt[q][:, lanes[p]])], axis=0).astype(BF16)
           for q, p in units]
    v2b = [split(v[q][:, lanes[p]]).astype(BF16) for q, p in units]
    idx = range(len(units))
    g = [_dot_nt(lhs[u], rhs[u]) for u in idx]
    xs = [_dot_nt(lhs[u], s0[u].astype(BF16)) for u in idx]
    a_ab = [jnp.where(m_strict[units[u][0]], g[u][:P2, :P2], 0.0) for u in idx]
    inv = [eye + jnp.where(leaf, a_ab[u], 0.0) for u in idx]
    for mk in merges:
        invb = [inv[u].astype(BF16) for u in idx]
        binv = [_dot(jnp.where(mk, a_ab[u], 0.0).astype(BF16), invb[u]).astype(BF16) for u in idx]
        inv = [inv[u] + _dot(invb[u], binv[u]) for u in idx]
    a_ak = [jnp.where(m_strict[units[u][0]], g[u][:P2, P2:], 0.0).astype(BF16) for u in idx]
    rhs_u = [(xs[u][:P2] + _dot(a_ak[u], v2b[u])).astype(BF16) for u in idx]
    u2b = [_dot(inv[u].astype(BF16), rhs_u[u]).astype(BF16) for u in idx]
    m_r = [jnp.concatenate([jnp.where(m_incl[units[u][0]], g[u][P2:, :P2], 0.0),
                            jnp.where(m_incl[units[u][0]], g[u][P2:, P2:], 0.0)], axis=1).astype(BF16)
           for u in idx]
    uv = [jnp.concatenate([u2b[u], v2b[u]], axis=0) for u in idx]
    for u, (q, p) in enumerate(units):
        s, d = streams[q]
        y2 = xs[u][P2:] + _dot(m_r[u], uv[u])
        y_refs[d][s, :, lanes[p]] = y2[:C] + y2[C:]
    for u, (q, p) in enumerate(units):
        bk = jnp.concatenate([split(bh[q][:, lanes[p]]), split(kh[q][:, lanes[p]])], axis=0).astype(BF16)
        s_ref[q, p] = s0[u] * pc[q][:, lanes[p]] + _dot_tn(uv[u], bk)

    @pl.when(n == n_chunks - 1)
    def _():
        for q, (s, d) in enumerate(streams):
            for hd in range(HEADS):
                o = RWKV_N * (hd % 2)
                sf_ref[s, d, hd] = s_ref[q, hd // 2, o:o + RWKV_N, o:o + RWKV_N]


def _scan_call(r, v, kk, kd, b, ld, s0, y_bufs, n_seq, t_seq, row0, name):
    n_chunks = t_seq // CHUNK
    rows3 = (N_TOK // t_seq, t_seq, D)
    blk0 = row0 // t_seq // SCAN_SEQS
    chunk = [lambda nn: nn, lambda nn: n_chunks - 1 - nn]
    in_specs, args = [], []
    for d in range(2):
        row = pl.BlockSpec((SCAN_SEQS, CHUNK, D), lambda bb, nn, d=d: (blk0 + bb, chunk[d](nn), 0))
        row2 = pl.BlockSpec((None, SCAN_SEQS, CHUNK, D), lambda bb, nn, d=d: (d, blk0 + bb, chunk[d](nn), 0))
        in_specs += [row, row, row, row2, row2, row2]
        args += [r.reshape(rows3), v.reshape(rows3), kk.reshape(rows3),
                 kd.reshape(2, *rows3), b.reshape(2, *rows3), ld.reshape(2, *rows3)]
    st = pl.BlockSpec((SCAN_SEQS, 2, HEADS, RWKV_N, RWKV_N), lambda bb, nn: (bb, 0, 0, 0, 0))
    if s0 is not None:
        in_specs.append(st)
        args.append(s0)
    aliases = {}
    if y_bufs is not None:
        aliases = {len(args): 0, len(args) + 1: 1}
        in_specs += [pl.BlockSpec(memory_space=pl.ANY)] * 2
        args += [y.reshape(rows3) for y in y_bufs]
    out = pl.pallas_call(
        functools.partial(_scan_kernel, n_chunks, s0 is not None, y_bufs is not None),
        grid=(n_seq // SCAN_SEQS, n_chunks),
        in_specs=in_specs,
        out_specs=[
            pl.BlockSpec((SCAN_SEQS, CHUNK, D), lambda bb, nn: (blk0 + bb, chunk[0](nn), 0)),
            pl.BlockSpec((SCAN_SEQS, CHUNK, D), lambda bb, nn: (blk0 + bb, chunk[1](nn), 0)),
            st,
        ],
        out_shape=[
            jax.ShapeDtypeStruct(rows3, F32),
            jax.ShapeDtypeStruct(rows3, F32),
            jax.ShapeDtypeStruct((n_seq, 2, HEADS, RWKV_N, RWKV_N), F32),
        ],
        scratch_shapes=[pltpu.VMEM((2 * SCAN_SEQS, HEADS // 2, 2 * RWKV_N, 2 * RWKV_N), F32)],
        input_output_aliases=aliases,
        compiler_params=_cparams(("parallel", "arbitrary")),
        name=name,
    )(*args)
    return (out[0].reshape(N_TOK, D), out[1].reshape(N_TOK, D)), out[2]


def _pool_kernel(yp_ref, y_ref, yn_ref, m_ref, w_ref, sc_ref, o_ref):
    i = pl.program_id(0)
    has_prev, has_next, pos0, seq_len = _mix_tile_flags(i)
    h, ext = _ext_tile(yp_ref, y_ref, yn_ref, m_ref, has_prev, has_next)
    t = pos0 + lax.broadcasted_iota(jnp.int32, (TMIX, 1), 0)
    for gi, win in enumerate(POOL_WINDOWS):
        cols = slice(POOL_C * gi, POOL_C * (gi + 1))
        ext_g = ext[:, cols]
        acc = None
        for j in range(-(win // 2), win - win // 2):
            x = _shifted(ext_g, j)
            acc = x if acc is None else acc + x
        lo = jnp.maximum(t - win // 2, 0)
        hi = jnp.minimum(t - win // 2 + win, seq_len)
        cnt = (hi - lo).astype(F32)
        pooled = (acc / cnt - h[:, cols]).astype(BF16)
        o_ref[:, cols] = _dot(pooled, w_ref[gi]) * sc_ref[:, cols]


def _pool_call(y, mods, pool_w, pool_scale):
    return pl.pallas_call(
        _pool_kernel,
        grid=(N_TOK // TMIX,),
        in_specs=_halo_specs() + [
            pl.BlockSpec((None, SUBLANES, D), lambda i: (_cond_of_tile(i, TMIX), 0, 0)),
            pl.BlockSpec((4, POOL_C, POOL_C), lambda i: (0, 0, 0)),
            pl.BlockSpec((1, D), lambda i: (0, 0)),
        ],
        out_specs=pl.BlockSpec((TMIX, D), lambda i: (i, 0)),
        out_shape=jax.ShapeDtypeStruct((N_TOK, D), F32),
        compiler_params=_cparams(("parallel",)),
        name="pool",
    )(y, y, y, mods, pool_w, pool_scale)


def _rope_tables():
    rows = DEC_SEQ // GRID_W
    row = jnp.repeat(jnp.arange(rows, dtype=F32), GRID_W)
    col = jnp.tile(jnp.arange(GRID_W, dtype=F32), rows)
    n_freq = ROPE // 4
    inv_freq = ROPE_THETA ** (-jnp.arange(n_freq, dtype=F32) / n_freq)
    ang = jnp.stack([row[:, None] * inv_freq, col[:, None] * inv_freq], axis=1)
    cos, sin = jnp.cos(ang), jnp.sin(ang)
    cos32 = jnp.concatenate([cos, cos], axis=-1).reshape(DEC_SEQ, ROPE)
    sin32 = jnp.concatenate([-sin, sin], axis=-1).reshape(DEC_SEQ, ROPE)
    ones = jnp.ones((DEC_SEQ, NOPE), F32)
    zeros = jnp.zeros((DEC_SEQ, NOPE), F32)
    tail = HEAD_PAD - NOPE - ROPE
    cos_t = jnp.concatenate([ones, cos32, jnp.zeros((DEC_SEQ, tail), F32)], axis=1)
    sin_t = jnp.concatenate([zeros, sin32, jnp.zeros((DEC_SEQ, tail), F32)], axis=1)
    return cos_t, sin_t


def _swap_halves(w):
    s = w.reshape(*w.shape[:-1], 2, 2, ROPE // 4)
    return s[..., ::-1, :].reshape(w.shape)


def _pad_heads(w):
    k, _, width = w.shape
    return jnp.pad(w, ((0, 0), (0, 0), (0, HEAD_PAD - width))).reshape(k, HEADS * HEAD_PAD)


def _pad_heads_alternating(w):
    k = w.shape[0]
    pair = w.reshape(k, HEADS // 2, 2, VDIM)
    z = jnp.zeros((k, HEADS // 2, VDIM), w.dtype)
    return jnp.concatenate([pair[:, :, 0], z, z, pair[:, :, 1]], axis=-1).reshape(k, HEADS * HEAD_PAD)


def _mla_weights(wq_a, q_norm, wq_b, wkv_a, kv_norm, wkv_b, wo, cos_t, sin_t):
    wq_b3 = wq_b.reshape(Q_LORA, HEADS, NOPE + ROPE)
    wq_b4 = jnp.concatenate([wq_b3, _swap_halves(wq_b3[..., NOPE:])], axis=-1)
    w_pe = wkv_a[:, KV_LORA:]
    w_pe4 = jnp.concatenate([jnp.zeros((D, NOPE), F32), w_pe, _swap_halves(w_pe)], axis=1)
    wkv_b3 = wkv_b.reshape(KV_LORA, HEADS, NOPE + VDIM)
    return {
        "w_a": jnp.concatenate([wq_a, wkv_a[:, :KV_LORA], w_pe4], axis=1).astype(BF16),
        "q_norm": q_norm.reshape(1, Q_LORA),
        "wq_b": _pad_heads(wq_b4).astype(BF16),
        "kv_norm": kv_norm.reshape(1, KV_LORA),
        "wkv_b_k": _pad_heads(wkv_b3[..., :NOPE]).astype(BF16),
        "wkv_b_v": _pad_heads_alternating(wkv_b3[..., NOPE:]).astype(BF16),
        "v_one": jnp.asarray(_V_ONE),
        "wo": wo.astype(BF16),
        "cos": cos_t,
        "sin": sin_t,
    }


def _dir_pad(w):
    z = jnp.zeros_like(w[0])
    return jnp.stack([jnp.concatenate([w[0], z], axis=0), jnp.concatenate([z, w[1]], axis=0)])


def _rwkv_weights(mu, wr, wk, wv, w0, w1, w2, a0, a1, a2, g1, g2, k_k, k_a, r_k, lnx_g, lnx_b, wo):
    blk = np.arange(256) // RWKV_N
    return {
        "mu": mu,
        "wr": wr.astype(BF16), "wk": wk.astype(BF16), "wv": wv.astype(BF16),
        "g1": g1.astype(BF16), "g2": g2.astype(BF16),
        "w1": jnp.concatenate([w1[0], w1[1]], axis=1).astype(BF16),
        "w2": _dir_pad(w2).astype(BF16),
        "a1": jnp.concatenate([a1[0], a1[1]], axis=1).astype(BF16),
        "a2": _dir_pad(a2).astype(BF16),
        "w0": w0, "a0": a0,
        "k_k": k_k.reshape(1, D), "k_a": k_a.reshape(1, D),
        "r_k": r_k.reshape(1, D), "lnx_g": lnx_g.reshape(1, D), "lnx_b": lnx_b.reshape(1, D),
        "wo": wo.astype(BF16),
        "ones_bd": jnp.asarray(blk[:, None] == blk[None, :], BF16),
    }


def _mla_layer(y, mods, w, cache_ckv, cache_kpe):
    q, ckv, kpe = _mla_pre_call(y, mods, w)
    cache_kpe_pad = jnp.pad(cache_kpe, ((0, 0), (0, 0), (NOPE, HEAD_PAD - NOPE - ROPE)))
    t_k = PAST + DEC_SEQ
    k_all, v_all = _kv_expand_call(cache_ckv.reshape(DEC_BATCH * PAST, KV_LORA),
                                   cache_kpe_pad.reshape(DEC_BATCH * PAST, HEAD_PAD), ckv, kpe, w)
    o = _attn_call(q, k_all, v_all, None, BATCH, SEQ, SEQ, 0, DEC_BATCH * t_k, "attn_prompt")
    o = _attn_call(q, k_all, v_all, o, DEC_BATCH, DEC_SEQ, t_k, N_PROMPT, 0, "attn_sample")
    new_ckv = ckv[:N_PROMPT].reshape(BATCH, SEQ, KV_LORA)
    new_kpe = kpe[:N_PROMPT, NOPE:NOPE + ROPE].reshape(BATCH, SEQ, ROPE)
    return o, new_ckv, new_kpe


def _rwkv_layer(y, mods, w, state):
    r, v, g, kk, kd, b, ld, bonus = _rwkv_pre_call(y, mods, w)
    y_fb, s_p = _scan_call(r, v, kk, kd, b, ld, None, None, BATCH, SEQ, 0, "scan_prompt")
    y_fb, _ = _scan_call(r, v, kk, kd, b, ld, state, y_fb, DEC_BATCH, DEC_SEQ, N_PROMPT, "scan_sample")
    return (y_fb[0], y_fb[1], bonus, g, w["lnx_g"], w["lnx_b"], w["ones_bd"]), s_p


def kernel(x_prompt, x_sample, cache_ckv, cache_kpe, state_wkv, c, c_ctx, ada_w, ada_b, ln_g, ln_b, ffn_wg, ffn_wu, ffn_wd, mla_wq_a, mla_q_norm, mla_wq_b, mla_wkv_a, mla_kv_norm, mla_wkv_b, mla_wo, rwkv_mu, rwkv_wr, rwkv_wk, rwkv_wv, rwkv_w0, rwkv_w1, rwkv_w2, rwkv_a0, rwkv_a1, rwkv_a2, rwkv_g1, rwkv_g2, rwkv_k_k, rwkv_k_a, rwkv_r_k, rwkv_lnx_g, rwkv_lnx_b, rwkv_wo, pool_w, pool_scale):
    y = (x_prompt.reshape(N_PROMPT, D), x_sample.reshape(N_SAMPLE, D))
    cond8 = jnp.concatenate([c_ctx[None, :], c, jnp.zeros((SUBLANES - 1 - DEC_BATCH, D), F32)], axis=0)
    mods_all = _ada_call(cond8, ada_w, ada_b)
    mods_all = jnp.pad(mods_all.reshape(DEPTH, SUBLANES, 6, D)[:, :1 + DEC_BATCH],
                       ((0, 0), (0, 0), (0, SUBLANES - 6), (0, 0)))
    cos_t, sin_t = _rope_tables()
    new_ckv, new_kpe, new_wkv = [], [], []
    for layer in range(DEPTH):
        kind, j = layer % 3, layer // 3
        mods = mods_all[layer]
        wo = None
        if kind == 0:
            w = _mla_weights(mla_wq_a[j], mla_q_norm[j], mla_wq_b[j], mla_wkv_a[j], mla_kv_norm[j],
                             mla_wkv_b[j], mla_wo[j], cos_t, sin_t)
            a, ckv, kpe = _mla_layer(y, mods, w, cache_ckv[:, j], cache_kpe[:, j])
            new_ckv.append(ckv)
            new_kpe.append(kpe)
            wo = w["wo"]
        elif kind == 1:
            w = _rwkv_weights(rwkv_mu[j], rwkv_wr[j], rwkv_wk[j], rwkv_wv[j], rwkv_w0[j], rwkv_w1[j],
                              rwkv_w2[j], rwkv_a0[j], rwkv_a1[j], rwkv_a2[j], rwkv_g1[j], rwkv_g2[j],
                              rwkv_k_k[j], rwkv_k_a[j], rwkv_r_k[j], rwkv_lnx_g[j], rwkv_lnx_b[j], rwkv_wo[j])
            a, s_new = _rwkv_layer(y, mods, w, state_wkv[:, j])
            new_wkv.append(s_new)
            wo = w["wo"]
        else:
            a = _pool_call(y, mods, pool_w[j].astype(BF16), pool_scale[j].reshape(1, D))
        if isinstance(a, tuple):
            ffn_w = [w_[layer:layer + 1].astype(BF16) for w_ in (ffn_wg, ffn_wu, ffn_wd)]
            y = _post_call(y, a, mods, ln_g[layer], ln_b[layer], wo, *ffn_w, 0)
        else:
            y = _post_call(y, a, mods, ln_g[layer], ln_b[layer], wo, ffn_wg, ffn_wu, ffn_wd, layer,
                           split_out=layer == DEPTH - 1)
    return (y[0].reshape(BATCH, SEQ, D), y[1].reshape(DEC_BATCH, DEC_SEQ, D),
            jnp.stack(new_ckv, axis=1), jnp.stack(new_kpe, axis=1), jnp.stack(new_wkv, axis=1))
```

```python
import functools

import jax
import jax.numpy as jnp
import numpy as np
from jax import lax
from jax.experimental import pallas as pl
from jax.experimental.pallas import tpu as pltpu

F32 = jnp.float32
BF16 = jnp.bfloat16

D = 1024
BATCH, SEQ = 32, 256
DEC_BATCH, DEC_SEQ = 2, 4096
PAST = 512
DEPTH = 4
GRID_W = 64
HEADS = 16
Q_LORA, KV_LORA = 384, 256
NOPE, ROPE, VDIM = 64, 32, 64
ROPE_THETA = 10000.0
RWKV_N = 64
RWKV_GN_EPS = 64e-5
POOL_WINDOWS = (2, 4, 8, 16)
POOL_C = D // 4
D_FF = 2816
ALPHA = (2 * DEPTH) ** 0.25
LN_EPS = 1e-5
RMS_EPS = 1e-6
LOG2_E = 1.4426950408889634
EXP_M_HALF = 0.6065306597126334

N_PROMPT = BATCH * SEQ
N_SAMPLE = DEC_BATCH * DEC_SEQ
N_TOK = N_PROMPT + N_SAMPLE

LANES = 128
SUBLANES = 8
HEAD_PAD = 128
VMEM_LIMIT = 56 * 1024 * 1024

TM = 512
TMIX = 256
HALO = SUBLANES
TQ = 256
ATTN_HEADS = 4
ATTN_KC = 512
ATTN_SKEW = 2
ATTN_GATE = 3
CHUNK = 64
SCAN_SEQS = 2
TFF = 256
POST_TM = 512
POST_SUB = 512
POST_PRO_AT = 5
POST_EPI_AT = 2


def _cparams(sem):
    return pltpu.CompilerParams(dimension_semantics=sem, vmem_limit_bytes=VMEM_LIMIT)


def _dot(a, b):
    return jnp.dot(a, b, preferred_element_type=F32)


def _dot_nt(a, b):
    return lax.dot_general(a, b, (((1,), (1,)), ((), ())), preferred_element_type=F32)


def _dot_tn(a, b):
    return lax.dot_general(a, b, (((0,), (0,)), ((), ())), preferred_element_type=F32)


def _layer_norm(x, g, b):
    mu = jnp.mean(x, axis=-1, keepdims=True)
    xc = x - mu
    var = jnp.mean(xc * xc, axis=-1, keepdims=True)
    return xc * lax.rsqrt(var + LN_EPS) * g + b


def _rms_norm(x, g):
    return x * lax.rsqrt(jnp.mean(x * x, axis=-1, keepdims=True) + RMS_EPS) * g


def _sigmoid(x):
    return 0.5 * jnp.tanh(0.5 * x) + 0.5


def _silu(x):
    return x * _sigmoid(x)


def _cond_of_tile(i, tile):
    first_sample = N_PROMPT // tile
    per_seq = DEC_SEQ // tile
    return jnp.where(i < first_sample, 0, 1 + (i - first_sample) // per_seq)


def _split3(x):
    hi = x.astype(BF16)
    r1 = x - hi.astype(F32)
    mid = r1.astype(BF16)
    lo = (r1 - mid.astype(F32)).astype(BF16)
    return hi, mid, lo


def _seg_sum(x, ones_bd):
    outs = []
    for g in range(D // 256):
        hi, mid, lo = _split3(x[:, 256 * g:256 * (g + 1)])
        outs.append(_dot(hi, ones_bd) + _dot(mid, ones_bd) + _dot(lo, ones_bd))
    return jnp.concatenate(outs, axis=1)


def _ada_kernel(c_ref, w_ref, b_ref, o_ref):
    a = _silu(c_ref[...]).astype(BF16)
    o_ref[...] = _dot(a, w_ref[...].astype(BF16)) + b_ref[...]


def _ada_call(cond8, ada_w, ada_b):
    tn = 1536
    return pl.pallas_call(
        _ada_kernel,
        grid=(DEPTH, 6 * D // tn),
        in_specs=[
            pl.BlockSpec((SUBLANES, D), lambda l, n: (0, 0)),
            pl.BlockSpec((None, D, tn), lambda l, n: (l, 0, n)),
            pl.BlockSpec((None, 1, tn), lambda l, n: (l, 0, n)),
        ],
        out_specs=pl.BlockSpec((None, SUBLANES, tn), lambda l, n: (l, 0, n)),
        out_shape=jax.ShapeDtypeStruct((DEPTH, SUBLANES, 6 * D), F32),
        compiler_params=_cparams(("parallel", "parallel")),
        name="adaln",
    )(cond8, ada_w, ada_b.reshape(DEPTH, 1, 6 * D))


def _act_specs(y, tile):
    if not isinstance(y, tuple):
        return [pl.BlockSpec((tile, D), lambda i: (i, 0))], [y]
    n_p = N_PROMPT // tile
    return [pl.BlockSpec((tile, D), lambda i: (jnp.minimum(i, n_p - 1), 0)),
            pl.BlockSpec((tile, D), lambda i: (jnp.maximum(i - n_p, 0), 0))], list(y)


def _act_tile(y_refs, tile):
    if len(y_refs) == 1:
        return y_refs[0][...]
    return jnp.where(pl.program_id(0) < N_PROMPT // tile, y_refs[0][...], y_refs[1][...])


def _mla_pre_kernel(n_y, *refs):
    m_ref, wa_ref, qn_ref, wqb_ref, kvn_ref, cos_ref, sin_ref, q_ref, ckv_ref, kpe_ref = refs[n_y:]
    i = pl.program_id(0)
    h = (_act_tile(refs[:n_y], TM) * (1.0 + m_ref[1:2, :]) + m_ref[0:1, :]).astype(BF16)
    x = _dot(h, wa_ref[...])
    qa = _rms_norm(x[:, :Q_LORA], qn_ref[...]).astype(BF16)
    ckv_ref[...] = _rms_norm(x[:, Q_LORA:Q_LORA + KV_LORA], kvn_ref[...])
    kpe = x[:, Q_LORA + KV_LORA:]
    q = _dot(qa, wqb_ref[...])
    is_latent = i >= N_PROMPT // TM
    swap_shift = HEAD_PAD - ROPE

    @pl.when(jnp.logical_not(is_latent))
    def _():
        q_ref[...] = q.astype(BF16)
        lane = lax.broadcasted_iota(jnp.int32, kpe.shape, 1)
        kpe_ref[...] = jnp.where(lane < NOPE + ROPE, kpe, 0.0)

    @pl.when(is_latent)
    def _():
        cos = cos_ref[...]
        sin = sin_ref[...]
        kpe_ref[...] = kpe * cos + pltpu.roll(kpe, swap_shift, 1) * sin
        for hd in range(HEADS):
            sl = slice(HEAD_PAD * hd, HEAD_PAD * (hd + 1))
            q_ref[:, sl] = (q[:, sl] * cos + pltpu.roll(q[:, sl], swap_shift, 1) * sin).astype(BF16)


def _mla_pre_call(y, mods, w):
    first_sample = N_PROMPT // TM
    per_seq = DEC_SEQ // TM
    const = lambda i: (0, 0)
    rope_map = lambda i: (jnp.where(i < first_sample, 0, (i - first_sample) % per_seq), 0)
    y_specs, y_args = _act_specs(y, TM)
    return pl.pallas_call(
        functools.partial(_mla_pre_kernel, len(y_args)),
        grid=(N_TOK // TM,),
        in_specs=y_specs + [
            pl.BlockSpec((None, SUBLANES, D), lambda i: (_cond_of_tile(i, TM), 0, 0)),
            pl.BlockSpec((D, Q_LORA + KV_LORA + HEAD_PAD), const),
            pl.BlockSpec((1, Q_LORA), const),
            pl.BlockSpec((Q_LORA, HEADS * HEAD_PAD), const),
            pl.BlockSpec((1, KV_LORA), const),
            pl.BlockSpec((TM, HEAD_PAD), rope_map),
            pl.BlockSpec((TM, HEAD_PAD), rope_map),
        ],
        out_specs=[
            pl.BlockSpec((TM, HEADS * HEAD_PAD), lambda i: (i, 0)),
            pl.BlockSpec((TM, KV_LORA), lambda i: (i, 0)),
            pl.BlockSpec((TM, HEAD_PAD), lambda i: (i, 0)),
        ],
        out_shape=[
            jax.ShapeDtypeStruct((N_TOK, HEADS * HEAD_PAD), BF16),
            jax.ShapeDtypeStruct((N_TOK, KV_LORA), F32),
            jax.ShapeDtypeStruct((N_TOK, HEAD_PAD), F32),
        ],
        compiler_params=_cparams(("parallel",)),
        name="mla_pre",
    )(*y_args, mods, w["w_a"], w["q_norm"], w["wq_b"], w["kv_norm"], w["cos"], w["sin"])


KV_TILES_PER_SEQ = (PAST + DEC_SEQ) // TM
KV_ROWS = DEC_BATCH * (PAST + DEC_SEQ) + N_PROMPT


def _kv_tile_source(i):
    n_lat = DEC_BATCH * KV_TILES_PER_SEQ
    b = jnp.minimum(i // KV_TILES_PER_SEQ, DEC_BATCH - 1)
    j = i - b * KV_TILES_PER_SEQ
    is_cache = (i < n_lat) & (j < PAST // TM)
    new_tile = N_PROMPT // TM + b * (DEC_SEQ // TM) + jnp.maximum(j - PAST // TM, 0)
    return is_cache, b * (PAST // TM) + jnp.minimum(j, PAST // TM - 1), jnp.where(i < n_lat, new_tile, i - n_lat)


def _kv_expand_kernel(cckv_ref, ckpe_ref, ckv_ref, kpe_ref, wk_ref, wv_ref, vone_ref, k_ref, v_ref):
    is_cache, _, _ = _kv_tile_source(pl.program_id(0))
    c = jnp.where(is_cache, cckv_ref[...], ckv_ref[...]).astype(BF16)
    kpe = jnp.where(is_cache, ckpe_ref[...], kpe_ref[...])
    kn = _dot(c, wk_ref[...])
    for hd in range(HEADS):
        sl = slice(HEAD_PAD * hd, HEAD_PAD * (hd + 1))
        k_ref[:, sl] = (kn[:, sl] + kpe).astype(BF16)
    v_ref[...] = (_dot(c, wv_ref[...]) + vone_ref[...]).astype(BF16)


def _kv_expand_call(cache_ckv, cache_kpe, ckv, kpe, w):
    rows = DEC_BATCH * (PAST + DEC_SEQ)
    const = lambda i: (0, 0)
    return pl.pallas_call(
        _kv_expand_kernel,
        grid=(rows // TM,),
        in_specs=[
            pl.BlockSpec((TM, KV_LORA), lambda i: (_kv_tile_source(i)[1], 0)),
            pl.BlockSpec((TM, HEAD_PAD), lambda i: (_kv_tile_source(i)[1], 0)),
            pl.BlockSpec((TM, KV_LORA), lambda i: (_kv_tile_source(i)[2], 0)),
            pl.BlockSpec((TM, HEAD_PAD), lambda i: (_kv_tile_source(i)[2], 0)),
            pl.BlockSpec((KV_LORA, HEADS * HEAD_PAD), const),
            pl.BlockSpec((KV_LORA, HEADS * HEAD_PAD), const),
            pl.BlockSpec((1, HEADS * HEAD_PAD), const),
        ],
        out_specs=[
            pl.BlockSpec((TM, HEADS * HEAD_PAD), lambda i: (i, 0)),
            pl.BlockSpec((TM, HEADS * HEAD_PAD), lambda i: (i, 0)),
        ],
        out_shape=[jax.ShapeDtypeStruct((rows, HEADS * HEAD_PAD), BF16)] * 2,
        compiler_params=_cparams(("parallel",)),
        name="kv_expand",
    )(cache_ckv, cache_kpe, ckv, kpe, w["wkv_b_k"], w["wkv_b_v"], w["v_one"])


def _sum_lane(h):
    return HEAD_PAD - 1 if h % 2 == 0 else 0


_V_ONE = np.zeros((1, HEADS * HEAD_PAD), np.float32)
_V_ONE[0, [HEAD_PAD * h + _sum_lane(h) for h in range(HEADS)]] = 1.0


def _zero_like_col(x):
    bits = pltpu.bitcast(x, jnp.uint32)
    zero = lax.shift_right_logical(lax.shift_right_logical(bits, jnp.uint32(16)), jnp.uint32(16))
    return pltpu.bitcast(zero, F32)[:, 0:1]


def _attn_kernel(t_k, fused_kv, q_ref, *rest):
    o_ref = rest[-1]
    scale2 = (NOPE + ROPE) ** -0.5 * LOG2_E
    kc = min(ATTN_KC, t_k)
    chunks = [slice(c * kc, (c + 1) * kc) for c in range(t_k // kc)]
    n_heads = q_ref.shape[1] // HEAD_PAD
    head = [slice(HEAD_PAD * e, HEAD_PAD * (e + 1)) for e in range(n_heads)]
    if fused_kv:
        ckv_ref, kpe_ref, wk_ref, wv_ref, vone_ref = rest[:5]
        c = ckv_ref[...].astype(BF16)
        kn = _dot(c, wk_ref[...])
        kpe = kpe_ref[...]
        k_ref = jnp.concatenate([(kn[:, head[e]] + kpe).astype(BF16) for e in range(n_heads)], axis=1)
        v_ref = (_dot(c, wv_ref[...]) + vone_ref[...]).astype(BF16)
    else:
        k_ref, v_ref = rest[:2]

    def scores(e, c):
        return _dot_nt(q_ref[:, head[e]], k_ref[c, head[e]])

    lane = lax.broadcasted_iota(jnp.int32, (TQ, HEAD_PAD), 1)

    def normalised(e, acc):
        sl = _sum_lane(e)
        return jnp.where(lane == sl, 0.0, acc) * (1.0 / acc[:, sl:sl + 1])

    def store(outs):
        for j in range(n_heads // 2):
            o_ref[:, 2 * VDIM * j:2 * VDIM * (j + 1)] = (outs[2 * j] + outs[2 * j + 1]).astype(BF16)

    if len(chunks) == 1:
        s_all = [scores(e, chunks[0]) for e in range(n_heads)]
        p_all = [jnp.exp2((s - jnp.max(s, axis=-1, keepdims=True)) * scale2).astype(BF16) for s in s_all]
        store([normalised(e, _dot(p_all[e], v_ref[:, head[e]])) for e in range(n_heads)])
        return

    s_cur = [scores(0, c) for c in chunks]
    outs = []
    for e in range(n_heads):
        m = functools.reduce(jnp.maximum, [jnp.max(s, axis=-1, keepdims=True) for s in s_cur])
        s_next, accs = [], []
        for i in range(len(chunks) + ATTN_SKEW):
            if e + 1 < n_heads and i < len(chunks):
                s_next.append(scores(e + 1, chunks[i]))
            if i >= ATTN_SKEW:
                j = i - ATTN_SKEW
                mj = m + _zero_like_col(accs[j - ATTN_GATE]) if j >= ATTN_GATE else m
                p = jnp.exp2((s_cur[j] - mj) * scale2).astype(BF16)
                pv = _dot(p, v_ref[chunks[j], head[e]])
                accs.append(pv if not accs else accs[-1] + pv)
        outs.append(normalised(e, accs[-1]))
        s_cur = s_next
    store(outs)


def _attn_call(q, k, v, o_buf, n_seq, t_q, t_k, q_row0, k_row0, name):
    nq = t_q // TQ
    q_blk0 = q_row0 // TQ
    k_blk0 = k_row0 // t_k
    heads = ATTN_HEADS if t_k > ATTN_KC else HEADS
    width = heads * HEAD_PAD
    fused_kv = v is None
    in_specs = [pl.BlockSpec((TQ, width), lambda b, p, i: (q_blk0 + b * nq + i, p))]
    if fused_kv:
        const = lambda b, p, i: (0, 0)
        in_specs += [
            pl.BlockSpec((t_k, KV_LORA), lambda b, p, i: (k_blk0 + b, 0)),
            pl.BlockSpec((t_k, HEAD_PAD), lambda b, p, i: (k_blk0 + b, 0)),
            pl.BlockSpec((KV_LORA, HEADS * HEAD_PAD), const),
            pl.BlockSpec((KV_LORA, HEADS * HEAD_PAD), const),
            pl.BlockSpec((1, HEADS * HEAD_PAD), const),
        ]
        args = [q, *k]
    else:
        in_specs += [
            pl.BlockSpec((t_k, width), lambda b, p, i: (k_blk0 + b, p)),
            pl.BlockSpec((t_k, width), lambda b, p, i: (k_blk0 + b, p)),
        ]
        args = [q, k, v]
    if o_buf is not None:
        in_specs.append(pl.BlockSpec(memory_space=pl.ANY))
        args.append(o_buf)
    return pl.pallas_call(
        functools.partial(_attn_kernel, t_k, fused_kv),
        grid=(n_seq, HEADS // heads, nq),
        in_specs=in_specs,
        out_specs=pl.BlockSpec((TQ, heads * VDIM), lambda b, p, i: (q_blk0 + b * nq + i, p)),
        out_shape=jax.ShapeDtypeStruct((N_TOK, HEADS * VDIM), BF16),
        input_output_aliases={} if o_buf is None else {3: 0},
        compiler_params=_cparams(("parallel", "parallel", "arbitrary")),
        name=name,
    )(*args)


def _rwkv_gate(rows, yf_ref, yb_ref, bonus_ref, g_ref, lg_ref, lb_ref, ones_ref):
    ones = ones_ref[...]
    y = yf_ref[rows, :] + yb_ref[rows, :]
    mu = _seg_sum(y, ones) * (1.0 / RWKV_N)
    yc = y - mu
    var = _seg_sum(yc * yc, ones) * (1.0 / RWKV_N)
    yn = yc * lax.rsqrt(var + RWKV_GN_EPS) * lg_ref[...] + lb_ref[...]
    return ((yn + bonus_ref[rows, :]) * g_ref[rows, :]).astype(BF16)


def _post_kernel(tile, n_y, n_mix, has_proj, split_out, *refs):
    y_refs = refs[:n_y]
    mix_refs = refs[n_y:n_y + n_mix]
    m_ref, lng_ref, lnb_ref = refs[n_y + n_mix:n_y + n_mix + 3]
    rest = refs[n_y + n_mix + 3:]
    if has_proj:
        wo_ref, rest = rest[0], rest[1:]
    wg_ref, wu_ref, wd_ref = rest[:3]
    o_refs = rest[3:]
    is_prompt = pl.program_id(0) < N_PROMPT // tile
    blocks = [slice(POST_SUB * s, POST_SUB * (s + 1)) for s in range(tile // POST_SUB)]
    n_chunks = D_FF // TFF

    def prologue(rows):
        a = mix_refs[0][rows, :] if n_mix == 1 else _rwkv_gate(rows, *mix_refs)
        mix = _dot(a, wo_ref[...]) if has_proj else a
        y = y_refs[0][rows, :] if n_y == 1 else jnp.where(is_prompt, y_refs[0][rows, :], y_refs[1][rows, :])
        y1 = _layer_norm(ALPHA * y + m_ref[2:3, :] * mix, lng_ref[0:1, :], lnb_ref[0:1, :])
        return y1, (y1 * (1.0 + m_ref[4:5, :]) + m_ref[3:4, :]).astype(BF16)

    def ffn_chunk(h, j):
        cols = slice(TFF * j, TFF * (j + 1))
        act = (_silu(_dot(h, wg_ref[:, cols].astype(BF16))) * _dot(h, wu_ref[:, cols].astype(BF16))).astype(BF16)
        return _dot(act, wd_ref[cols, :].astype(BF16))

    def epilogue(rows, y1, acc):
        out = _layer_norm(ALPHA * y1 + m_ref[5:6, :] * acc, lng_ref[1:2, :], lnb_ref[1:2, :])
        if not split_out:
            o_refs[0][rows, :] = out
        else:
            @pl.when(is_prompt)
            def _():
                o_refs[0][rows, :] = out

            @pl.when(jnp.logical_not(is_prompt))
            def _():
                o_refs[1][rows, :] = out

    state = [None] * len(blocks)
    state[0] = prologue(blocks[0])
    pending = None
    for s, rows in enumerate(blocks):
        y1, h = state[s]
        acc = None
        for j in range(n_chunks):
            part = ffn_chunk(h, j)
            acc = part if acc is None else acc + part
            if j == POST_PRO_AT and s + 1 < len(blocks):
                state[s + 1] = prologue(blocks[s + 1])
            if j == POST_EPI_AT and pending is not None:
                epilogue(*pending)
                pending = None
        pending = (rows, y1, acc)
    epilogue(*pending)


def _post_call(y, a, mods, ln_g, ln_b, wo, wg, wu, wd, layer, split_out=False):
    has_proj = wo is not None
    const = lambda i: (0, 0)
    this_layer = lambda i: (layer, 0, 0)
    resident = pl.Buffered(1)
    tile = POST_SUB if isinstance(a, tuple) else POST_TM
    y_specs, y_args = _act_specs(y, tile)
    row = pl.BlockSpec((tile, D), lambda i: (i, 0))
    if isinstance(a, tuple):
        mix_specs = [row] * 4 + [pl.BlockSpec((1, D), const)] * 2 + [pl.BlockSpec((256, 256), const)]
        mix_args = list(a)
    else:
        mix_specs, mix_args = [row], [a]
    in_specs = y_specs + mix_specs + [
        pl.BlockSpec((None, SUBLANES, D), lambda i: (_cond_of_tile(i, tile), 0, 0)),
        pl.BlockSpec((2, D), const),
        pl.BlockSpec((2, D), const),
    ]
    args = y_args + mix_args + [mods, ln_g, ln_b]
    if has_proj:
        in_specs.append(pl.BlockSpec((D, D), const, pipeline_mode=resident))
        args.append(wo)
    in_specs += [
        pl.BlockSpec((None, D, D_FF), this_layer, pipeline_mode=resident),
        pl.BlockSpec((None, D, D_FF), this_layer, pipeline_mode=resident),
        pl.BlockSpec((None, D_FF, D), this_layer, pipeline_mode=resident),
    ]
    args += [wg, wu, wd]
    if split_out:
        n_p = N_PROMPT // tile
        out_specs = [pl.BlockSpec((tile, D), lambda i: (jnp.minimum(i, n_p - 1), 0)),
                     pl.BlockSpec((tile, D), lambda i: (jnp.maximum(i - n_p, 0), 0))]
        out_shape = [jax.ShapeDtypeStruct((N_PROMPT, D), F32), jax.ShapeDtypeStruct((N_SAMPLE, D), F32)]
    else:
        out_specs = pl.BlockSpec((tile, D), lambda i: (i, 0))
        out_shape = jax.ShapeDtypeStruct((N_TOK, D), F32)
    return pl.pallas_call(
        functools.partial(_post_kernel, tile, len(y_args), len(mix_args), has_proj, split_out),
        grid=(N_TOK // tile,),
        in_specs=in_specs,
        out_specs=out_specs,
        out_shape=out_shape,
        compiler_params=_cparams(("arbitrary",) if split_out else ("parallel",)),
        name="post_proj" if has_proj else "post_noproj",
    )(*args)


def _mix_tile_flags(i):
    first_sample = N_PROMPT // TMIX
    tiles_prompt = SEQ // TMIX
    tiles_sample = DEC_SEQ // TMIX
    is_p = i < first_sample
    j = jnp.where(is_p, i % tiles_prompt, (i - first_sample) % tiles_sample)
    n = jnp.where(is_p, tiles_prompt, tiles_sample)
    return j > 0, j < n - 1, j * TMIX, n * TMIX


def _halo_specs():
    blocks_per_tile = TMIX // HALO
    last = N_TOK // HALO - 1
    return [
        pl.BlockSpec((HALO, D), lambda i: (jnp.maximum(i * blocks_per_tile - 1, 0), 0)),
        pl.BlockSpec((TMIX, D), lambda i: (i, 0)),
        pl.BlockSpec((HALO, D), lambda i: (jnp.minimum((i + 1) * blocks_per_tile, last), 0)),
    ]


EXT = TMIX + 2 * HALO


def _ext_tile(yp_ref, y_ref, yn_ref, m_ref, has_prev, has_next):
    scale1 = 1.0 + m_ref[1:2, :]
    shift = m_ref[0:1, :]
    h = y_ref[...] * scale1 + shift
    before = jnp.where(has_prev, yp_ref[...] * scale1 + shift, 0.0)
    after = jnp.where(has_next, yn_ref[...] * scale1 + shift, 0.0)
    return h, jnp.concatenate([before, h, after], axis=0)


def _shifted(ext, j):
    if j == 0:
        return ext[HALO:HALO + TMIX]
    return pltpu.roll(ext, (-j) % EXT, 0)[HALO:HALO + TMIX]


def _rwkv_pre_kernel(yp_ref, y_ref, yn_ref, m_ref, mu_ref, wr_ref, wk_ref, wv_ref, g1_ref, g2_ref,
                     w1_ref, w2_ref, a1_ref, a2_ref, w0_ref, a0_ref, kk_ref, ka_ref, rk_ref, ones_ref,
                     r_out, v_out, g_out, kk_out, kd_out, b_out, ld_out, bonus_out):
    i = pl.program_id(0)
    has_prev, has_next, _, _ = _mix_tile_flags(i)
    h, ext = _ext_tile(yp_ref, y_ref, yn_ref, m_ref, has_prev, has_next)
    xx = 0.5 * (_shifted(ext, -1) + _shifted(ext, 1)) - h

    def mix(j):
        return (h + xx * mu_ref[j:j + 1, :]).astype(BF16)

    r = _dot(mix(0), wr_ref[...])
    r_out[...] = r
    k = _dot(mix(2), wk_ref[...])
    v = _dot(mix(3), wv_ref[...])
    v_out[...] = v.astype(BF16)
    g_out[...] = _dot(_sigmoid(_dot(mix(5), g1_ref[...])).astype(BF16), g2_ref[...])
    tw = jnp.tanh(_dot(mix(1), w1_ref[...])).astype(BF16)
    ta = _dot(mix(4), a1_ref[...]).astype(BF16)

    kkf = k * kk_ref[...]
    kk = kkf * jnp.minimum(lax.rsqrt(_seg_sum(kkf * kkf, ones_ref[...])), 1e12)
    kk_out[...] = kk
    kd_sum = None
    for d in range(2):
        ld_out[d] = -EXP_M_HALF * _sigmoid(w0_ref[d:d + 1, :] + _dot(tw, w2_ref[d]))
        a = _sigmoid(a0_ref[d:d + 1, :] + _dot(ta, a2_ref[d]))
        kd = k * (1.0 + (a - 1.0) * ka_ref[...])
        kd_out[d] = kd
        b_out[d] = kk * a
        kd_sum = kd if kd_sum is None else kd_sum + kd
    bonus_out[...] = _seg_sum(r * kd_sum * rk_ref[...], ones_ref[...]) * v


def _rwkv_pre_call(y, mods, w):
    const2 = lambda i: (0, 0)
    const3 = lambda i: (0, 0, 0)
    row = pl.BlockSpec((TMIX, D), lambda i: (i, 0))
    row2 = pl.BlockSpec((2, TMIX, D), lambda i: (0, i, 0))
    one = jax.ShapeDtypeStruct((N_TOK, D), F32)
    two = jax.ShapeDtypeStruct((2, N_TOK, D), F32)
    return pl.pallas_call(
        _rwkv_pre_kernel,
        grid=(N_TOK // TMIX,),
        in_specs=_halo_specs() + [
            pl.BlockSpec((None, SUBLANES, D), lambda i: (_cond_of_tile(i, TMIX), 0, 0)),
            pl.BlockSpec((6, D), const2),
            pl.BlockSpec((D, D), const2), pl.BlockSpec((D, D), const2), pl.BlockSpec((D, D), const2),
            pl.BlockSpec((D, 128), const2), pl.BlockSpec((128, D), const2),
            pl.BlockSpec((D, 128), const2), pl.BlockSpec((2, 128, D), const3),
            pl.BlockSpec((D, 128), const2), pl.BlockSpec((2, 128, D), const3),
            pl.BlockSpec((2, D), const2), pl.BlockSpec((2, D), const2),
            pl.BlockSpec((1, D), const2), pl.BlockSpec((1, D), const2), pl.BlockSpec((1, D), const2),
            pl.BlockSpec((256, 256), const2),
        ],
        out_specs=[row, row, row, row, row2, row2, row2, row],
        out_shape=[one, jax.ShapeDtypeStruct((N_TOK, D), BF16), one, one, two, two, two, one],
        compiler_params=_cparams(("parallel",)),
        name="rwkv_pre",
    )(y, y, y, mods, w["mu"], w["wr"], w["wk"], w["wv"], w["g1"], w["g2"], w["w1"], w["w2"],
      w["a1"], w["a2"], w["w0"], w["a0"], w["k_k"], w["k_a"], w["r_k"], w["ones_bd"])


def _scan_kernel(n_chunks, has_s0, aliased, *refs):
    ins = [refs[0:6], refs[6:12]]
    pos = 12
    s0_ref = refs[pos] if has_s0 else None
    pos += (1 if has_s0 else 0) + (2 if aliased else 0)
    y_refs = refs[pos:pos + 2]
    sf_ref, s_ref = refs[pos + 2], refs[pos + 3]
    n = pl.program_id(1)
    C = CHUNK
    P2 = 2 * C
    streams = [(s, d) for s in range(SCAN_SEQS) for d in range(2)]
    units = [(q, p) for q in range(len(streams)) for p in range(HEADS // 2)]
    lanes = [slice(2 * RWKV_N * p, 2 * RWKV_N * (p + 1)) for p in range(HEADS // 2)]

    @pl.when(n == 0)
    def _():
        s_ref[...] = jnp.zeros_like(s_ref)
        if has_s0:
            for q, (s, d) in enumerate(streams):
                for hd in range(HEADS):
                    o = RWKV_N * (hd % 2)
                    s_ref[q, hd // 2, o:o + RWKV_N, o:o + RWKV_N] = s0_ref[s, d, hd]

    lane = lax.broadcasted_iota(jnp.int32, (C, 2 * RWKV_N), 1)
    even = lane < RWKV_N
    ti = lax.broadcasted_iota(jnp.int32, (C, C), 0)
    si = lax.broadcasted_iota(jnp.int32, (C, C), 1)
    r2 = lax.broadcasted_iota(jnp.int32, (P2, P2), 0)
    c2 = lax.broadcasted_iota(jnp.int32, (P2, P2), 1)
    same = (r2 // C) == (c2 // C)
    eye = (r2 == c2).astype(F32)

    def same_block(size):
        return (r2 // size) == (c2 // size)

    leaf = same_block(2)
    merges = [same_block(2 * size) & jnp.logical_not(same_block(size)) for size in (2, 4, 8, 16, 32)]

    def split(x):
        return jnp.concatenate([jnp.where(even, x, 0.0), jnp.where(even, 0.0, x)], axis=0)

    def twice(x):
        return jnp.concatenate([x, x], axis=0)

    cum, m_incl, m_strict = [None, None], [None, None], [None, None]
    for d in range(2):
        sign = 1 - 2 * d
        cum[d] = (sign * (si - ti) <= 0).astype(BF16)
        order = sign * (c2 % C - r2 % C)
        m_incl[d] = same & (order <= 0)
        m_strict[d] = same & (order < 0)
    at, rt, bt, kt, bh, kh, pc, v = ([None] * len(streams) for _ in range(8))
    for q, (s, d) in enumerate(streams):
        r_ref, v_ref, kk_ref, kd_ref, b_ref, ld_ref = ins[d]
        ld = ld_ref[s]
        ld_hi, ld_mid, ld_lo = _split3(ld)
        cs = _dot(cum[d], ld_hi) + _dot(cum[d], ld_mid) + _dot(cum[d], ld_lo)
        last = cs[0:1, :] if d == 1 else cs[C - 1:C, :]
        pin = jnp.exp(cs)
        pinv = jnp.exp(-cs)
        at[q] = -kk_ref[s] * jnp.exp(cs - ld)
        rt[q] = r_ref[s] * pin
        bt[q] = b_ref[s] * pinv
        kt[q] = kd_ref[s] * pinv
        pc[q] = jnp.exp(last)
        tail = pc[q] * pinv
        bh[q] = b_ref[s] * tail
        kh[q] = kd_ref[s] * tail
        v[q] = v_ref[s]
    m_incl = [m_incl[d] for _, d in streams]
    m_strict = [m_strict[d] for _, d in streams]

    s0 = [s_ref[q, p] for q, p in units]
    lhs = [jnp.concatenate([split(at[q][:, lanes[p]]), split(rt[q][:, lanes[p]])], axis=0).astype(BF16)
           for q, p in units]
    rhs = [jnp.concatenate([twice(bt[q][:, lanes[p]]), twice(kt[q][:, lanes[p]])], axis=0).astype(BF16)
           for q, p in units]
    v2b = [split(v[q][:, lanes[p]]).astype(BF16) for q, p in units]
    idx = range(len(units))
    g = [_dot_nt(lhs[u], rhs[u]) for u in idx]
    xs = [_dot_nt(lhs[u], s0[u].astype(BF16)) for u in idx]
    a_ab = [jnp.where(m_strict[units[u][0]], g[u][:P2, :P2], 0.0) for u in idx]
    inv = [eye + jnp.where(leaf, a_ab[u], 0.0) for u in idx]
    for mk in merges:
        invb = [inv[u].astype(BF16) for u in idx]
        binv = [_dot(jnp.where(mk, a_ab[u], 0.0).astype(BF16), invb[u]).astype(BF16) for u in idx]
        inv = [inv[u] + _dot(invb[u], binv[u]) for u in idx]
    a_ak = [jnp.where(m_strict[units[u][0]], g[u][:P2, P2:], 0.0).astype(BF16) for u in idx]
    rhs_u = [(xs[u][:P2] + _dot(a_ak[u], v2b[u])).astype(BF16) for u in idx]
    u2b = [_dot(inv[u].astype(BF16), rhs_u[u]).astype(BF16) for u in idx]
    m_r = [jnp.concatenate([jnp.where(m_incl[units[u][0]], g[u][P2:, :P2], 0.0),
                            jnp.where(m_incl[units[u][0]], g[u][P2:, P2:], 0.0)], axis=1).astype(BF16)
           for u in idx]
    uv = [jnp.concatenate([u2b[u], v2b[u]], axis=0) for u in idx]
    for u, (q, p) in enumerate(units):
        s, d = streams[q]
        y2 = xs[u][P2:] + _dot(m_r[u], uv[u])
        y_refs[d][s, :, lanes[p]] = y2[:C] + y2[C:]
    for u, (q, p) in enumerate(units):
        bk = jnp.concatenate([split(bh[q][:, lanes[p]]), split(kh[q][:, lanes[p]])], axis=0).astype(BF16)
        s_ref[q, p] = s0[u] * pc[q][:, lanes[p]] + _dot_tn(uv[u], bk)

    @pl.when(n == n_chunks - 1)
    def _():
        for q, (s, d) in enumerate(streams):
            for hd in range(HEADS):
                o = RWKV_N * (hd % 2)
                sf_ref[s, d, hd] = s_ref[q, hd // 2, o:o + RWKV_N, o:o + RWKV_N]


def _scan_call(r, v, kk, kd, b, ld, s0, y_bufs, n_seq, t_seq, row0, name):
    n_chunks = t_seq // CHUNK
    rows3 = (N_TOK // t_seq, t_seq, D)
    blk0 = row0 // t_seq // SCAN_SEQS
    chunk = [lambda nn: nn, lambda nn: n_chunks - 1 - nn]
    in_specs, args = [], []
    for d in range(2):
        row = pl.BlockSpec((SCAN_SEQS, CHUNK, D), lambda bb, nn, d=d: (blk0 + bb, chunk[d](nn), 0))
        row2 = pl.BlockSpec((None, SCAN_SEQS, CHUNK, D), lambda bb, nn, d=d: (d, blk0 + bb, chunk[d](nn), 0))
        in_specs += [row, row, row, row2, row2, row2]
        args += [r.reshape(rows3), v.reshape(rows3), kk.reshape(rows3),
                 kd.reshape(2, *rows3), b.reshape(2, *rows3), ld.reshape(2, *rows3)]
    st = pl.BlockSpec((SCAN_SEQS, 2, HEADS, RWKV_N, RWKV_N), lambda bb, nn: (bb, 0, 0, 0, 0))
    if s0 is not None:
        in_specs.append(st)
        args.append(s0)
    aliases = {}
    if y_bufs is not None:
        aliases = {len(args): 0, len(args) + 1: 1}
        in_specs += [pl.BlockSpec(memory_space=pl.ANY)] * 2
        args += [y.reshape(rows3) for y in y_bufs]
    out = pl.pallas_call(
        functools.partial(_scan_kernel, n_chunks, s0 is not None, y_bufs is not None),
        grid=(n_seq // SCAN_SEQS, n_chunks),
        in_specs=in_specs,
        out_specs=[
            pl.BlockSpec((SCAN_SEQS, CHUNK, D), lambda bb, nn: (blk0 + bb, chunk[0](nn), 0)),
            pl.BlockSpec((SCAN_SEQS, CHUNK, D), lambda bb, nn: (blk0 + bb, chunk[1](nn), 0)),
            st,
        ],
        out_shape=[
            jax.ShapeDtypeStruct(rows3, F32),
            jax.ShapeDtypeStruct(rows3, F32),
            jax.ShapeDtypeStruct((n_seq, 2, HEADS, RWKV_N, RWKV_N), F32),
        ],
        scratch_shapes=[pltpu.VMEM((2 * SCAN_SEQS, HEADS // 2, 2 * RWKV_N, 2 * RWKV_N), F32)],
        input_output_aliases=aliases,
        compiler_params=_cparams(("parallel", "arbitrary")),
        name=name,
    )(*args)
    return (out[0].reshape(N_TOK, D), out[1].reshape(N_TOK, D)), out[2]


def _pool_kernel(yp_ref, y_ref, yn_ref, m_ref, w_ref, sc_ref, o_ref):
    i = pl.program_id(0)
    has_prev, has_next, pos0, seq_len = _mix_tile_flags(i)
    h, ext = _ext_tile(yp_ref, y_ref, yn_ref, m_ref, has_prev, has_next)
    t = pos0 + lax.broadcasted_iota(jnp.int32, (TMIX, 1), 0)
    for gi, win in enumerate(POOL_WINDOWS):
        cols = slice(POOL_C * gi, POOL_C * (gi + 1))
        ext_g = ext[:, cols]
        acc = None
        for j in range(-(win // 2), win - win // 2):
            x = _shifted(ext_g, j)
            acc = x if acc is None else acc + x
        lo = jnp.maximum(t - win // 2, 0)
        hi = jnp.minimum(t - win // 2 + win, seq_len)
        cnt = (hi - lo).astype(F32)
        pooled = (acc / cnt - h[:, cols]).astype(BF16)
        o_ref[:, cols] = _dot(pooled, w_ref[gi]) * sc_ref[:, cols]


def _pool_call(y, mods, pool_w, pool_scale):
    return pl.pallas_call(
        _pool_kernel,
        grid=(N_TOK // TMIX,),
        in_specs=_halo_specs() + [
            pl.BlockSpec((None, SUBLANES, D), lambda i: (_cond_of_tile(i, TMIX), 0, 0)),
            pl.BlockSpec((4, POOL_C, POOL_C), lambda i: (0, 0, 0)),
            pl.BlockSpec((1, D), lambda i: (0, 0)),
        ],
        out_specs=pl.BlockSpec((TMIX, D), lambda i: (i, 0)),
        out_shape=jax.ShapeDtypeStruct((N_TOK, D), F32),
        compiler_params=_cparams(("parallel",)),
        name="pool",
    )(y, y, y, mods, pool_w, pool_scale)


def _rope_tables():
    rows = DEC_SEQ // GRID_W
    row = jnp.repeat(jnp.arange(rows, dtype=F32), GRID_W)
    col = jnp.tile(jnp.arange(GRID_W, dtype=F32), rows)
    n_freq = ROPE // 4
    inv_freq = ROPE_THETA ** (-jnp.arange(n_freq, dtype=F32) / n_freq)
    ang = jnp.stack([row[:, None] * inv_freq, col[:, None] * inv_freq], axis=1)
    cos, sin = jnp.cos(ang), jnp.sin(ang)
    cos32 = jnp.concatenate([cos, cos], axis=-1).reshape(DEC_SEQ, ROPE)
    sin32 = jnp.concatenate([-sin, sin], axis=-1).reshape(DEC_SEQ, ROPE)
    ones = jnp.ones((DEC_SEQ, NOPE), F32)
    zeros = jnp.zeros((DEC_SEQ, NOPE), F32)
    tail = HEAD_PAD - NOPE - ROPE
    cos_t = jnp.concatenate([ones, cos32, jnp.zeros((DEC_SEQ, tail), F32)], axis=1)
    sin_t = jnp.concatenate([zeros, sin32, jnp.zeros((DEC_SEQ, tail), F32)], axis=1)
    return cos_t, sin_t


def _swap_halves(w):
    s = w.reshape(*w.shape[:-1], 2, 2, ROPE // 4)
    return s[..., ::-1, :].reshape(w.shape)


def _pad_heads(w):
    k, _, width = w.shape
    return jnp.pad(w, ((0, 0), (0, 0), (0, HEAD_PAD - width))).reshape(k, HEADS * HEAD_PAD)


def _pad_heads_alternating(w):
    k = w.shape[0]
    pair = w.reshape(k, HEADS // 2, 2, VDIM)
    z = jnp.zeros((k, HEADS // 2, VDIM), w.dtype)
    return jnp.concatenate([pair[:, :, 0], z, z, pair[:, :, 1]], axis=-1).reshape(k, HEADS * HEAD_PAD)


def _mla_weights(wq_a, q_norm, wq_b, wkv_a, kv_norm, wkv_b, wo, cos_t, sin_t):
    wq_b3 = wq_b.reshape(Q_LORA, HEADS, NOPE + ROPE)
    wq_b4 = jnp.concatenate([wq_b3, _swap_halves(wq_b3[..., NOPE:])], axis=-1)
    w_pe = wkv_a[:, KV_LORA:]
    w_pe4 = jnp.concatenate([jnp.zeros((D, NOPE), F32), w_pe, _swap_halves(w_pe)], axis=1)
    wkv_b3 = wkv_b.reshape(KV_LORA, HEADS, NOPE + VDIM)
    return {
        "w_a": jnp.concatenate([wq_a, wkv_a[:, :KV_LORA], w_pe4], axis=1).astype(BF16),
        "q_norm": q_norm.reshape(1, Q_LORA),
        "wq_b": _pad_heads(wq_b4).astype(BF16),
        "kv_norm": kv_norm.reshape(1, KV_LORA),
        "wkv_b_k": _pad_heads(wkv_b3[..., :NOPE]).astype(BF16),
        "wkv_b_v": _pad_heads_alternating(wkv_b3[..., NOPE:]).astype(BF16),
        "v_one": jnp.asarray(_V_ONE),
        "wo": wo.astype(BF16),
        "cos": cos_t,
        "sin": sin_t,
    }


def _dir_pad(w):
    z = jnp.zeros_like(w[0])
    return jnp.stack([jnp.concatenate([w[0], z], axis=0), jnp.concatenate([z, w[1]], axis=0)])


def _rwkv_weights(mu, wr, wk, wv, w0, w1, w2, a0, a1, a2, g1, g2, k_k, k_a, r_k, lnx_g, lnx_b, wo):
    blk = np.arange(256) // RWKV_N
    return {
        "mu": mu,
        "wr": wr.astype(BF16), "wk": wk.astype(BF16), "wv": wv.astype(BF16),
        "g1": g1.astype(BF16), "g2": g2.astype(BF16),
        "w1": jnp.concatenate([w1[0], w1[1]], axis=1).astype(BF16),
        "w2": _dir_pad(w2).astype(BF16),
        "a1": jnp.concatenate([a1[0], a1[1]], axis=1).astype(BF16),
        "a2": _dir_pad(a2).astype(BF16),
        "w0": w0, "a0": a0,
        "k_k": k_k.reshape(1, D), "k_a": k_a.reshape(1, D),
        "r_k": r_k.reshape(1, D), "lnx_g": lnx_g.reshape(1, D), "lnx_b": lnx_b.reshape(1, D),
        "wo": wo.astype(BF16),
        "ones_bd": jnp.asarray(blk[:, None] == blk[None, :], BF16),
    }


def _mla_layer(y, mods, w, cache_ckv, cache_kpe):
    q, ckv, kpe = _mla_pre_call(y, mods, w)
    cache_kpe_pad = jnp.pad(cache_kpe, ((0, 0), (0, 0), (NOPE, HEAD_PAD - NOPE - ROPE)))
    t_k = PAST + DEC_SEQ
    k_all, v_all = _kv_expand_call(cache_ckv.reshape(DEC_BATCH * PAST, KV_LORA),
                                   cache_kpe_pad.reshape(DEC_BATCH * PAST, HEAD_PAD), ckv, kpe, w)
    o = _attn_call(q, (ckv, kpe, w["wkv_b_k"], w["wkv_b_v"], w["v_one"]), None, None,
                   BATCH, SEQ, SEQ, 0, 0, "attn_prompt")
    o = _attn_call(q, k_all, v_all, o, DEC_BATCH, DEC_SEQ, t_k, N_PROMPT, 0, "attn_sample")
    new_ckv = ckv[:N_PROMPT].reshape(BATCH, SEQ, KV_LORA)
    new_kpe = kpe[:N_PROMPT, NOPE:NOPE + ROPE].reshape(BATCH, SEQ, ROPE)
    return o, new_ckv, new_kpe


def _rwkv_layer(y, mods, w, state):
    r, v, g, kk, kd, b, ld, bonus = _rwkv_pre_call(y, mods, w)
    y_fb, s_p = _scan_call(r, v, kk, kd, b, ld, None, None, BATCH, SEQ, 0, "scan_prompt")
    y_fb, _ = _scan_call(r, v, kk, kd, b, ld, state, y_fb, DEC_BATCH, DEC_SEQ, N_PROMPT, "scan_sample")
    return (y_fb[0], y_fb[1], bonus, g, w["lnx_g"], w["lnx_b"], w["ones_bd"]), s_p


def kernel(x_prompt, x_sample, cache_ckv, cache_kpe, state_wkv, c, c_ctx, ada_w, ada_b, ln_g, ln_b, ffn_wg, ffn_wu, ffn_wd, mla_wq_a, mla_q_norm, mla_wq_b, mla_wkv_a, mla_kv_norm, mla_wkv_b, mla_wo, rwkv_mu, rwkv_wr, rwkv_wk, rwkv_wv, rwkv_w0, rwkv_w1, rwkv_w2, rwkv_a0, rwkv_a1, rwkv_a2, rwkv_g1, rwkv_g2, rwkv_k_k, rwkv_k_a, rwkv_r_k, rwkv_lnx_g, rwkv_lnx_b, rwkv_wo, pool_w, pool_scale):
    y = (x_prompt.reshape(N_PROMPT, D), x_sample.reshape(N_SAMPLE, D))
    cond8 = jnp.concatenate([c_ctx[None, :], c, jnp.zeros((SUBLANES - 1 - DEC_BATCH, D), F32)], axis=0)
    mods_all = _ada_call(cond8, ada_w, ada_b)
    mods_all = jnp.pad(mods_all.reshape(DEPTH, SUBLANES, 6, D)[:, :1 + DEC_BATCH],
                       ((0, 0), (0, 0), (0, SUBLANES - 6), (0, 0)))
    cos_t, sin_t = _rope_tables()
    new_ckv, new_kpe, new_wkv = [], [], []
    for layer in range(DEPTH):
        kind, j = layer % 3, layer // 3
        mods = mods_all[layer]
        wo = None
        if kind == 0:
            w = _mla_weights(mla_wq_a[j], mla_q_norm[j], mla_wq_b[j], mla_wkv_a[j], mla_kv_norm[j],
                             mla_wkv_b[j], mla_wo[j], cos_t, sin_t)
            a, ckv, kpe = _mla_layer(y, mods, w, cache_ckv[:, j], cache_kpe[:, j])
            new_ckv.append(ckv)
            new_kpe.append(kpe)
            wo = w["wo"]
        elif kind == 1:
            w = _rwkv_weights(rwkv_mu[j], rwkv_wr[j], rwkv_wk[j], rwkv_wv[j], rwkv_w0[j], rwkv_w1[j],
                              rwkv_w2[j], rwkv_a0[j], rwkv_a1[j], rwkv_a2[j], rwkv_g1[j], rwkv_g2[j],
                              rwkv_k_k[j], rwkv_k_a[j], rwkv_r_k[j], rwkv_lnx_g[j], rwkv_lnx_b[j], rwkv_wo[j])
            a, s_new = _rwkv_layer(y, mods, w, state_wkv[:, j])
            new_wkv.append(s_new)
            wo = w["wo"]
        else:
            a = _pool_call(y, mods, pool_w[j].astype(BF16), pool_scale[j].reshape(1, D))
        if isinstance(a, tuple):
            ffn_w = [w_[layer:layer + 1].astype(BF16) for w_ in (ffn_wg, ffn_wu, ffn_wd)]
            y = _post_call(y, a, mods, ln_g[layer], ln_b[layer], wo, *ffn_w, 0)
        else:
            y = _post_call(y, a, mods, ln_g[layer], ln_b[layer], wo, ffn_wg, ffn_wu, ffn_wd, layer,
                           split_out=layer == DEPTH - 1)
    return (y[0].reshape(BATCH, SEQ, D), y[1].reshape(DEC_BATCH, DEC_SEQ, D),
            jnp.stack(new_ckv, axis=1), jnp.stack(new_kpe, axis=1), jnp.stack(new_wkv, axis=1))
```
